```python
import jax, jax.numpy as jnp
from jax import lax
import numpy as np

D_MODEL = 1024
BATCH = 8
SEQ = 4096
DEPTH = 1

GRID_W = 64
CTX_LEN = 256

POOL_GROUPS = 4
POOL_WINDOWS = (2, 4, 8, 16)
POOL_WIDTH = D_MODEL // 2
POOL_GROUP_W = POOL_WIDTH // POOL_GROUPS

MLSTM_WIDTH = D_MODEL
MLSTM_HEADS = 4
MLSTM_HEAD_DIM = MLSTM_WIDTH // MLSTM_HEADS
MLSTM_CHUNK = 64
CONV_W = 5
N_DIRS = 2
N_BRANCHES = 2

N_EXPERTS = 16
EXPERT_FF = D_MODEL
EC_CAPACITY = 2

NORM_EPS = 1e-6

POOL_OFF = 0
Q_OFF = POOL_OFF + POOL_WIDTH
K_OFF = Q_OFF + MLSTM_WIDTH
V_OFF = K_OFF + MLSTM_WIDTH
O_OFF = V_OFF + MLSTM_WIDTH
IF_OFF = O_OFF + MLSTM_WIDTH
GATE_OFF = IF_OFF + N_DIRS * 2 * MLSTM_HEADS
IN_WIDTH = GATE_OFF + N_BRANCHES * D_MODEL

kernel_name = "hybrid_pool_mlstm_ec_moe_dit"


def rmsnorm(x, g):
    xf = x.astype(jnp.float32)
    y = xf * lax.rsqrt(jnp.mean(xf * xf, axis=-1, keepdims=True) + NORM_EPS)
    return (y * g.astype(jnp.float32)).astype(x.dtype)


def modulate(h, shift, scale):
    return h * (1 + scale) + shift


def short_conv(u, w, b):
    pad = CONV_W // 2
    y = lax.conv_general_dilated(u, w[:, None, :].astype(u.dtype), window_strides=(1,),
                                 padding=[(pad, pad)], dimension_numbers=('NWC', 'WIO', 'NWC'),
                                 feature_group_count=u.shape[-1])
    return y + b.astype(u.dtype)


def grid_box_mean(u, side):
    _, R, W, _ = u.shape
    sat = jnp.cumsum(jnp.cumsum(u.astype(jnp.float32), axis=1), axis=2)
    sat = jnp.pad(sat, ((0, 0), (1, 0), (1, 0), (0, 0)))
    lo, hi = side // 2, side - side // 2
    r = jnp.arange(R)
    col = jnp.arange(W)
    r0, r1 = jnp.clip(r - lo, 0, R), jnp.clip(r + hi, 0, R)
    c0, c1 = jnp.clip(col - lo, 0, W), jnp.clip(col + hi, 0, W)

    def corner(ri, ci):
        return jnp.take(jnp.take(sat, ri, axis=1), ci, axis=2)

    s = corner(r1, c1) - corner(r0, c1) - corner(r1, c0) + corner(r0, c0)
    cnt = ((r1 - r0)[:, None] * (c1 - c0)[None, :]).astype(jnp.float32)
    return (s / cnt[None, :, :, None]).astype(u.dtype)


def pool_branch(u, mix, scale, rows):
    B_, T, _ = u.shape
    g = u.reshape(B_, rows, GRID_W, POOL_GROUPS, POOL_GROUP_W)
    outs = [grid_box_mean(g[..., i, :], s) - g[..., i, :] for i, s in enumerate(POOL_WINDOWS)]
    p = jnp.stack(outs, axis=-2)
    p = jnp.einsum('brwgc,gcd->brwgd', p, mix)
    return p.reshape(B_, T, POOL_WIDTH) * scale


def split_heads(a):
    B_, T, _ = a.shape
    return a.reshape(B_, T, MLSTM_HEADS, MLSTM_HEAD_DIM).transpose(0, 2, 1, 3)


def mlstm_gates(g, b_if):
    B_, T, _ = g.shape
    g = (g.astype(jnp.float32) + b_if.astype(jnp.float32)).reshape(B_, T, N_DIRS, 2, MLSTM_HEADS)
    g = g.transpose(2, 3, 0, 4, 1)
    return g[:, 0], jax.nn.log_sigmoid(g[:, 1])


def rev_time(a):
    return jnp.flip(a, axis=2)


def mlstm_final_state(k, v, log_i, log_f):
    k = k.astype(jnp.float32)
    v = v.astype(jnp.float32)
    F = jnp.cumsum(log_f, axis=-1)
    w = F[..., -1:] - F + log_i
    m = jnp.max(w, axis=-1)
    wk = jnp.exp(w - m[..., None])[..., None] * k
    C = jnp.einsum('bhsd,bhse->bhde', wk, v)
    n = jnp.sum(wk, axis=-2)
    return C, n, m


def mlstm_chunkwise(q, k, v, log_i, log_f, C0, n0, m0):
    out_dtype = q.dtype
    B_, H, T, dh = q.shape
    L = MLSTM_CHUNK
    nc = T // L

    def chunks(a):
        a = a.astype(jnp.float32)
        return jnp.moveaxis(a.reshape(B_, H, nc, L, *a.shape[3:]), 2, 0)

    xs = (chunks(q), chunks(k), chunks(v), chunks(log_i), chunks(log_f))
    lower = jnp.tril(jnp.ones((L, L), dtype=bool))

    def step(carry, inp):
        C, n, m = carry
        qb, kb, vb, li, lf = inp
        b = jnp.cumsum(lf, axis=-1)
        Dm = jnp.where(lower, b[..., :, None] - b[..., None, :] + li[..., None, :], -jnp.inf)
        inter = b + m[..., None]
        mt = jnp.maximum(inter, jnp.max(Dm, axis=-1))
        w_inter = jnp.exp(inter - mt)
        s = jnp.einsum('bhtd,bhsd->bhts', qb, kb) * jnp.exp(Dm - mt[..., None])
        num = w_inter[..., None] * jnp.einsum('bhtd,bhde->bhte', qb, C) + jnp.einsum('bhts,bhse->bhte', s, vb)
        den = w_inter * jnp.einsum('bhtd,bhd->bht', qb, n) + jnp.sum(s, axis=-1)
        h = num / jnp.maximum(jnp.abs(den), jnp.exp(-mt))[..., None]
        g = b[..., -1]
        a = g[..., None] - b + li
        m_new = jnp.maximum(g + m, jnp.max(a, axis=-1))
        decay = jnp.exp(g + m - m_new)
        wk = jnp.exp(a - m_new[..., None])[..., None] * kb
        C_new = decay[..., None, None] * C + jnp.einsum('bhsd,bhse->bhde', wk, vb)
        n_new = decay[..., None] * n + jnp.sum(wk, axis=-2)
        return (C_new, n_new, m_new), h

    _, hs = lax.scan(step, (C0, n0, m0), xs)
    return jnp.moveaxis(hs, 0, 2).reshape(B_, H, T, dh).astype(out_dtype)


def mlstm_context_states(hc, w_in, conv_w, conv_b, b_if):
    k = jax.nn.silu(short_conv(hc @ w_in[:, K_OFF:V_OFF], conv_w[:, MLSTM_WIDTH:], conv_b[MLSTM_WIDTH:]))
    k = split_heads(k) * MLSTM_HEAD_DIM ** -0.5
    v = split_heads(hc @ w_in[:, V_OFF:O_OFF])
    log_i, log_f = mlstm_gates(hc @ w_in[:, IF_OFF:GATE_OFF], b_if)
    fwd = mlstm_final_state(k, v, log_i[0], log_f[0])
    bwd = mlstm_final_state(rev_time(k), rev_time(v), rev_time(log_i[1]), rev_time(log_f[1]))
    return fwd, bwd


def mlstm_branch(proj, conv_w, conv_b, b_if, norm_g, ctx_fwd, ctx_bwd):
    B_, T, _ = proj.shape
    qk = jax.nn.silu(short_conv(proj[..., Q_OFF:V_OFF], conv_w, conv_b))
    q = split_heads(qk[..., :MLSTM_WIDTH])
    k = split_heads(qk[..., MLSTM_WIDTH:]) * MLSTM_HEAD_DIM ** -0.5
    v = split_heads(proj[..., V_OFF:O_OFF])
    log_i, log_f = mlstm_gates(proj[..., IF_OFF:GATE_OFF], b_if)
    h_fwd = mlstm_chunkwise(q, k, v, log_i[0], log_f[0], *ctx_fwd)
    h_bwd = rev_time(mlstm_chunkwise(rev_time(q), rev_time(k), rev_time(v),
                                     rev_time(log_i[1]), rev_time(log_f[1]), *ctx_bwd))
    h = (h_fwd + h_bwd).transpose(0, 2, 1, 3)
    h = rmsnorm(h, norm_g.reshape(MLSTM_HEADS, MLSTM_HEAD_DIM)).reshape(B_, T, MLSTM_WIDTH)
    return h * jax.nn.sigmoid(proj[..., O_OFF:IF_OFF])


def expert_choice_ffn(h, w_router, w_gate, w_up, w_down):
    B_, T, D = h.shape
    cap = EC_CAPACITY * T // N_EXPERTS
    aff = jax.nn.softmax((h @ w_router).astype(jnp.float32), axis=-1)
    top_aff, top_idx = lax.top_k(jnp.swapaxes(aff, 1, 2), cap)
    xe = jax.vmap(lambda hb, ib: hb[ib])(h, top_idx)
    hid = jax.nn.silu(jnp.einsum('becd,edf->becf', xe, w_gate)) * jnp.einsum('becd,edf->becf', xe, w_up)
    ye = jnp.einsum('becf,efd->becd', hid, w_down) * top_aff[..., None].astype(h.dtype)
    return jax.vmap(lambda yb, ib: jnp.zeros((T, D), yb.dtype).at[ib.reshape(-1)].add(yb.reshape(-1, D)))(ye, top_idx)


def setup_inputs(seed: int = 0) -> dict:
    key = jax.random.key(seed)
    ks = jax.random.split(key, 24)
    D = D_MODEL
    f32 = jnp.float32

    def nrm(k, shape, fan_in):
        return jax.random.normal(k, shape, f32) * fan_in ** -0.5

    def gain(k, shape):
        return 1.0 + 0.02 * jax.random.normal(k, shape, f32)

    b_if = 0.1 * jax.random.normal(ks[10], (DEPTH, N_DIRS, 2, MLSTM_HEADS), f32)
    b_if = b_if.at[:, :, 1, :].add(jnp.linspace(3.0, 6.0, MLSTM_HEADS, dtype=f32))
    return {
        "x": jax.random.normal(ks[0], (BATCH, SEQ, D), f32),
        "c": jax.random.normal(ks[1], (BATCH, D), f32),
        "ctx": jax.random.normal(ks[2], (BATCH, CTX_LEN, D), f32),
        "c_ctx": jax.random.normal(ks[3], (D,), f32),
        "w_mod": nrm(ks[4], (DEPTH, D, 6 * D), D),
        "b_mod": 0.02 * jax.random.normal(ks[5], (DEPTH, 6 * D), f32),
        "norm1_g": gain(ks[6], (DEPTH, D)),
        "norm2_g": gain(ks[7], (DEPTH, D)),
        "w_in": nrm(ks[8], (DEPTH, D, IN_WIDTH), D),
        "conv_w": nrm(ks[9], (DEPTH, CONV_W, 2 * MLSTM_WIDTH), CONV_W),
        "conv_b": 0.02 * jax.random.normal(ks[11], (DEPTH, 2 * MLSTM_WIDTH), f32),
        "b_if": b_if.reshape(DEPTH, N_DIRS * 2 * MLSTM_HEADS),
        "pool_mix": nrm(ks[12], (DEPTH, POOL_GROUPS, POOL_GROUP_W, POOL_GROUP_W), POOL_GROUP_W),
        "pool_scale": 1.0 + 0.1 * jax.random.normal(ks[13], (DEPTH, POOL_WIDTH), f32),
        "mlstm_norm_g": gain(ks[14], (DEPTH, MLSTM_WIDTH)),
        "w_pool_out": nrm(ks[15], (DEPTH, POOL_WIDTH, D), POOL_WIDTH),
        "w_mlstm_out": nrm(ks[16], (DEPTH, MLSTM_WIDTH, D), MLSTM_WIDTH),
        "w_out": nrm(ks[17], (DEPTH, D, D), D),
        "w_router": nrm(ks[18], (DEPTH, D, N_EXPERTS), D),
        "w_gate": nrm(ks[19], (DEPTH, N_EXPERTS, D, EXPERT_FF), D),
        "w_up": nrm(ks[20], (DEPTH, N_EXPERTS, D, EXPERT_FF), D),
        "w_down": nrm(ks[21], (DEPTH, N_EXPERTS, EXPERT_FF, D), EXPERT_FF),
        "final_g": gain(ks[22], (D,)),
    }


def reference(x, c, ctx, c_ctx, w_mod, b_mod, norm1_g, norm2_g, w_in, conv_w, conv_b, b_if,
              pool_mix, pool_scale, mlstm_norm_g, w_pool_out, w_mlstm_out, w_out,
              w_router, w_gate, w_up, w_down, final_g):
    D = D_MODEL
    rows = x.shape[1] // GRID_W
    for l in range(DEPTH):
        mod = jax.nn.silu(c) @ w_mod[l] + b_mod[l]
        shift1, scale1, gate1, shift2, scale2, gate2 = jnp.split(mod[:, None, :], 6, axis=-1)
        mod_c = jax.nn.silu(c_ctx) @ w_mod[l] + b_mod[l]

        hc = modulate(rmsnorm(ctx, norm1_g[l]), mod_c[:D], mod_c[D:2 * D])
        ctx_fwd, ctx_bwd = mlstm_context_states(hc, w_in[l], conv_w[l], conv_b[l], b_if[l])

        hx = modulate(rmsnorm(x, norm1_g[l]), shift1, scale1)
        proj = hx @ w_in[l]
        a = pool_branch(proj[..., POOL_OFF:Q_OFF], pool_mix[l], pool_scale[l], rows)
        m = mlstm_branch(proj, conv_w[l], conv_b[l], b_if[l], mlstm_norm_g[l], ctx_fwd, ctx_bwd)
        g = jax.nn.sigmoid(proj[..., GATE_OFF:])
        mixed = g[..., :D] * (a @ w_pool_out[l]) + g[..., D:] * (m @ w_mlstm_out[l])
        x = x + gate1 * (mixed @ w_out[l])

        h2 = modulate(rmsnorm(x, norm2_g[l]), shift2, scale2)
        x = x + gate2 * expert_choice_ffn(h2, w_router[l], w_gate[l], w_up[l], w_down[l])
    return rmsnorm(x, final_g)
```

```python
import functools

import jax
import jax.numpy as jnp
from jax import lax
from jax.experimental import pallas as pl
from jax.experimental.pallas import tpu as pltpu

F32 = jnp.float32
BF16 = jnp.bfloat16

GRID_W = 64
POOL_WINDOWS = (2, 4, 8, 16)
N_HEADS = 4
CONV_W = 5
N_DIRS = 2
N_EXPERTS = 16
EC_CAPACITY = 2
NORM_EPS = 1e-6

CHUNK = 256
TOKEN_TILE = 512
HALO = 16
LANES = 128
V7X_VMEM_LIMIT_BYTES = 56 * 1024 * 1024

NN = (((1,), (0,)), ((), ()))
NT = (((1,), (1,)), ((), ()))
TN = (((0,), (0,)), ((), ()))


def _dot(a, b, dims=NN):
    return lax.dot_general(a, b, dims, preferred_element_type=F32)


def _split2(a):
    hi = a.astype(BF16)
    lo = (a - hi.astype(F32)).astype(BF16)
    return hi, lo


def _split3(a):
    a1 = a.astype(BF16)
    r1 = a - a1.astype(F32)
    a2 = r1.astype(BF16)
    a3 = (r1 - a2.astype(F32)).astype(BF16)
    return a1, a2, a3


def _dot3(a, b, dims=NN):
    ah, al = _split2(a)
    bh, bl = _split2(b)
    return _dot(ah, bh, dims) + _dot(ah, bl, dims) + _dot(al, bh, dims)


def _dot_left01(t01, a):
    a1, a2, a3 = _split3(a)
    return _dot(t01, a1) + _dot(t01, a2) + _dot(t01, a3)


def _dot_right01(a, t01):
    a1, a2, a3 = _split3(a)
    return _dot(a1, t01) + _dot(a2, t01) + _dot(a3, t01)


def _silu(x):
    return x * jax.nn.sigmoid(x)


def _log_sigmoid(x):
    return jnp.minimum(x, 0.0) - jnp.log1p(jnp.exp(-jnp.abs(x)))


def _rms_scale(x):
    return lax.rsqrt(jnp.mean(x * x, axis=-1, keepdims=True) + NORM_EPS)


def _tri01(n, kind):
    i = lax.broadcasted_iota(jnp.int32, (n, n), 0)
    j = lax.broadcasted_iota(jnp.int32, (n, n), 1)
    cond = {"le": j <= i, "ge": j >= i, "lt": j < i, "gt": j > i}[kind]
    return jnp.where(cond, 1.0, 0.0).astype(BF16)


def _shift(n):
    assert n & (n - 1) == 0, n
    return n.bit_length() - 1


def _div_pow2(x, n):
    return lax.shift_right_logical(x, _shift(n))


def _mod_pow2(x, n):
    return jnp.bitwise_and(x, n - 1)


def _params(*sem):
    return pltpu.CompilerParams(dimension_semantics=sem, vmem_limit_bytes=V7X_VMEM_LIMIT_BYTES)


def _resident(shape):
    nd = len(shape)
    return pl.BlockSpec(shape, lambda *_: (0,) * nd)


def _mod_kernel(c_ref, w_ref, b_ref, o_ref):
    o_ref[...] = _dot3(_silu(c_ref[...]), w_ref[...]) + b_ref[...]


def _mod_call(cvec, w_mod, b_mod):
    rows, d = cvec.shape
    n = w_mod.shape[1]
    tn = 1536
    return pl.pallas_call(
        _mod_kernel,
        grid=(n // tn,),
        in_specs=[pl.BlockSpec((rows, d), lambda j: (0, 0)),
                  pl.BlockSpec((d, tn), lambda j: (0, j)),
                  pl.BlockSpec((1, tn), lambda j: (0, j))],
        out_specs=pl.BlockSpec((rows, tn), lambda j: (0, j)),
        out_shape=jax.ShapeDtypeStruct((rows, n), F32),
        compiler_params=_params("parallel"),
        name="mod",
    )(cvec, w_mod, b_mod)


def _ctx_kernel(ctx_ref, sh_ref, sc_ref, g_ref, wk_ref, wv_ref, wif_ref, cw_ref, cb_ref, bif_ref,
                c_out, n_out, m_out):
    lc, d = ctx_ref.shape
    dh = d // N_HEADS
    x = ctx_ref[...]
    hc = (x * _rms_scale(x) * g_ref[...]) * (1.0 + sc_ref[...]) + sh_ref[...]
    hcb = hc.astype(BF16)

    kpre = _dot(hcb, wk_ref[...])
    pad = jnp.zeros((8, d), F32)
    kp = jnp.concatenate([pad, kpre, pad], axis=0)
    cw = cw_ref[...]
    acc = cb_ref[...] + cw[0:1, :] * kp[6:6 + lc, :]
    for j in range(1, CONV_W):
        acc = acc + cw[j:j + 1, :] * kp[6 + j:6 + j + lc, :]
    k = _silu(acc) * (dh ** -0.5)
    v = _dot(hcb, wv_ref[...]).astype(BF16)

    g = _dot(hcb, wif_ref[...]) + bif_ref[...]
    lf = _log_sigmoid(g)
    suf = _dot_left01(_tri01(lc, "gt"), lf)
    pre = _dot_left01(_tri01(lc, "lt"), lf)
    for dr in range(N_DIRS):
        for h in range(N_HEADS):
            li = g[:, dr * 8 + h:dr * 8 + h + 1]
            fc = dr * 8 + 4 + h
            w = (suf if dr == 0 else pre)[:, fc:fc + 1] + li
            m = jnp.max(w, axis=0, keepdims=True)
            wk = jnp.exp(w - m) * k[:, h * dh:(h + 1) * dh]
            c_out[dr, h] = _dot(wk.astype(BF16), v[:, h * dh:(h + 1) * dh], TN)
            n_out[dr, h] = jnp.sum(wk, axis=0, keepdims=True)
            m_out[dr, h] = jnp.broadcast_to(m, (1, LANES))


def _ctx_call(ctx, sh_c, sc_c, g1, wk, wv, wif, cw_k, cb_k, bif):
    b, lc, d = ctx.shape
    dh = d // N_HEADS
    row = lambda w: pl.BlockSpec((1, w), lambda i: (0, 0))
    return pl.pallas_call(
        _ctx_kernel,
        grid=(b,),
        in_specs=[pl.BlockSpec((None, lc, d), lambda i: (i, 0, 0)),
                  row(d), row(d), row(d),
                  pl.BlockSpec((d, d), lambda i: (0, 0)),
                  pl.BlockSpec((d, d), lambda i: (0, 0)),
                  pl.BlockSpec((d, LANES), lambda i: (0, 0)),
                  pl.BlockSpec((CONV_W, d), lambda i: (0, 0)),
                  row(d), row(LANES)],
        out_specs=[pl.BlockSpec((None, N_DIRS, N_HEADS, dh, dh), lambda i: (i, 0, 0, 0, 0)),
                   pl.BlockSpec((None, N_DIRS, N_HEADS, 1, dh), lambda i: (i, 0, 0, 0, 0)),
                   pl.BlockSpec((None, N_DIRS, N_HEADS, 1, LANES), lambda i: (i, 0, 0, 0, 0))],
        out_shape=[jax.ShapeDtypeStruct((b, N_DIRS, N_HEADS, dh, dh), F32),
                   jax.ShapeDtypeStruct((b, N_DIRS, N_HEADS, 1, dh), F32),
                   jax.ShapeDtypeStruct((b, N_DIRS, N_HEADS, 1, LANES), F32)],
        compiler_params=_params("parallel"),
        name="ctx_states",
    )(ctx, sh_c, sc_c, g1, wk, wv, wif, cw_k, cb_k, bif)


def _proj_kernel(xp_ref, x_ref, xn_ref, sh_ref, sc_ref, g_ref,
                 wpool_ref, wqk_ref, wv_ref, wo_ref, wg_ref, wif_ref, wift_ref,
                 cw_ref, cb_ref, bif_ref, bift_ref,
                 u_out, q_out, k_out, v_out, og_out, gg_out, gc_out, gb0_out, gb1_out, grow_out,
                 hx_scr):
    tm, d = x_ref.shape
    dh = d // N_HEADS
    i = pl.program_id(1)
    last = pl.num_programs(1) - 1

    x_ext = jnp.concatenate([xp_ref[...], x_ref[...], xn_ref[...]], axis=0)
    hx = (x_ext * _rms_scale(x_ext) * g_ref[...]) * (1.0 + sc_ref[...]) + sh_ref[...]
    hx_scr[...] = hx.astype(BF16)
    hxc = hx_scr[HALO:HALO + tm, :]

    nc = 512
    u_out[...] = _dot(hxc, wpool_ref[...]).astype(BF16)

    r_id = lax.broadcasted_iota(jnp.int32, (tm + 2 * HALO, 1), 0)
    valid = jnp.logical_and(jnp.logical_or(i > 0, r_id >= HALO),
                            jnp.logical_or(i < last, r_id < HALO + tm))
    base = HALO - CONV_W // 2
    for c in range(2 * d // nc):
        cols = slice(c * nc, (c + 1) * nc)
        r = jnp.where(valid, _dot(hx_scr[...], wqk_ref[:, cols]), 0.0)
        cw = cw_ref[:, cols]
        acc = cb_ref[:, cols] + cw[0:1, :] * r[base:base + tm, :]
        for j in range(1, CONV_W):
            acc = acc + cw[j:j + 1, :] * r[base + j:base + j + tm, :]
        y = _silu(acc)
        if c * nc < d:
            q_out[:, cols] = y.astype(BF16)
        else:
            k_out[:, c * nc - d:(c + 1) * nc - d] = (y * (dh ** -0.5)).astype(BF16)

    for c in range(d // nc):
        cols = slice(c * nc, (c + 1) * nc)
        v_out[:, cols] = _dot(hxc, wv_ref[:, cols]).astype(BF16)
        og_out[:, cols] = jax.nn.sigmoid(_dot(hxc, wo_ref[:, cols])).astype(BF16)
    for c in range(2 * d // nc):
        cols = slice(c * nc, (c + 1) * nc)
        gg_out[:, cols] = jax.nn.sigmoid(_dot(hxc, wg_ref[:, cols])).astype(BF16)

    ng = N_DIRS * 2 * N_HEADS
    gcol = _dot(hxc, wif_ref[...]) + bif_ref[...]
    lane = lax.broadcasted_iota(jnp.int32, gcol.shape, 1)
    gcol = jnp.where(_mod_pow2(_div_pow2(lane, N_HEADS), 2) == 1, _log_sigmoid(gcol), gcol)
    grow = _dot(wift_ref[...], hxc, NT) + bift_ref[...]
    sub = lax.broadcasted_iota(jnp.int32, grow.shape, 0)
    grow = jnp.where(_mod_pow2(_div_pow2(sub, N_HEADS), 2) == 1, _log_sigmoid(grow), grow)
    gc_out[...] = gcol[:, :ng]
    grow_out[0:ng, :] = grow
    t_le, t_ge = _tri01(CHUNK, "le"), _tri01(CHUNK, "ge")
    for j in range(tm // CHUNK):
        rows = slice(j * CHUNK, (j + 1) * CHUNK)
        gb0_out[rows, :] = _dot_left01(t_le, gcol[rows, :])[:, :ng]
        gb1_out[rows, :] = _dot_left01(t_ge, gcol[rows, :])[:, :ng]
        grow_out[ng:2 * ng, rows] = _dot_right01(grow[:, rows], t_ge)
        grow_out[2 * ng:3 * ng, rows] = _dot_right01(grow[:, rows], t_le)


def _proj_call(x, sh, sc, g1, wpool, wqk, wv, wo, wg, wif, wift, cw, cb, bif, bift):
    b, t, d = x.shape
    tm = TOKEN_TILE
    nt = t // tm
    hb = tm // HALO
    ng = N_DIRS * 2 * N_HEADS
    per_b = pl.BlockSpec((None, 1, d), lambda bi, i: (bi, 0, 0))
    tile = lambda w: pl.BlockSpec((None, tm, w), lambda bi, i: (bi, i, 0))
    const = lambda a: pl.BlockSpec(a.shape, lambda bi, i: (0,) * a.ndim)
    out_shapes = [jax.ShapeDtypeStruct((b, t, d // 2), BF16),
                  jax.ShapeDtypeStruct((b, t, d), BF16),
                  jax.ShapeDtypeStruct((b, t, d), BF16),
                  jax.ShapeDtypeStruct((b, t, d), BF16),
                  jax.ShapeDtypeStruct((b, t, d), BF16),
                  jax.ShapeDtypeStruct((b, t, 2 * d), BF16),
                  jax.ShapeDtypeStruct((b, t, ng), F32),
                  jax.ShapeDtypeStruct((b, t, ng), F32),
                  jax.ShapeDtypeStruct((b, t, ng), F32),
                  jax.ShapeDtypeStruct((b, 3 * ng, t), F32)]
    out_specs = [tile(d // 2), tile(d), tile(d), tile(d), tile(d), tile(2 * d),
                 tile(ng), tile(ng), tile(ng),
                 pl.BlockSpec((None, 3 * ng, tm), lambda bi, i: (bi, 0, i))]
    return pl.pallas_call(
        _proj_kernel,
        grid=(b, nt),
        in_specs=[pl.BlockSpec((None, HALO, d), lambda bi, i: (bi, jnp.maximum(i * hb - 1, 0), 0)),
                  tile(d),
                  pl.BlockSpec((None, HALO, d), lambda bi, i: (bi, jnp.minimum((i + 1) * hb, t // HALO - 1), 0)),
                  per_b, per_b, const(g1),
                  const(wpool), const(wqk), const(wv), const(wo), const(wg), const(wif), const(wift),
                  const(cw), const(cb), const(bif), const(bift)],
        out_specs=out_specs,
        out_shape=out_shapes,
        scratch_shapes=[pltpu.VMEM((tm + 2 * HALO, d), BF16)],
        compiler_params=_params("parallel", "parallel"),
        name="proj",
    )(x, x, x, sh, sc, g1, wpool, wqk, wv, wo, wg, wif, wift, cw, cb, bif, bift)


def _pool_kernel(u_ref, mix_ref, scale_ref, p_out, pad_scr):
    t, pw = u_ref.shape
    gw = pw // len(POOL_WINDOWS)
    rows = t // GRID_W
    tile = 256
    maxlo = max(POOL_WINDOWS) // 2
    padr = maxlo * GRID_W
    pad_scr[0:padr, :] = jnp.zeros((padr, gw), F32)
    pad_scr[padr + t:padr + t + padr, :] = jnp.zeros((padr, gw), F32)

    ti = lax.broadcasted_iota(jnp.int32, (tile, tile), 0)
    tj = lax.broadcasted_iota(jnp.int32, (tile, tile), 1)
    same_row = _div_pow2(ti, GRID_W) == _div_pow2(tj, GRID_W)
    ci, cj = _mod_pow2(ti, GRID_W), _mod_pow2(tj, GRID_W)
    tok = lax.broadcasted_iota(jnp.int32, (t, gw), 0)
    r_id, c_id = _div_pow2(tok, GRID_W), _mod_pow2(tok, GRID_W)

    for g, side in enumerate(POOL_WINDOWS):
        lo, hi = side // 2, side - side // 2
        cols = slice(g * gw, (g + 1) * gw)
        band = jnp.logical_and(same_row, jnp.logical_and(cj >= ci - lo, cj < ci + hi))
        pw01 = jnp.where(band, 1.0, 0.0).astype(BF16)
        for k in range(t // tile):
            rs = slice(k * tile, (k + 1) * tile)
            pad_scr[padr + k * tile:padr + (k + 1) * tile, :] = _dot(pw01, u_ref[rs, cols])
        tot = pad_scr[padr - lo * GRID_W:padr - lo * GRID_W + t, :]
        for dlt in range(-lo + 1, hi):
            tot = tot + pad_scr[padr + dlt * GRID_W:padr + dlt * GRID_W + t, :]
        cnt = ((jnp.minimum(r_id + hi, rows) - jnp.maximum(r_id - lo, 0))
               * (jnp.minimum(c_id + hi, GRID_W) - jnp.maximum(c_id - lo, 0))).astype(F32)
        a = tot / cnt - u_ref[:, cols].astype(F32)
        p = _dot(a.astype(BF16), mix_ref[g]) * scale_ref[:, cols]
        p_out[:, cols] = p.astype(BF16)


def _pool_call(u, mix, scale):
    b, t, pw = u.shape
    gw = pw // len(POOL_WINDOWS)
    padr = (max(POOL_WINDOWS) // 2) * GRID_W
    return pl.pallas_call(
        _pool_kernel,
        grid=(b,),
        in_specs=[pl.BlockSpec((None, t, pw), lambda i: (i, 0, 0)),
                  pl.BlockSpec(mix.shape, lambda i: (0, 0, 0)),
                  pl.BlockSpec((1, pw), lambda i: (0, 0))],
        out_specs=pl.BlockSpec((None, t, pw), lambda i: (i, 0, 0)),
        out_shape=jax.ShapeDtypeStruct((b, t, pw), BF16),
        scratch_shapes=[pltpu.VMEM((t + 2 * padr, gw), F32)],
        compiler_params=_params("parallel"),
        name="pool",
    )(u, mix, scale)


def _mlstm_kernel(q_ref, k_ref, v_ref, og_ref, gc_ref, gb0_ref, gb1_ref, grow_ref,
                  c0_ref, n0_ref, m0_ref, ng_ref, o_ref,
                  c_scr, n_scr, m_scr, hb_scr):
    L, d = q_ref.shape
    dh = d // N_HEADS
    ngate = N_DIRS * 2 * N_HEADS
    s = pl.program_id(1)
    nch = pl.num_programs(1) // 2
    is_bwd = s < nch
    chunk = jnp.where(is_bwd, nch - 1 - s, s - nch)

    @pl.when(s == 0)
    def _():
        for h in range(N_HEADS):
            c_scr[h] = c0_ref[1, h]
            n_scr[h] = n0_ref[1, h]
            m_scr[h] = m0_ref[1, h]

    @pl.when(s == nch)
    def _():
        for h in range(N_HEADS):
            c_scr[h] = c0_ref[0, h]
            n_scr[h] = n0_ref[0, h]
            m_scr[h] = m0_ref[0, h]

    ri = lax.broadcasted_iota(jnp.int32, (L, L), 0)
    cj = lax.broadcasted_iota(jnp.int32, (L, L), 1)
    sgn = jnp.where(is_bwd, 1, -1)
    mask = (cj - ri) * sgn >= 0
    row0 = pl.multiple_of(chunk * L, L)

    for h in range(N_HEADS):
        hs = slice(h * dh, (h + 1) * dh)
        q = q_ref[:, hs]
        k = k_ref[:, hs]
        v = v_ref[:, hs]
        li_c = jnp.where(is_bwd, gc_ref[:, 8 + h:9 + h], gc_ref[:, h:h + 1])
        b_c = jnp.where(is_bwd, gb1_ref[:, 12 + h:13 + h], gb0_ref[:, 4 + h:5 + h])
        li_r = jnp.where(is_bwd, grow_ref[8 + h:9 + h, :], grow_ref[h:h + 1, :])
        b_r = jnp.where(is_bwd, grow_ref[2 * ngate + 12 + h:2 * ngate + 13 + h, :],
                        grow_ref[ngate + 4 + h:ngate + 5 + h, :])
        m_prev = m_scr[h][:, 0:1]
        c_prev = c_scr[h]
        n_prev = n_scr[h]

        dm = jnp.where(mask, b_c + (li_r - b_r), -jnp.inf)
        inter = b_c + m_prev
        mt = jnp.maximum(inter, jnp.max(dm, axis=-1, keepdims=True))
        w_inter = jnp.exp(inter - mt)
        smat = _dot(q, k, NT) * jnp.exp(dm - mt)
        num = w_inter * _dot(q, c_prev.astype(BF16)) + _dot(smat.astype(BF16), v)
        den = (w_inter * jnp.sum(q.astype(F32) * n_prev, axis=-1, keepdims=True)
               + jnp.sum(smat, axis=-1, keepdims=True))
        hh = num / jnp.maximum(jnp.abs(den), jnp.exp(-mt))

        g_tot = jnp.where(is_bwd, b_c[0:1, :], b_c[L - 1:L, :])
        a = g_tot + (li_c - b_c)
        m_new = jnp.maximum(g_tot + m_prev, jnp.max(a, axis=0, keepdims=True))
        decay = jnp.exp(g_tot + m_prev - m_new)
        wk = jnp.exp(a - m_new) * k.astype(F32)
        c_scr[h] = decay * c_prev + _dot(wk.astype(BF16), v, TN)
        n_scr[h] = decay * n_prev + jnp.sum(wk, axis=0, keepdims=True)
        m_scr[h] = jnp.broadcast_to(m_new, (1, LANES))

        @pl.when(is_bwd)
        def _():
            hb_scr[pl.ds(row0, L), hs] = hh

        @pl.when(jnp.logical_not(is_bwd))
        def _():
            ht = hh + hb_scr[pl.ds(row0, L), hs]
            y = ht * _rms_scale(ht) * ng_ref[:, hs]
            o_ref[:, hs] = (y * og_ref[:, hs].astype(F32)).astype(BF16)


def _mlstm_call(q, k, v, og, gc, gb0, gb1, grow, c0, n0, m0, norm_g):
    b, t, d = q.shape
    dh = d // N_HEADS
    L = CHUNK
    nch = t // L
    ngate = N_DIRS * 2 * N_HEADS

    def chunk_of(s):
        return jnp.where(s < nch, nch - 1 - s, s - nch)

    seq = lambda w: pl.BlockSpec((None, L, w), lambda bi, s: (bi, chunk_of(s), 0))
    state = lambda w0, w1: pl.BlockSpec((None, N_DIRS, N_HEADS, w0, w1), lambda bi, s: (bi, 0, 0, 0, 0))
    return pl.pallas_call(
        _mlstm_kernel,
        grid=(b, 2 * nch),
        in_specs=[seq(d), seq(d), seq(d), seq(d), seq(ngate), seq(ngate), seq(ngate),
                  pl.BlockSpec((None, 3 * ngate, L), lambda bi, s: (bi, 0, chunk_of(s))),
                  state(dh, dh), state(1, dh), state(1, LANES),
                  pl.BlockSpec((1, d), lambda bi, s: (0, 0))],
        out_specs=pl.BlockSpec((None, L, d), lambda bi, s: (bi, jnp.maximum(s - nch, 0), 0)),
        out_shape=jax.ShapeDtypeStruct((b, t, d), BF16),
        scratch_shapes=[pltpu.VMEM((N_HEADS, dh, dh), F32),
                        pltpu.VMEM((N_HEADS, 1, dh), F32),
                        pltpu.VMEM((N_HEADS, 1, LANES), F32),
                        pltpu.VMEM((t, d), F32)],
        compiler_params=_params("parallel", "arbitrary"),
        name="mlstm",
    )(q, k, v, og, gc, gb0, gb1, grow, c0, n0, m0, norm_g)


def _merge_kernel(p_ref, m_ref, gg_ref, x_ref, g1_ref, sh2_ref, sc2_ref, n2_ref,
                  wpo_ref, wmo_ref, wout_ref, wr_ref, x1_out, h2_out, aff_out):
    tm, d = x_ref.shape
    a = _dot(p_ref[...], wpo_ref[...])
    mm = _dot(m_ref[...], wmo_ref[...])
    mixed = gg_ref[:, 0:d].astype(F32) * a + gg_ref[:, d:2 * d].astype(F32) * mm
    x1 = x_ref[...] + g1_ref[...] * _dot(mixed.astype(BF16), wout_ref[...])
    x1_out[...] = x1
    h2 = (x1 * _rms_scale(x1) * n2_ref[...]) * (1.0 + sc2_ref[...]) + sh2_ref[...]
    h2_out[...] = h2.astype(BF16)
    logits = _dot3(wr_ref[...], h2, NT)
    z = jnp.exp(logits - jnp.max(logits, axis=0, keepdims=True))
    aff_out[...] = z / jnp.sum(z, axis=0, keepdims=True)


def _merge_call(p, m, gg, x, gate1, sh2, sc2, n2, wpo, wmo, wout, wr_t):
    b, t, d = x.shape
    tm = TOKEN_TILE
    e = wr_t.shape[0]
    per_b = pl.BlockSpec((None, 1, d), lambda bi, i: (bi, 0, 0))
    tile = lambda w: pl.BlockSpec((None, tm, w), lambda bi, i: (bi, i, 0))
    const = lambda a: pl.BlockSpec(a.shape, lambda bi, i: (0,) * a.ndim)
    return pl.pallas_call(
        _merge_kernel,
        grid=(b, t // tm),
        in_specs=[tile(d // 2), tile(d), tile(2 * d), tile(d), per_b, per_b, per_b, const(n2),
                  const(wpo), const(wmo), const(wout), const(wr_t)],
        out_specs=[tile(d), tile(d), pl.BlockSpec((None, e, tm), lambda bi, i: (bi, 0, i))],
        out_shape=[jax.ShapeDtypeStruct((b, t, d), F32),
                   jax.ShapeDtypeStruct((b, t, d), BF16),
                   jax.ShapeDtypeStruct((b, e, t), F32)],
        compiler_params=_params("parallel", "parallel"),
        name="merge",
    )(p, m, gg, x, gate1, sh2, sc2, n2, wpo, wmo, wout, wr_t)


def _route_kernel(aff_ref, slot_out, *, cap):
    e, t = aff_ref.shape
    bits = pltpu.bitcast(aff_ref[...], jnp.int32)

    def step(i, thr):
        cand = thr | (jnp.int32(1) << (30 - i))
        cnt = jnp.sum(jnp.where(bits >= cand, 1.0, 0.0), axis=-1, keepdims=True)
        return jnp.where(cnt >= cap, cand, thr)

    thr = lax.fori_loop(0, 31, step, jnp.zeros((e, 1), jnp.int32))
    gt = bits > thr
    eq = bits == thr
    need = cap - jnp.sum(jnp.where(gt, 1.0, 0.0), axis=-1, keepdims=True).astype(jnp.int32)

    seg = 256
    t_ge = _tri01(seg, "ge")

    def prefix_incl(x01):
        outs, carry = [], jnp.zeros((e, 1), F32)
        for j in range(t // seg):
            p = _dot(x01[:, j * seg:(j + 1) * seg].astype(BF16), t_ge) + carry
            outs.append(p)
            carry = p[:, seg - 1:seg]
        return jnp.concatenate(outs, axis=1)

    eq_f = jnp.where(eq, 1.0, 0.0)
    tie_rank = (prefix_incl(eq_f) - eq_f).astype(jnp.int32)
    sel = jnp.logical_or(gt, jnp.logical_and(eq, tie_rank < need))
    rank = prefix_incl(jnp.where(sel, 1.0, 0.0)).astype(jnp.int32) - 1
    slot_out[...] = jnp.where(sel, rank, -1)


def _route_call(aff_t, cap):
    b, e, t = aff_t.shape
    return pl.pallas_call(
        functools.partial(_route_kernel, cap=cap),
        grid=(b,),
        in_specs=[pl.BlockSpec((None, e, t), lambda i: (i, 0, 0))],
        out_specs=pl.BlockSpec((None, e, t), lambda i: (i, 0, 0)),
        out_shape=jax.ShapeDtypeStruct((b, e, t), jnp.int32),
        compiler_params=_params("parallel"),
        name="route",
    )(aff_t)


def _expert_kernel(h2_ref, slot_ref, aff_ref, wg_ref, wu_ref, wd_ref, ye_out, xe_scr, *, cap):
    t, d = h2_ref.shape
    f = wg_ref.shape[1]
    tc = min(TOKEN_TILE, t)
    s_id = lax.broadcasted_iota(jnp.int32, (cap, tc), 0)
    affs = jnp.zeros((cap, 1), F32)
    for c in range(t // tc):
        cols = slice(c * tc, (c + 1) * tc)
        hit = s_id == slot_ref[:, cols]
        part = _dot(jnp.where(hit, 1.0, 0.0).astype(BF16), h2_ref[cols, :])
        if c == 0:
            xe_scr[...] = part
        else:
            xe_scr[...] += part
        affs = affs + jnp.sum(jnp.where(hit, aff_ref[:, cols], 0.0), axis=-1, keepdims=True)
    xe = xe_scr[...].astype(BF16)
    fc = 512
    y = jnp.zeros((cap, d), F32)
    for c in range(f // fc):
        cols = slice(c * fc, (c + 1) * fc)
        hid = _silu(_dot(xe, wg_ref[:, cols])) * _dot(xe, wu_ref[:, cols])
        y = y + _dot(hid.astype(BF16), wd_ref[cols, :])
    ye_out[...] = (y * affs).astype(BF16)


def _expert_call(h2, slot_t, aff_t, wg, wu, wd, cap):
    b, t, d = h2.shape
    e = wg.shape[0]
    f = wg.shape[2]
    slot4 = slot_t.reshape(b, e, 1, t)
    aff4 = aff_t.reshape(b, e, 1, t)
    return pl.pallas_call(
        functools.partial(_expert_kernel, cap=cap),
        grid=(b, e),
        in_specs=[pl.BlockSpec((None, t, d), lambda bi, ei: (bi, 0, 0)),
                  pl.BlockSpec((None, None, 1, t), lambda bi, ei: (bi, ei, 0, 0)),
                  pl.BlockSpec((None, None, 1, t), lambda bi, ei: (bi, ei, 0, 0)),
                  pl.BlockSpec((None, d, f), lambda bi, ei: (ei, 0, 0)),
                  pl.BlockSpec((None, d, f), lambda bi, ei: (ei, 0, 0)),
                  pl.BlockSpec((None, f, d), lambda bi, ei: (ei, 0, 0))],
        out_specs=pl.BlockSpec((None, None, cap, d), lambda bi, ei: (bi, ei, 0, 0)),
        out_shape=jax.ShapeDtypeStruct((b, e, cap, d), BF16),
        scratch_shapes=[pltpu.VMEM((cap, d), F32)],
        compiler_params=_params("parallel", "arbitrary"),
        name="experts",
    )(h2, slot4, aff4, wg, wu, wd)


def _combine_kernel(ye_ref, slot_ref, x1_ref, g2_ref, fg_ref, o_ref, acc_scr, *, final_norm):
    e, cap, d = ye_ref.shape
    tm = x1_ref.shape[0]
    lane = lax.broadcasted_iota(jnp.int32, (tm, cap), 1)
    for ei in range(e):
        hit = lane == slot_ref[:, ei:ei + 1]
        part = _dot(jnp.where(hit, 1.0, 0.0).astype(BF16), ye_ref[ei])
        if ei == 0:
            acc_scr[...] = part
        else:
            acc_scr[...] += part
    x2 = x1_ref[...] + g2_ref[...] * acc_scr[...]
    o_ref[...] = x2 * _rms_scale(x2) * fg_ref[...] if final_norm else x2


def _combine_call(ye, slot_c, x1, gate2, final_g, final_norm):
    b, t, d = x1.shape
    e, cap = ye.shape[1], ye.shape[2]
    tm = TOKEN_TILE
    return pl.pallas_call(
        functools.partial(_combine_kernel, final_norm=final_norm),
        grid=(b, t // tm),
        in_specs=[pl.BlockSpec((None, e, cap, d), lambda bi, i: (bi, 0, 0, 0)),
                  pl.BlockSpec((None, tm, e), lambda bi, i: (bi, i, 0)),
                  pl.BlockSpec((None, tm, d), lambda bi, i: (bi, i, 0)),
                  pl.BlockSpec((None, 1, d), lambda bi, i: (bi, 0, 0)),
                  pl.BlockSpec((1, d), lambda bi, i: (0, 0))],
        out_specs=pl.BlockSpec((None, tm, d), lambda bi, i: (bi, i, 0)),
        out_shape=jax.ShapeDtypeStruct((b, t, d), F32),
        scratch_shapes=[pltpu.VMEM((tm, d), F32)],
        compiler_params=_params("parallel", "arbitrary"),
        name="combine",
    )(ye, slot_c, x1, gate2, final_g)


def _layer(x, c, ctx, c_ctx, w_mod, b_mod, norm1_g, norm2_g, w_in, conv_w, conv_b, b_if,
           pool_mix, pool_scale, mlstm_norm_g, w_pool_out, w_mlstm_out, w_out,
           w_router, w_gate, w_up, w_down):
    b, t, d = x.shape
    pw = d // 2
    ng = N_DIRS * 2 * N_HEADS
    q_off, k_off, v_off, o_off = pw, pw + d, pw + 2 * d, pw + 3 * d
    if_off, gate_off = pw + 4 * d, pw + 4 * d + ng
    cap = EC_CAPACITY * t // N_EXPERTS
    row = lambda a: a.reshape(1, -1)

    rows = -(-(b + 1) // 8) * 8
    cvec = jnp.zeros((rows, d), F32).at[:b].set(c).at[b].set(c_ctx)
    mod = _mod_call(cvec, w_mod, row(b_mod))
    shift1, scale1, gate1, shift2, scale2, gate2 = [
        mod[:b, j * d:(j + 1) * d].reshape(b, 1, d) for j in range(6)]
    shift_c, scale_c = mod[b:b + 1, 0:d], mod[b:b + 1, d:2 * d]

    w_in_b = w_in.astype(BF16)
    w_if = jnp.zeros((d, LANES), BF16).at[:, :ng].set(w_in_b[:, if_off:gate_off])
    b_if_p = jnp.zeros((1, LANES), F32).at[:, :ng].set(row(b_if))

    c0, n0, m0 = _ctx_call(ctx, shift_c, scale_c, row(norm1_g),
                           w_in_b[:, k_off:v_off], w_in_b[:, v_off:o_off], w_if,
                           conv_w[:, d:], row(conv_b[d:]), b_if_p)

    u, q, k, v, og, gg, gc, gb0, gb1, grow = _proj_call(
        x, shift1, scale1, row(norm1_g),
        w_in_b[:, 0:q_off], w_in_b[:, q_off:v_off], w_in_b[:, v_off:o_off], w_in_b[:, o_off:if_off],
        w_in_b[:, gate_off:], w_if, w_in_b[:, if_off:gate_off].T,
        conv_w, row(conv_b), b_if_p, b_if.reshape(ng, 1))

    p = _pool_call(u, pool_mix.astype(BF16), row(pool_scale))
    m = _mlstm_call(q, k, v, og, gc, gb0, gb1, grow, c0, n0, m0, row(mlstm_norm_g))

    x1, h2, aff_t = _merge_call(p, m, gg, x, gate1, shift2, scale2, row(norm2_g),
                                w_pool_out.astype(BF16), w_mlstm_out.astype(BF16), w_out.astype(BF16),
                                w_router.T)
    slot_t = _route_call(aff_t, cap)
    ye = _expert_call(h2, slot_t, aff_t, w_gate.astype(BF16), w_up.astype(BF16), w_down.astype(BF16), cap)
    return ye, jnp.swapaxes(slot_t, 1, 2), x1, gate2


def kernel(x, c, ctx, c_ctx, w_mod, b_mod, norm1_g, norm2_g, w_in, conv_w, conv_b, b_if, pool_mix, pool_scale,
           mlstm_norm_g, w_pool_out, w_mlstm_out, w_out, w_router, w_gate, w_up, w_down, final_g):
    depth = w_mod.shape[0]
    for l in range(depth):
        ye, slot_c, x1, gate2 = _layer(
            x, c, ctx, c_ctx, w_mod[l], b_mod[l], norm1_g[l], norm2_g[l], w_in[l], conv_w[l], conv_b[l],
            b_if[l], pool_mix[l], pool_scale[l], mlstm_norm_g[l], w_pool_out[l], w_mlstm_out[l], w_out[l],
            w_router[l], w_gate[l], w_up[l], w_down[l])
        x = _combine_call(ye, slot_c, x1, gate2, final_g.reshape(1, -1), final_norm=l == depth - 1)
    return x
```

```python
import functools

import jax
import jax.numpy as jnp
from jax import lax
from jax.experimental import pallas as pl
from jax.experimental.pallas import tpu as pltpu

F32 = jnp.float32
BF16 = jnp.bfloat16

GRID_W = 64
POOL_WINDOWS = (2, 4, 8, 16)
N_HEADS = 4
CONV_W = 5
N_DIRS = 2
N_EXPERTS = 16
EC_CAPACITY = 2
NORM_EPS = 1e-6

CHUNK = 256
TOKEN_TILE = 512
HALO = 16
LANES = 128
V7X_VMEM_LIMIT_BYTES = 56 * 1024 * 1024

NN = (((1,), (0,)), ((), ()))
NT = (((1,), (1,)), ((), ()))
TN = (((0,), (0,)), ((), ()))


def _dot(a, b, dims=NN):
    return lax.dot_general(a, b, dims, preferred_element_type=F32)


def _split2(a):
    hi = a.astype(BF16)
    lo = (a - hi.astype(F32)).astype(BF16)
    return hi, lo


def _split3(a):
    a1 = a.astype(BF16)
    r1 = a - a1.astype(F32)
    a2 = r1.astype(BF16)
    a3 = (r1 - a2.astype(F32)).astype(BF16)
    return a1, a2, a3


def _dot3(a, b, dims=NN):
    ah, al = _split2(a)
    bh, bl = _split2(b)
    return _dot(ah, bh, dims) + _dot(ah, bl, dims) + _dot(al, bh, dims)


def _dot_left01(t01, a):
    a1, a2, a3 = _split3(a)
    return _dot(t01, a1) + _dot(t01, a2) + _dot(t01, a3)


def _dot_right01(a, t01):
    a1, a2, a3 = _split3(a)
    return _dot(a1, t01) + _dot(a2, t01) + _dot(a3, t01)


def _silu(x):
    return x * jax.nn.sigmoid(x)


def _log_sigmoid(x):
    return jnp.minimum(x, 0.0) - jnp.log1p(jnp.exp(-jnp.abs(x)))


def _rms_scale(x):
    return lax.rsqrt(jnp.mean(x * x, axis=-1, keepdims=True) + NORM_EPS)


def _tri01(n, kind):
    i = lax.broadcasted_iota(jnp.int32, (n, n), 0)
    j = lax.broadcasted_iota(jnp.int32, (n, n), 1)
    cond = {"le": j <= i, "ge": j >= i, "lt": j < i, "gt": j > i}[kind]
    return jnp.where(cond, 1.0, 0.0).astype(BF16)


def _shift(n):
    assert n & (n - 1) == 0, n
    return n.bit_length() - 1


def _div_pow2(x, n):
    return lax.shift_right_logical(x, _shift(n))


def _mod_pow2(x, n):
    return jnp.bitwise_and(x, n - 1)


def _params(*sem):
    return pltpu.CompilerParams(dimension_semantics=sem, vmem_limit_bytes=V7X_VMEM_LIMIT_BYTES)


def _resident(shape):
    nd = len(shape)
    return pl.BlockSpec(shape, lambda *_: (0,) * nd)


def _mod_kernel(c_ref, w_ref, b_ref, o_ref):
    o_ref[...] = _dot3(_silu(c_ref[...]), w_ref[...]) + b_ref[...]


def _mod_call(cvec, w_mod, b_mod):
    rows, d = cvec.shape
    n = w_mod.shape[1]
    tn = 1536
    return pl.pallas_call(
        _mod_kernel,
        grid=(n // tn,),
        in_specs=[pl.BlockSpec((rows, d), lambda j: (0, 0)),
                  pl.BlockSpec((d, tn), lambda j: (0, j)),
                  pl.BlockSpec((1, tn), lambda j: (0, j))],
        out_specs=pl.BlockSpec((rows, tn), lambda j: (0, j)),
        out_shape=jax.ShapeDtypeStruct((rows, n), F32),
        compiler_params=_params("parallel"),
        name="mod",
    )(cvec, w_mod, b_mod)


def _ctx_kernel(ctx_ref, sh_ref, sc_ref, g_ref, wk_ref, wv_ref, wi_ref, wf_ref, cw_ref, cb_ref, bi_ref, bf_ref,
                c_out, m_out):
    lc, d = ctx_ref.shape
    dh = d // N_HEADS
    x = ctx_ref[...]
    hc = (x * _rms_scale(x) * g_ref[...]) * (1.0 + sc_ref[...]) + sh_ref[...]
    hcb = hc.astype(BF16)

    kpre = _dot(hcb, wk_ref[...])
    pad = jnp.zeros((8, d), F32)
    kp = jnp.concatenate([pad, kpre, pad], axis=0)
    cw = cw_ref[...]
    acc = cb_ref[...] + cw[0:1, :] * kp[6:6 + lc, :]
    for j in range(1, CONV_W):
        acc = acc + cw[j:j + 1, :] * kp[6 + j:6 + j + lc, :]
    k = _silu(acc) * (dh ** -0.5)
    v = _dot(hcb, wv_ref[...]).astype(BF16)

    gi = _dot(hcb, wi_ref[...]) + bi_ref[...]
    lf = _log_sigmoid(_dot(hcb, wf_ref[...]) + bf_ref[...])
    lane = lax.broadcasted_iota(jnp.int32, lf.shape, 1)
    w_all = gi + jnp.where(lane < N_HEADS, _dot_left01(_tri01(lc, "gt"), lf), _dot_left01(_tri01(lc, "lt"), lf))
    ones = jnp.ones((lc, LANES), BF16)
    for dr in range(N_DIRS):
        for h in range(N_HEADS):
            col = dr * N_HEADS + h
            w = w_all[:, col:col + 1]
            m = jnp.max(w, axis=0, keepdims=True)
            wk = jnp.exp(w - m) * k[:, h * dh:(h + 1) * dh]
            v_aug = jnp.concatenate([v[:, h * dh:(h + 1) * dh], ones], axis=1)
            c_out[dr, h] = _dot(wk.astype(BF16), v_aug, TN)
            m_out[dr, h] = jnp.broadcast_to(m, (1, LANES))


def _ctx_call(ctx, sh_c, sc_c, g1, wk, wv, wi, wf, cw_k, cb_k, bi, bf):
    b, lc, d = ctx.shape
    dh = d // N_HEADS
    row = lambda w: pl.BlockSpec((1, w), lambda i: (0, 0))
    return pl.pallas_call(
        _ctx_kernel,
        grid=(b,),
        in_specs=[pl.BlockSpec((None, lc, d), lambda i: (i, 0, 0)),
                  row(d), row(d), row(d),
                  pl.BlockSpec((d, d), lambda i: (0, 0)),
                  pl.BlockSpec((d, d), lambda i: (0, 0)),
                  pl.BlockSpec((d, LANES), lambda i: (0, 0)),
                  pl.BlockSpec((d, LANES), lambda i: (0, 0)),
                  pl.BlockSpec((CONV_W, d), lambda i: (0, 0)),
                  row(d), row(LANES), row(LANES)],
        out_specs=[pl.BlockSpec((None, N_DIRS, N_HEADS, dh, dh + LANES), lambda i: (i, 0, 0, 0, 0)),
                   pl.BlockSpec((None, N_DIRS, N_HEADS, 1, LANES), lambda i: (i, 0, 0, 0, 0))],
        out_shape=[jax.ShapeDtypeStruct((b, N_DIRS, N_HEADS, dh, dh + LANES), F32),
                   jax.ShapeDtypeStruct((b, N_DIRS, N_HEADS, 1, LANES), F32)],
        compiler_params=_params("parallel"),
        name="ctx_states",
    )(ctx, sh_c, sc_c, g1, wk, wv, wi, wf, cw_k, cb_k, bi, bf)


def _cummax_rows(x, reverse):
    n, w = x.shape
    k = 1
    while k < n:
        fill = jnp.full((k, w), -jnp.inf, F32)
        shifted = (jnp.concatenate([x[k:, :], fill], axis=0) if reverse
                   else jnp.concatenate([fill, x[:n - k, :]], axis=0))
        x = jnp.maximum(x, shifted)
        k *= 2
    return x


def _proj_kernel(xp_ref, x_ref, xn_ref, sh_ref, sc_ref, g_ref,
                 wpool_ref, wqk_ref, wv_ref, wo_ref, wg_ref, wi_ref, wf_ref, wit_ref, wft_ref,
                 cw_ref, cb_ref, bi_ref, bf_ref, bit_ref, bft_ref,
                 u_out, q_out, k_out, v_out, og_out, gg_out, bq_out, cq_out, mq_out, crow_out,
                 hx_scr, r_scr):
    tm, d = x_ref.shape
    dh = d // N_HEADS
    i = pl.program_id(1)
    last = pl.num_programs(1) - 1

    x_ext = jnp.concatenate([xp_ref[...], x_ref[...], xn_ref[...]], axis=0)
    hx = (x_ext * _rms_scale(x_ext) * g_ref[...]) * (1.0 + sc_ref[...]) + sh_ref[...]
    n_ext = tm + 2 * HALO
    r_id = lax.broadcasted_iota(jnp.int32, (n_ext, 1), 0)
    valid = jnp.logical_and(jnp.logical_or(i > 0, r_id >= HALO),
                            jnp.logical_or(i < last, r_id < HALO + tm))
    hx_scr[...] = jnp.where(valid, hx, 0.0).astype(BF16)
    hxc = hx_scr[HALO:HALO + tm, :]

    nc = 512
    half = CONV_W // 2

    def qk_dot(c):
        r_scr[c % 2] = _dot(hx_scr[...], wqk_ref[:, c * nc:(c + 1) * nc])

    def qk_conv(c):
        cols = slice(c * nc, (c + 1) * nc)
        r = r_scr.at[c % 2]
        cw = cw_ref[:, cols]
        acc = cb_ref[:, cols] + cw[0:1, :] * r[HALO - half:HALO - half + tm, :]
        for j in range(1, CONV_W):
            acc = acc + cw[j:j + 1, :] * r[HALO - half + j:HALO - half + j + tm, :]
        y = _silu(acc)
        if c * nc < d:
            q_out[:, cols] = y.astype(BF16)
        else:
            k_out[:, c * nc - d:(c + 1) * nc - d] = (y * (dh ** -0.5)).astype(BF16)

    def v_chunk(c):
        cols = slice(c * nc, (c + 1) * nc)
        v_out[:, cols] = _dot(hxc, wv_ref[:, cols]).astype(BF16)

    def o_chunk(c):
        cols = slice(c * nc, (c + 1) * nc)
        og_out[:, cols] = jax.nn.sigmoid(_dot(hxc, wo_ref[:, cols])).astype(BF16)

    def g_chunk(c):
        cols = slice(c * nc, (c + 1) * nc)
        gg_out[:, cols] = jax.nn.sigmoid(_dot(hxc, wg_ref[:, cols])).astype(BF16)

    qk_dot(0); qk_dot(1)
    qk_conv(0); v_chunk(0); v_chunk(1); qk_dot(2)
    qk_conv(1); o_chunk(0); o_chunk(1); qk_dot(3)
    qk_conv(2); g_chunk(0); g_chunk(1)
    qk_conv(3); g_chunk(2); g_chunk(3)
    u_out[...] = _dot(hxc, wpool_ref[...]).astype(BF16)

    nq = N_DIRS * N_HEADS
    gi = _dot(hxc, wi_ref[...]) + bi_ref[...]
    lf = _log_sigmoid(_dot(hxc, wf_ref[...]) + bf_ref[...])
    gi_r = _dot(wit_ref[...], hxc, NT) + bit_ref[...]
    lf_r = _log_sigmoid(_dot(wft_ref[...], hxc, NT) + bft_ref[...])
    fwd_lane = lax.broadcasted_iota(jnp.int32, (CHUNK, LANES), 1) < N_HEADS
    fwd_sub = lax.broadcasted_iota(jnp.int32, (nq, CHUNK), 0) < N_HEADS
    t_le, t_ge = _tri01(CHUNK, "le"), _tri01(CHUNK, "ge")
    for j in range(tm // CHUNK):
        rows = slice(j * CHUNK, (j + 1) * CHUNK)
        b = jnp.where(fwd_lane, _dot_left01(t_le, lf[rows, :]), _dot_left01(t_ge, lf[rows, :]))
        c = gi[rows, :] - b
        cm = jnp.where(fwd_lane, _cummax_rows(c, False), _cummax_rows(c, True))
        bq_out[rows, :] = b[:, :nq]
        cq_out[rows, :] = c[:, :nq]
        mq_out[rows, :] = cm[:, :nq]
        b_r = jnp.where(fwd_sub, _dot_right01(lf_r[:, rows], t_ge), _dot_right01(lf_r[:, rows], t_le))
        crow_out[:, rows] = gi_r[:, rows] - b_r


def _proj_call(x, sh, sc, g1, wpool, wqk, wv, wo, wg, wi, wf, wit, wft, cw, cb, bi, bf, bit, bft):
    b, t, d = x.shape
    tm = TOKEN_TILE
    nt = t // tm
    hb = tm // HALO
    nq = N_DIRS * N_HEADS
    per_b = pl.BlockSpec((None, 1, d), lambda bi_, i: (bi_, 0, 0))
    tile = lambda w: pl.BlockSpec((None, tm, w), lambda bi_, i: (bi_, i, 0))
    const = lambda a: pl.BlockSpec(a.shape, lambda bi_, i: (0,) * a.ndim)
    out_shapes = [jax.ShapeDtypeStruct((b, t, d // 2), BF16),
                  jax.ShapeDtypeStruct((b, t, d), BF16),
                  jax.ShapeDtypeStruct((b, t, d), BF16),
                  jax.ShapeDtypeStruct((b, t, d), BF16),
                  jax.ShapeDtypeStruct((b, t, d), BF16),
                  jax.ShapeDtypeStruct((b, t, 2 * d), BF16),
                  jax.ShapeDtypeStruct((b, t, nq), F32),
                  jax.ShapeDtypeStruct((b, t, nq), F32),
                  jax.ShapeDtypeStruct((b, t, nq), F32),
                  jax.ShapeDtypeStruct((b, nq, t), F32)]
    out_specs = [tile(d // 2), tile(d), tile(d), tile(d), tile(d), tile(2 * d),
                 tile(nq), tile(nq), tile(nq),
                 pl.BlockSpec((None, nq, tm), lambda bi_, i: (bi_, 0, i))]
    return pl.pallas_call(
        _proj_kernel,
        grid=(b, nt),
        in_specs=[pl.BlockSpec((None, HALO, d), lambda bi_, i: (bi_, jnp.maximum(i * hb - 1, 0), 0)),
                  tile(d),
                  pl.BlockSpec((None, HALO, d), lambda bi_, i: (bi_, jnp.minimum((i + 1) * hb, t // HALO - 1), 0)),
                  per_b, per_b, const(g1),
                  const(wpool), const(wqk), const(wv), const(wo), const(wg),
                  const(wi), const(wf), const(wit), const(wft),
                  const(cw), const(cb), const(bi), const(bf), const(bit), const(bft)],
        out_specs=out_specs,
        out_shape=out_shapes,
        scratch_shapes=[pltpu.VMEM((tm + 2 * HALO, d), BF16),
                        pltpu.VMEM((2, tm + 2 * HALO, 512), F32)],
        compiler_params=_params("parallel", "parallel"),
        name="proj",
    )(x, x, x, sh, sc, g1, wpool, wqk, wv, wo, wg, wi, wf, wit, wft, cw, cb, bi, bf, bit, bft)


def _pool_kernel(u_ref, mix_ref, scale_ref, p_out, pad_scr):
    t, pw = u_ref.shape
    gw = pw // len(POOL_WINDOWS)
    rows = t // GRID_W
    tile = 256
    maxlo = max(POOL_WINDOWS) // 2
    padr = maxlo * GRID_W
    pad_scr[0:padr, :] = jnp.zeros((padr, gw), F32)
    pad_scr[padr + t:padr + t + padr, :] = jnp.zeros((padr, gw), F32)

    ti = lax.broadcasted_iota(jnp.int32, (tile, tile), 0)
    tj = lax.broadcasted_iota(jnp.int32, (tile, tile), 1)
    same_row = _div_pow2(ti, GRID_W) == _div_pow2(tj, GRID_W)
    ci, cj = _mod_pow2(ti, GRID_W), _mod_pow2(tj, GRID_W)
    tok = lax.broadcasted_iota(jnp.int32, (t, gw), 0)
    r_id, c_id = _div_pow2(tok, GRID_W), _mod_pow2(tok, GRID_W)

    for g, side in enumerate(POOL_WINDOWS):
        lo, hi = side // 2, side - side // 2
        cols = slice(g * gw, (g + 1) * gw)
        band = jnp.logical_and(same_row, jnp.logical_and(cj >= ci - lo, cj < ci + hi))
        pw01 = jnp.where(band, 1.0, 0.0).astype(BF16)
        for k in range(t // tile):
            rs = slice(k * tile, (k + 1) * tile)
            pad_scr[padr + k * tile:padr + (k + 1) * tile, :] = _dot(pw01, u_ref[rs, cols])
        tot = pad_scr[padr - lo * GRID_W:padr - lo * GRID_W + t, :]
        for dlt in range(-lo + 1, hi):
            tot = tot + pad_scr[padr + dlt * GRID_W:padr + dlt * GRID_W + t, :]
        cnt = ((jnp.minimum(r_id + hi, rows) - jnp.maximum(r_id - lo, 0))
               * (jnp.minimum(c_id + hi, GRID_W) - jnp.maximum(c_id - lo, 0))).astype(F32)
        a = tot / cnt - u_ref[:, cols].astype(F32)
        p = _dot(a.astype(BF16), mix_ref[g]) * scale_ref[:, cols]
        p_out[:, cols] = p.astype(BF16)


def _pool_call(u, mix, scale):
    b, t, pw = u.shape
    gw = pw // len(POOL_WINDOWS)
    padr = (max(POOL_WINDOWS) // 2) * GRID_W
    return pl.pallas_call(
        _pool_kernel,
        grid=(b,),
        in_specs=[pl.BlockSpec((None, t, pw), lambda i: (i, 0, 0)),
                  pl.BlockSpec(mix.shape, lambda i: (0, 0, 0)),
                  pl.BlockSpec((1, pw), lambda i: (0, 0))],
        out_specs=pl.BlockSpec((None, t, pw), lambda i: (i, 0, 0)),
        out_shape=jax.ShapeDtypeStruct((b, t, pw), BF16),
        scratch_shapes=[pltpu.VMEM((t + 2 * padr, gw), F32)],
        compiler_params=_params("parallel"),
        name="pool",
    )(u, mix, scale)


def _mlstm_dir(q_ref, k_ref, v_ref, bq_ref, cq_ref, mq_ref, crow_ref, c_scr, m_scr, reverse):
    L, d = q_ref.shape
    dh = d // N_HEADS
    nhalf = dh // LANES
    ri = lax.broadcasted_iota(jnp.int32, (L, LANES), 0)
    cj = lax.broadcasted_iota(jnp.int32, (L, LANES), 1)
    masks = [(cj + i * LANES >= ri) if reverse else (cj + i * LANES <= ri) for i in range(L // LANES)]
    ones = jnp.ones((L, LANES), BF16)
    end = 0 if reverse else L - 1
    off = N_HEADS if reverse else 0

    hs_all = []
    for h in range(N_HEADS):
        hs = slice(h * dh, (h + 1) * dh)
        st = off + h
        q = q_ref[:, hs]
        k = k_ref[:, hs]
        v_aug = jnp.concatenate([v_ref[:, hs], ones], axis=1)
        b_c = bq_ref[:, st:st + 1]
        c_c = cq_ref[:, st:st + 1]
        cm_c = mq_ref[:, st:st + 1]
        c_r = crow_ref[st:st + 1, :]
        m_prev = m_scr[st][:, 0:1]
        c_prev = c_scr[st]

        mm = jnp.broadcast_to(jnp.maximum(cm_c, m_prev), (L, LANES))
        b_rep = jnp.broadcast_to(b_c, (L, LANES))
        w_inter = jnp.exp(m_prev - mm)
        qk = _dot(q, k, NT)
        smat = jnp.concatenate(
            [qk[:, i * LANES:(i + 1) * LANES]
             * jnp.exp(jnp.where(masks[i], c_r[:, i * LANES:(i + 1) * LANES] - mm, -jnp.inf))
             for i in range(L // LANES)], axis=1).astype(BF16)
        intra = _dot(smat, v_aug)
        inter = _dot(q, c_prev.astype(BF16))
        den = w_inter * inter[:, dh:] + intra[:, dh:]
        inv = 1.0 / jnp.maximum(jnp.abs(den), jnp.exp(-(b_rep + mm)))
        hs_all.append([(w_inter * inter[:, i * LANES:(i + 1) * LANES] + intra[:, i * LANES:(i + 1) * LANES]) * inv
                       for i in range(nhalf)])

        g_tot = b_c[end:end + 1, :]
        m_new = g_tot + jnp.maximum(m_prev, cm_c[end:end + 1, :])
        decay = jnp.exp(g_tot + m_prev - m_new)
        e_rep = jnp.broadcast_to(jnp.exp(g_tot + c_c - m_new), (L, LANES))
        kf = k.astype(F32)
        wk = jnp.concatenate([kf[:, i * LANES:(i + 1) * LANES] * e_rep for i in range(nhalf)], axis=1)
        c_scr[st] = decay * c_prev + _dot(wk.astype(BF16), v_aug, TN)
        m_scr[st] = jnp.broadcast_to(m_new, (1, LANES))
    return hs_all


def _mlstm_kernel(qf_ref, kf_ref, vf_ref, ogf_ref, bqf_ref, cqf_ref, mqf_ref, crf_ref,
                  qb_ref, kb_ref, vb_ref, ogb_ref, bqb_ref, cqb_ref, mqb_ref, crb_ref,
                  c0_ref, m0_ref, ng_ref, o_ref,
                  c_scr, m_scr, hf_scr, hb_scr):
    L, d = qf_ref.shape
    dh = d // N_HEADS
    nhalf = dh // LANES
    s = pl.program_id(1)
    nch = pl.num_programs(1)
    half = nch // 2

    @pl.when(s == 0)
    def _():
        for j in range(N_DIRS * N_HEADS):
            c_scr[j] = c0_ref[j // N_HEADS, j % N_HEADS]
            m_scr[j] = m0_ref[j // N_HEADS, j % N_HEADS]

    h_f = _mlstm_dir(qf_ref, kf_ref, vf_ref, bqf_ref, cqf_ref, mqf_ref, crf_ref, c_scr, m_scr, False)
    h_b = _mlstm_dir(qb_ref, kb_ref, vb_ref, bqb_ref, cqb_ref, mqb_ref, crb_ref, c_scr, m_scr, True)
    blocks = [(h, i, slice(h * dh + i * LANES, h * dh + (i + 1) * LANES))
              for h in range(N_HEADS) for i in range(nhalf)]

    @pl.when(s < half)
    def _():
        rf = pl.multiple_of(s * L, L)
        rb = pl.multiple_of((half - 1 - s) * L, L)
        for h, i, cs in blocks:
            hf_scr[pl.ds(rf, L), cs] = h_f[h][i]
            hb_scr[pl.ds(rb, L), cs] = h_b[h][i]

    @pl.when(s >= half)
    def _():
        ru = pl.multiple_of((s - half) * L, L)
        rl = pl.multiple_of((nch - 1 - s) * L, L)
        ones_sq = jnp.ones((dh, LANES), BF16)

        def finish(ht, h, og_ref, out):
            sq_hi, sq_lo = _split2(jnp.concatenate([x * x for x in ht], axis=1))
            ssq = _dot(sq_hi, ones_sq) + _dot(sq_lo, ones_sq)
            scale = lax.rsqrt(ssq * (1.0 / dh) + NORM_EPS)
            for i in range(nhalf):
                cs = slice(h * dh + i * LANES, h * dh + (i + 1) * LANES)
                out[:, cs] = (ht[i] * scale * ng_ref[:, cs] * og_ref[:, cs].astype(F32)).astype(BF16)

        for h in range(N_HEADS):
            cols = [slice(h * dh + i * LANES, h * dh + (i + 1) * LANES) for i in range(nhalf)]
            finish([h_f[h][i] + hb_scr[pl.ds(ru, L), cols[i]] for i in range(nhalf)], h, ogf_ref, o_ref.at[1])
            finish([h_b[h][i] + hf_scr[pl.ds(rl, L), cols[i]] for i in range(nhalf)], h, ogb_ref, o_ref.at[0])


def _mlstm_call(q, k, v, og, bq, cq, mq, crow, c0, m0, norm_g):
    b, t, d = q.shape
    dh = d // N_HEADS
    L = CHUNK
    nch = t // L
    half = nch // 2
    assert nch % 2 == 0
    nq = N_DIRS * N_HEADS

    fwd = lambda w: pl.BlockSpec((None, L, w), lambda bi, s: (bi, s, 0))
    bwd = lambda w: pl.BlockSpec((None, L, w), lambda bi, s: (bi, nch - 1 - s, 0))
    state = lambda w0, w1: pl.BlockSpec((None, N_DIRS, N_HEADS, w0, w1), lambda bi, s: (bi, 0, 0, 0, 0))
    mo = pl.pallas_call(
        _mlstm_kernel,
        grid=(b, nch),
        in_specs=[fwd(d), fwd(d), fwd(d), fwd(d), fwd(nq), fwd(nq), fwd(nq),
                  pl.BlockSpec((None, nq, L), lambda bi, s: (bi, 0, s)),
                  bwd(d), bwd(d), bwd(d), bwd(d), bwd(nq), bwd(nq), bwd(nq),
                  pl.BlockSpec((None, nq, L), lambda bi, s: (bi, 0, nch - 1 - s)),
                  state(dh, dh + LANES), state(1, LANES),
                  pl.BlockSpec((1, d), lambda bi, s: (0, 0))],
        out_specs=pl.BlockSpec((None, 2, None, L, d), lambda bi, s: (bi, 0, jnp.maximum(s - half, 0), 0, 0)),
        out_shape=jax.ShapeDtypeStruct((b, 2, half, L, d), BF16),
        scratch_shapes=[pltpu.VMEM((N_DIRS * N_HEADS, dh, dh + LANES), F32),
                        pltpu.VMEM((N_DIRS * N_HEADS, 1, LANES), F32),
                        pltpu.VMEM((t // 2, d), F32),
                        pltpu.VMEM((t // 2, d), F32)],
        compiler_params=_params("parallel", "arbitrary"),
        name="mlstm",
    )(q, k, v, og, bq, cq, mq, crow, q, k, v, og, bq, cq, mq, crow, c0, m0, norm_g)
    return jnp.concatenate([mo[:, 0, ::-1].reshape(b, t // 2, d), mo[:, 1].reshape(b, t // 2, d)], axis=1)


def _merge_kernel(p_ref, m_ref, gg_ref, x_ref, g1_ref, sh2_ref, sc2_ref, n2_ref,
                  wpo_ref, wmo_ref, wout_ref, wr_ref, x1_out, h2_out, aff_out):
    tm, d = x_ref.shape
    sub = tm
    for r in range(tm // sub):
        rows = slice(r * sub, (r + 1) * sub)
        a = _dot(p_ref[rows, :], wpo_ref[...])
        mm = _dot(m_ref[rows, :], wmo_ref[...])
        mixed = gg_ref[rows, 0:d].astype(F32) * a + gg_ref[rows, d:2 * d].astype(F32) * mm
        x1 = x_ref[rows, :] + g1_ref[...] * _dot(mixed.astype(BF16), wout_ref[...])
        x1_out[rows, :] = x1
        h2 = (x1 * _rms_scale(x1) * n2_ref[...]) * (1.0 + sc2_ref[...]) + sh2_ref[...]
        h2_out[rows, :] = h2.astype(BF16)
        logits = _dot3(wr_ref[...], h2, NT)
        z = jnp.exp(logits - jnp.max(logits, axis=0, keepdims=True))
        aff_out[:, rows] = z / jnp.sum(z, axis=0, keepdims=True)


def _merge_call(p, m, gg, x, gate1, sh2, sc2, n2, wpo, wmo, wout, wr_t):
    b, t, d = x.shape
    tm = TOKEN_TILE
    e = wr_t.shape[0]
    per_b = pl.BlockSpec((None, 1, d), lambda bi, i: (bi, 0, 0))
    tile = lambda w: pl.BlockSpec((None, tm, w), lambda bi, i: (bi, i, 0))
    const = lambda a: pl.BlockSpec(a.shape, lambda bi, i: (0,) * a.ndim)
    return pl.pallas_call(
        _merge_kernel,
        grid=(b, t // tm),
        in_specs=[tile(d // 2), tile(d), tile(2 * d), tile(d), per_b, per_b, per_b, const(n2),
                  const(wpo), const(wmo), const(wout), const(wr_t)],
        out_specs=[tile(d), tile(d), pl.BlockSpec((None, e, tm), lambda bi, i: (bi, 0, i))],
        out_shape=[jax.ShapeDtypeStruct((b, t, d), F32),
                   jax.ShapeDtypeStruct((b, t, d), BF16),
                   jax.ShapeDtypeStruct((b, e, t), F32)],
        compiler_params=_params("parallel", "parallel"),
        name="merge",
    )(p, m, gg, x, gate1, sh2, sc2, n2, wpo, wmo, wout, wr_t)


def _route_kernel(aff_ref, slot_out, lo_out, *, cap):
    e, t = aff_ref.shape
    aff = aff_ref[...]

    def step(i, thr):
        cand = thr | (jnp.int32(1) << (30 - i))
        cnt = jnp.sum(jnp.where(aff >= pltpu.bitcast(cand, F32), 1.0, 0.0), axis=-1, keepdims=True)
        return jnp.where(cnt >= cap, cand, thr)

    thr = pltpu.bitcast(lax.fori_loop(0, 31, step, jnp.zeros((e, 1), jnp.int32)), F32)
    gt = aff > thr
    eq = aff == thr
    need = cap - jnp.sum(jnp.where(gt, 1.0, 0.0), axis=-1, keepdims=True).astype(jnp.int32)

    seg = 256
    t_ge = _tri01(seg, "ge")

    def prefix_incl(x01):
        outs, carries, carry = [], [], jnp.zeros((e, 1), F32)
        for j in range(t // seg):
            p = _dot(x01[:, j * seg:(j + 1) * seg].astype(BF16), t_ge) + carry
            outs.append(p)
            carry = p[:, seg - 1:seg]
            carries.append(carry)
        return jnp.concatenate(outs, axis=1), carries

    eq_f = jnp.where(eq, 1.0, 0.0)
    tie_rank = (prefix_incl(eq_f)[0] - eq_f).astype(jnp.int32)
    sel = jnp.logical_or(gt, jnp.logical_and(eq, tie_rank < need))
    rank, carries = prefix_incl(jnp.where(sel, 1.0, 0.0))
    slot_out[...] = jnp.where(sel, rank.astype(jnp.int32) - 1, -1)

    lane = lax.broadcasted_iota(jnp.int32, (e, LANES), 1)
    lo = jnp.zeros((e, LANES), F32)
    per_tile = TOKEN_TILE // seg
    for c in range(1, t // TOKEN_TILE + 1):
        lo = jnp.where(lane == c, carries[c * per_tile - 1], lo)
    lo_out[...] = lo.astype(jnp.int32)


def _route_call(aff_t, cap):
    b, e, t = aff_t.shape
    return pl.pallas_call(
        functools.partial(_route_kernel, cap=cap),
        grid=(b,),
        in_specs=[pl.BlockSpec((None, e, t), lambda i: (i, 0, 0))],
        out_specs=[pl.BlockSpec((None, e, t), lambda i: (i, 0, 0)),
                   pl.BlockSpec((None, e, LANES), lambda i: (i, 0, 0))],
        out_shape=[jax.ShapeDtypeStruct((b, e, t), jnp.int32),
                   jax.ShapeDtypeStruct((b, e, LANES), jnp.int32)],
        compiler_params=_params("parallel"),
        name="route",
    )(aff_t)


SLOT_WINDOW = 128
SLOT_ALIGN = 16
EXPERT_GROUP = 4


def _aligned(lo):
    return jnp.bitwise_and(lo, -SLOT_ALIGN)


def _window_start(nominal, cap):
    return pl.multiple_of(jnp.minimum(nominal, cap - SLOT_WINDOW), SLOT_ALIGN)


def _n_windows(lo, hi):
    return lax.shift_right_logical(hi - _aligned(lo) + (SLOT_WINDOW - 1), _shift(SLOT_WINDOW))


def _gather_kernel(lo_ref, h2_ref, slot_ref, xe_out, *, cap, n_tiles):
    n_exp = slot_ref.shape[0]
    tc, w = TOKEN_TILE, SLOT_WINDOW
    b = pl.program_id(0)
    stride = n_tiles + 1
    xe_out[...] = jnp.zeros(xe_out.shape, BF16)
    s_id = lax.broadcasted_iota(jnp.int32, (w, tc), 0)

    def add_rows(e, start, z):
        xe_out[e, pl.ds(start, w), :] = xe_out[e, pl.ds(start, w), :] + z.astype(BF16)

    def tile_body(c, carry):
        t0 = pl.multiple_of(c * tc, tc)
        for g0 in range(0, n_exp, EXPERT_GROUP):
            starts, blocks = [], []
            for e in range(g0, g0 + EXPERT_GROUP):
                a0 = _window_start(_aligned(lo_ref[b, e * stride + c]), cap)
                hit = (s_id + a0) == slot_ref[e:e + 1, pl.ds(t0, tc)]
                blocks.append(jnp.where(hit, 1.0, 0.0).astype(BF16))
                starts.append(a0)
            z = _dot(jnp.concatenate(blocks, axis=0), h2_ref[pl.ds(t0, tc), :])
            for j in range(EXPERT_GROUP):
                add_rows(g0 + j, starts[j], z[j * w:(j + 1) * w, :])
        for e in range(n_exp):
            lo, hi = lo_ref[b, e * stride + c], lo_ref[b, e * stride + c + 1]

            def window_body(k, carry2, e=e, lo=lo):
                nominal = _aligned(lo) + k * w
                a = _window_start(nominal, cap)
                srow = slot_ref[e:e + 1, pl.ds(t0, tc)]
                hit = jnp.logical_and((s_id + a) == srow, srow >= nominal)
                add_rows(e, a, _dot(jnp.where(hit, 1.0, 0.0).astype(BF16), h2_ref[pl.ds(t0, tc), :]))
                return carry2

            lax.fori_loop(1, _n_windows(lo, hi), window_body, 0)
        return carry

    lax.fori_loop(0, n_tiles, tile_body, 0)


def _gather_call(lo2, h2, slot_t, cap):
    b, t, d = h2.shape
    e = slot_t.shape[1]
    n_tiles = t // TOKEN_TILE
    grid_spec = pltpu.PrefetchScalarGridSpec(
        num_scalar_prefetch=1,
        grid=(b,),
        in_specs=[pl.BlockSpec((None, t, d), lambda i, lo: (i, 0, 0)),
                  pl.BlockSpec((None, e, t), lambda i, lo: (i, 0, 0))],
        out_specs=pl.BlockSpec((None, e, cap, d), lambda i, lo: (i, 0, 0, 0)),
    )
    return pl.pallas_call(
        functools.partial(_gather_kernel, cap=cap, n_tiles=n_tiles),
        grid_spec=grid_spec,
        out_shape=jax.ShapeDtypeStruct((b, e, cap, d), BF16),
        compiler_params=_params("arbitrary"),
        name="gather",
    )(lo2, h2, slot_t)


def _expert_kernel(xe_ref, wg_ref, wu_ref, wd_ref, ye_out, wg_scr, wu_scr, wd_scr):
    f = wg_ref.shape[1]

    @pl.when(pl.program_id(1) == 0)
    def _():
        wg_scr[...] = wg_ref[...].astype(BF16)
        wu_scr[...] = wu_ref[...].astype(BF16)
        wd_scr[...] = wd_ref[...].astype(BF16)

    xe = xe_ref[...]
    fc = 512
    y = None
    for c in range(f // fc):
        cols = slice(c * fc, (c + 1) * fc)
        hid = _silu(_dot(xe, wg_scr[:, cols])) * _dot(xe, wu_scr[:, cols])
        part = _dot(hid.astype(BF16), wd_scr[cols, :])
        y = part if y is None else y + part
    ye_out[...] = y.astype(BF16)


def _expert_call(xe, wg, wu, wd):
    b, e, cap, d = xe.shape
    f = wg.shape[2]
    return pl.pallas_call(
        _expert_kernel,
        grid=(e, b),
        in_specs=[pl.BlockSpec((None, None, cap, d), lambda ei, bi: (bi, ei, 0, 0)),
                  pl.BlockSpec((None, d, f), lambda ei, bi: (ei, 0, 0)),
                  pl.BlockSpec((None, d, f), lambda ei, bi: (ei, 0, 0)),
                  pl.BlockSpec((None, f, d), lambda ei, bi: (ei, 0, 0))],
        out_specs=pl.BlockSpec((None, None, cap, d), lambda ei, bi: (bi, ei, 0, 0)),
        out_shape=jax.ShapeDtypeStruct((b, e, cap, d), BF16),
        scratch_shapes=[pltpu.VMEM((d, f), BF16), pltpu.VMEM((d, f), BF16), pltpu.VMEM((f, d), BF16)],
        compiler_params=_params("parallel", "arbitrary"),
        name="experts",
    )(xe, wg, wu, wd)


def _combine_kernel(lo_ref, ye_ref, slot_ref, aff_ref, x1_ref, g2_ref, fg_ref, o_ref, acc_scr,
                    *, final_norm, n_tiles):
    n_exp, cap, d = ye_ref.shape
    tm, w = x1_ref.shape[0], SLOT_WINDOW
    b, i = pl.program_id(0), pl.program_id(1)
    stride = n_tiles + 1
    lane = lax.broadcasted_iota(jnp.int32, (tm, w), 1)

    for g0 in range(0, n_exp, EXPERT_GROUP):
        pts, rows = [], []
        for e in range(g0, g0 + EXPERT_GROUP):
            a0 = _window_start(_aligned(lo_ref[b, e * stride + i]), cap)
            hit = (lane + a0) == slot_ref[:, e:e + 1]
            pts.append(jnp.where(hit, aff_ref[:, e:e + 1], 0.0).astype(BF16))
            rows.append(ye_ref[e, pl.ds(a0, w), :])
        part = _dot(jnp.concatenate(pts, axis=1), jnp.concatenate(rows, axis=0))
        if g0 == 0:
            acc_scr[...] = part
        else:
            acc_scr[...] += part

    for e in range(n_exp):
        lo, hi = lo_ref[b, e * stride + i], lo_ref[b, e * stride + i + 1]

        def window_body(k, carry, e=e, lo=lo):
            nominal = _aligned(lo) + k * w
            a = _window_start(nominal, cap)
            sc = slot_ref[:, e:e + 1]
            hit = jnp.logical_and((lane + a) == sc, sc >= nominal)
            pt = jnp.where(hit, aff_ref[:, e:e + 1], 0.0).astype(BF16)
            acc_scr[...] += _dot(pt, ye_ref[e, pl.ds(a, w), :])
            return carry

        lax.fori_loop(1, _n_windows(lo, hi), window_body, 0)

    x2 = x1_ref[...] + g2_ref[...] * acc_scr[...]
    o_ref[...] = x2 * _rms_scale(x2) * fg_ref[...] if final_norm else x2


def _combine_call(lo2, ye, slot_c, aff_c, x1, gate2, final_g, final_norm):
    b, t, d = x1.shape
    e, cap = ye.shape[1], ye.shape[2]
    tm = TOKEN_TILE
    grid_spec = pltpu.PrefetchScalarGridSpec(
        num_scalar_prefetch=1,
        grid=(b, t // tm),
        in_specs=[pl.BlockSpec((None, e, cap, d), lambda bi, i, lo: (bi, 0, 0, 0)),
                  pl.BlockSpec((None, tm, e), lambda bi, i, lo: (bi, i, 0)),
                  pl.BlockSpec((None, tm, e), lambda bi, i, lo: (bi, i, 0)),
                  pl.BlockSpec((None, tm, d), lambda bi, i, lo: (bi, i, 0)),
                  pl.BlockSpec((None, 1, d), lambda bi, i, lo: (bi, 0, 0)),
                  pl.BlockSpec((1, d), lambda bi, i, lo: (0, 0))],
        out_specs=pl.BlockSpec((None, tm, d), lambda bi, i, lo: (bi, i, 0)),
        scratch_shapes=[pltpu.VMEM((tm, d), F32)],
    )
    return pl.pallas_call(
        functools.partial(_combine_kernel, final_norm=final_norm, n_tiles=t // tm),
        grid_spec=grid_spec,
        out_shape=jax.ShapeDtypeStruct((b, t, d), F32),
        compiler_params=_params("parallel", "arbitrary"),
        name="combine",
    )(lo2, ye, slot_c, aff_c, x1, gate2, final_g)


def _layer(x, c, ctx, c_ctx, w_mod, b_mod, norm1_g, norm2_g, w_in, conv_w, conv_b, b_if,
           pool_mix, pool_scale, mlstm_norm_g, w_pool_out, w_mlstm_out, w_out,
           w_router, w_gate, w_up, w_down):
    b, t, d = x.shape
    pw = d // 2
    ng = N_DIRS * 2 * N_HEADS
    q_off, k_off, v_off, o_off = pw, pw + d, pw + 2 * d, pw + 3 * d
    if_off, gate_off = pw + 4 * d, pw + 4 * d + ng
    cap = EC_CAPACITY * t // N_EXPERTS
    row = lambda a: a.reshape(1, -1)

    rows = -(-(b + 1) // 8) * 8
    cvec = jnp.zeros((rows, d), F32).at[:b].set(c).at[b].set(c_ctx)
    mod = _mod_call(cvec, w_mod, row(b_mod))
    shift1, scale1, gate1, shift2, scale2, gate2 = [
        mod[:b, j * d:(j + 1) * d].reshape(b, 1, d) for j in range(6)]
    shift_c, scale_c = mod[b:b + 1, 0:d], mod[b:b + 1, d:2 * d]

    w_in_b = w_in.astype(BF16)
    nq = N_DIRS * N_HEADS
    w_if3 = w_in_b[:, if_off:gate_off].reshape(d, N_DIRS, 2, N_HEADS)
    b_if3 = b_if.reshape(N_DIRS, 2, N_HEADS)
    w_i, w_f = w_if3[:, :, 0, :].reshape(d, nq), w_if3[:, :, 1, :].reshape(d, nq)
    b_i, b_f = b_if3[:, 0, :].reshape(nq), b_if3[:, 1, :].reshape(nq)
    pad_w = lambda w: jnp.zeros((d, LANES), BF16).at[:, :nq].set(w)
    pad_b = lambda v: jnp.zeros((1, LANES), F32).at[0, :nq].set(v)

    c0, m0 = _ctx_call(ctx, shift_c, scale_c, row(norm1_g),
                       w_in_b[:, k_off:v_off], w_in_b[:, v_off:o_off], pad_w(w_i), pad_w(w_f),
                       conv_w[:, d:], row(conv_b[d:]), pad_b(b_i), pad_b(b_f))

    u, q, k, v, og, gg, bq, cq, mq, crow = _proj_call(
        x, shift1, scale1, row(norm1_g),
        w_in_b[:, 0:q_off], w_in_b[:, q_off:v_off], w_in_b[:, v_off:o_off], w_in_b[:, o_off:if_off],
        w_in_b[:, gate_off:], pad_w(w_i), pad_w(w_f), w_i.T, w_f.T,
        conv_w, row(conv_b), pad_b(b_i), pad_b(b_f), b_i.reshape(nq, 1), b_f.reshape(nq, 1))

    p = _pool_call(u, pool_mix.astype(BF16), row(pool_scale))
    m = _mlstm_call(q, k, v, og, bq, cq, mq, crow, c0, m0, row(mlstm_norm_g))

    x1, h2, aff_t = _merge_call(p, m, gg, x, gate1, shift2, scale2, row(norm2_g),
                                w_pool_out.astype(BF16), w_mlstm_out.astype(BF16), w_out.astype(BF16),
                                w_router.T)
    slot_t, lo = _route_call(aff_t, cap)
    lo2 = lo[:, :, :t // TOKEN_TILE + 1].reshape(b, -1)
    xe = _gather_call(lo2, h2, slot_t, cap)
    ye = _expert_call(xe, w_gate, w_up, w_down)
    return lo2, ye, jnp.swapaxes(slot_t, 1, 2), jnp.swapaxes(aff_t, 1, 2), x1, gate2


def kernel(x, c, ctx, c_ctx, w_mod, b_mod, norm1_g, norm2_g, w_in, conv_w, conv_b, b_if, pool_mix, pool_scale,
           mlstm_norm_g, w_pool_out, w_mlstm_out, w_out, w_router, w_gate, w_up, w_down, final_g):
    depth = w_mod.shape[0]
    for l in range(depth):
        lo2, ye, slot_c, aff_c, x1, gate2 = _layer(
            x, c, ctx, c_ctx, w_mod[l], b_mod[l], norm1_g[l], norm2_g[l], w_in[l], conv_w[l], conv_b[l],
            b_if[l], pool_mix[l], pool_scale[l], mlstm_norm_g[l], w_pool_out[l], w_mlstm_out[l], w_out[l],
            w_router[l], w_gate[l], w_up[l], w_down[l])
        x = _combine_call(lo2, ye, slot_c, aff_c, x1, gate2, final_g.reshape(1, -1), final_norm=l == depth - 1)
    return x
```

```python
import functools

import jax
import jax.numpy as jnp
from jax import lax
from jax.experimental import pallas as pl
from jax.experimental.pallas import tpu as pltpu

F32 = jnp.float32
BF16 = jnp.bfloat16

GRID_W = 64
POOL_WINDOWS = (2, 4, 8, 16)
N_HEADS = 4
CONV_W = 5
N_DIRS = 2
N_EXPERTS = 16
EC_CAPACITY = 2
NORM_EPS = 1e-6

CHUNK = 256
TOKEN_TILE = 512
HALO = 16
LANES = 128
AUG_ROWS = 16
V7X_VMEM_LIMIT_BYTES = 56 * 1024 * 1024

NN = (((1,), (0,)), ((), ()))
NT = (((1,), (1,)), ((), ()))
TN = (((0,), (0,)), ((), ()))


def _dot(a, b, dims=NN):
    return lax.dot_general(a, b, dims, preferred_element_type=F32)


def _split2(a):
    hi = a.astype(BF16)
    lo = (a - hi.astype(F32)).astype(BF16)
    return hi, lo


def _split3(a):
    a1 = a.astype(BF16)
    r1 = a - a1.astype(F32)
    a2 = r1.astype(BF16)
    a3 = (r1 - a2.astype(F32)).astype(BF16)
    return a1, a2, a3


def _dot3(a, b, dims=NN):
    ah, al = _split2(a)
    bh, bl = _split2(b)
    return _dot(ah, bh, dims) + _dot(ah, bl, dims) + _dot(al, bh, dims)


def _dot_left01(t01, a):
    a1, a2, a3 = _split3(a)
    return _dot(t01, a1) + _dot(t01, a2) + _dot(t01, a3)


def _dot_right01(a, t01):
    a1, a2, a3 = _split3(a)
    return _dot(a1, t01) + _dot(a2, t01) + _dot(a3, t01)


def _silu(x):
    return x * jax.nn.sigmoid(x)


def _log_sigmoid(x):
    return jnp.minimum(x, 0.0) - jnp.log1p(jnp.exp(-jnp.abs(x)))


def _rms_scale(x):
    return lax.rsqrt(jnp.mean(x * x, axis=-1, keepdims=True) + NORM_EPS)


def _tri01(n, kind):
    i = lax.broadcasted_iota(jnp.int32, (n, n), 0)
    j = lax.broadcasted_iota(jnp.int32, (n, n), 1)
    cond = {"le": j <= i, "ge": j >= i, "lt": j < i, "gt": j > i}[kind]
    return jnp.where(cond, 1.0, 0.0).astype(BF16)


def _shift(n):
    assert n & (n - 1) == 0, n
    return n.bit_length() - 1


def _div_pow2(x, n):
    return lax.shift_right_logical(x, _shift(n))


def _mod_pow2(x, n):
    return jnp.bitwise_and(x, n - 1)


def _params(*sem):
    return pltpu.CompilerParams(dimension_semantics=sem, vmem_limit_bytes=V7X_VMEM_LIMIT_BYTES)


def _resident(shape):
    nd = len(shape)
    return pl.BlockSpec(shape, lambda *_: (0,) * nd)


def _mod_kernel(c_ref, w_ref, b_ref, o_ref):
    o_ref[...] = _dot3(_silu(c_ref[...]), w_ref[...]) + b_ref[...]


def _mod_call(cvec, w_mod, b_mod):
    rows, d = cvec.shape
    n = w_mod.shape[1]
    tn = 1536
    return pl.pallas_call(
        _mod_kernel,
        grid=(n // tn,),
        in_specs=[pl.BlockSpec((rows, d), lambda j: (0, 0)),
                  pl.BlockSpec((d, tn), lambda j: (0, j)),
                  pl.BlockSpec((1, tn), lambda j: (0, j))],
        out_specs=pl.BlockSpec((rows, tn), lambda j: (0, j)),
        out_shape=jax.ShapeDtypeStruct((rows, n), F32),
        compiler_params=_params("parallel"),
        name="mod",
    )(cvec, w_mod, b_mod)


def _ctx_kernel(ctx_ref, sh_ref, sc_ref, g_ref, wk_ref, wvt_ref, wi_ref, wf_ref, cw_ref, cb_ref, bi_ref, bf_ref,
                c_out, m_out):
    lc, d = ctx_ref.shape
    dh = d // N_HEADS
    x = ctx_ref[...]
    hc = (x * _rms_scale(x) * g_ref[...]) * (1.0 + sc_ref[...]) + sh_ref[...]
    hcb = hc.astype(BF16)

    kpre = _dot(hcb, wk_ref[...])
    pad = jnp.zeros((8, d), F32)
    kp = jnp.concatenate([pad, kpre, pad], axis=0)
    cw = cw_ref[...]
    acc = cb_ref[...] + cw[0:1, :] * kp[6:6 + lc, :]
    for j in range(1, CONV_W):
        acc = acc + cw[j:j + 1, :] * kp[6 + j:6 + j + lc, :]
    k = _silu(acc) * (dh ** -0.5)
    vt = _dot(wvt_ref[...], hcb, NT).astype(BF16)

    gi = _dot(hcb, wi_ref[...]) + bi_ref[...]
    lf = _log_sigmoid(_dot(hcb, wf_ref[...]) + bf_ref[...])
    lane = lax.broadcasted_iota(jnp.int32, lf.shape, 1)
    w_all = gi + jnp.where(lane < N_HEADS, _dot_left01(_tri01(lc, "gt"), lf), _dot_left01(_tri01(lc, "lt"), lf))
    ones = jnp.ones((AUG_ROWS, lc), BF16)
    for dr in range(N_DIRS):
        for h in range(N_HEADS):
            col = dr * N_HEADS + h
            w = w_all[:, col:col + 1]
            m = jnp.max(w, axis=0, keepdims=True)
            wk = jnp.exp(w - m) * k[:, h * dh:(h + 1) * dh]
            vt_aug = jnp.concatenate([vt[h * dh:(h + 1) * dh, :], ones], axis=0)
            c_out[dr, h] = _dot(vt_aug, wk.astype(BF16))
            m_out[dr, h] = jnp.broadcast_to(m, (1, LANES))


def _ctx_call(ctx, sh_c, sc_c, g1, wk, wv, wi, wf, cw_k, cb_k, bi, bf):
    b, lc, d = ctx.shape
    dh = d // N_HEADS
    row = lambda w: pl.BlockSpec((1, w), lambda i: (0, 0))
    return pl.pallas_call(
        _ctx_kernel,
        grid=(b,),
        in_specs=[pl.BlockSpec((None, lc, d), lambda i: (i, 0, 0)),
                  row(d), row(d), row(d),
                  pl.BlockSpec((d, d), lambda i: (0, 0)),
                  pl.BlockSpec((d, d), lambda i: (0, 0)),
                  pl.BlockSpec((d, LANES), lambda i: (0, 0)),
                  pl.BlockSpec((d, LANES), lambda i: (0, 0)),
                  pl.BlockSpec((CONV_W, d), lambda i: (0, 0)),
                  row(d), row(LANES), row(LANES)],
        out_specs=[pl.BlockSpec((None, N_DIRS, N_HEADS, dh + AUG_ROWS, dh), lambda i: (i, 0, 0, 0, 0)),
                   pl.BlockSpec((None, N_DIRS, N_HEADS, 1, LANES), lambda i: (i, 0, 0, 0, 0))],
        out_shape=[jax.ShapeDtypeStruct((b, N_DIRS, N_HEADS, dh + AUG_ROWS, dh), F32),
                   jax.ShapeDtypeStruct((b, N_DIRS, N_HEADS, 1, LANES), F32)],
        compiler_params=_params("parallel"),
        name="ctx_states",
    )(ctx, sh_c, sc_c, g1, wk, wv, wi, wf, cw_k, cb_k, bi, bf)


def _cummax_lanes(x, reverse):
    n = x.shape[-1]
    lane = lax.broadcasted_iota(jnp.int32, x.shape, x.ndim - 1)
    k = 1
    while k < n:
        if reverse:
            shifted = jnp.where(lane < n - k, pltpu.roll(x, n - k, axis=x.ndim - 1), -jnp.inf)
        else:
            shifted = jnp.where(lane >= k, pltpu.roll(x, k, axis=x.ndim - 1), -jnp.inf)
        x = jnp.maximum(x, shifted)
        k *= 2
    return x


def _proj_kernel(xp_ref, x_ref, xn_ref, sh_ref, sc_ref, g_ref,
                 wpool_ref, wqk_ref, wvt_ref, wo_ref, wg_ref, wi_ref, wf_ref, wit_ref, wft_ref,
                 cw_ref, cb_ref, bi_ref, bf_ref, bit_ref, bft_ref,
                 u_out, q_out, k_out, vt_out, og_out, gg_out, cq_out, rows_out,
                 hx_scr, r_scr):
    tm, d = x_ref.shape
    dh = d // N_HEADS
    i = pl.program_id(1)
    last = pl.num_programs(1) - 1

    x_ext = jnp.concatenate([xp_ref[...], x_ref[...], xn_ref[...]], axis=0)
    hx = (x_ext * _rms_scale(x_ext) * g_ref[...]) * (1.0 + sc_ref[...]) + sh_ref[...]
    n_ext = tm + 2 * HALO
    r_id = lax.broadcasted_iota(jnp.int32, (n_ext, 1), 0)
    valid = jnp.logical_and(jnp.logical_or(i > 0, r_id >= HALO),
                            jnp.logical_or(i < last, r_id < HALO + tm))
    hx_scr[...] = jnp.where(valid, hx, 0.0).astype(BF16)
    hxc = hx_scr[HALO:HALO + tm, :]

    nc = 512
    half = CONV_W // 2

    def qk_dot(c):
        r_scr[c % 2] = _dot(hx_scr[...], wqk_ref[:, c * nc:(c + 1) * nc])

    def qk_conv(c):
        cols = slice(c * nc, (c + 1) * nc)
        r = r_scr.at[c % 2]
        cw = cw_ref[:, cols]
        acc = cb_ref[:, cols] + cw[0:1, :] * r[HALO - half:HALO - half + tm, :]
        for j in range(1, CONV_W):
            acc = acc + cw[j:j + 1, :] * r[HALO - half + j:HALO - half + j + tm, :]
        y = _silu(acc)
        if c * nc < d:
            q_out[:, cols] = y.astype(BF16)
        else:
            k_out[:, c * nc - d:(c + 1) * nc - d] = (y * (dh ** -0.5)).astype(BF16)

    def v_chunk(c):
        cols = slice(c * nc, (c + 1) * nc)
        vt_out[cols, :] = _dot(wvt_ref[cols, :], hxc, NT).astype(BF16)

    def o_chunk(c):
        cols = slice(c * nc, (c + 1) * nc)
        og_out[:, cols] = jax.nn.sigmoid(_dot(hxc, wo_ref[:, cols])).astype(BF16)

    def g_chunk(c):
        cols = slice(c * nc, (c + 1) * nc)
        gg_out[:, cols] = jax.nn.sigmoid(_dot(hxc, wg_ref[:, cols])).astype(BF16)

    qk_dot(0); qk_dot(1)
    qk_conv(0); v_chunk(0); v_chunk(1); qk_dot(2)
    qk_conv(1); o_chunk(0); o_chunk(1); qk_dot(3)
    qk_conv(2); g_chunk(0); g_chunk(1)
    qk_conv(3); g_chunk(2); g_chunk(3)
    u_out[...] = _dot(hxc, wpool_ref[...]).astype(BF16)

    nq = N_DIRS * N_HEADS
    gi = _dot(hxc, wi_ref[...]) + bi_ref[...]
    lf = _log_sigmoid(_dot(hxc, wf_ref[...]) + bf_ref[...])
    gi_r = _dot(wit_ref[...], hxc, NT) + bit_ref[...]
    lf_r = _log_sigmoid(_dot(wft_ref[...], hxc, NT) + bft_ref[...])
    fwd_lane = lax.broadcasted_iota(jnp.int32, (CHUNK, LANES), 1) < N_HEADS
    fwd_sub = lax.broadcasted_iota(jnp.int32, (nq, CHUNK), 0) < N_HEADS
    t_le, t_ge = _tri01(CHUNK, "le"), _tri01(CHUNK, "ge")
    for j in range(tm // CHUNK):
        rows = slice(j * CHUNK, (j + 1) * CHUNK)
        b = jnp.where(fwd_lane, _dot_left01(t_le, lf[rows, :]), _dot_left01(t_ge, lf[rows, :]))
        cq_out[rows, :] = (gi[rows, :] - b)[:, :nq]
        b_r = jnp.where(fwd_sub, _dot_right01(lf_r[:, rows], t_ge), _dot_right01(lf_r[:, rows], t_le))
        c_r = gi_r[:, rows] - b_r
        rows_out[0:nq, rows] = b_r
        rows_out[nq:2 * nq, rows] = jnp.where(fwd_sub, _cummax_lanes(c_r, False), _cummax_lanes(c_r, True))


def _proj_call(x, sh, sc, g1, wpool, wqk, wvt, wo, wg, wi, wf, wit, wft, cw, cb, bi, bf, bit, bft):
    b, t, d = x.shape
    tm = TOKEN_TILE
    nt = t // tm
    hb = tm // HALO
    nq = N_DIRS * N_HEADS
    per_b = pl.BlockSpec((None, 1, d), lambda bi_, i: (bi_, 0, 0))
    tile = lambda w: pl.BlockSpec((None, tm, w), lambda bi_, i: (bi_, i, 0))
    tile_t = lambda h: pl.BlockSpec((None, h, tm), lambda bi_, i: (bi_, 0, i))
    const = lambda a: pl.BlockSpec(a.shape, lambda bi_, i: (0,) * a.ndim)
    out_shapes = [jax.ShapeDtypeStruct((b, t, d // 2), BF16),
                  jax.ShapeDtypeStruct((b, t, d), BF16),
                  jax.ShapeDtypeStruct((b, t, d), BF16),
                  jax.ShapeDtypeStruct((b, d, t), BF16),
                  jax.ShapeDtypeStruct((b, t, d), BF16),
                  jax.ShapeDtypeStruct((b, t, 2 * d), BF16),
                  jax.ShapeDtypeStruct((b, t, nq), F32),
                  jax.ShapeDtypeStruct((b, 2 * nq, t), F32)]
    out_specs = [tile(d // 2), tile(d), tile(d), tile_t(d), tile(d), tile(2 * d),
                 tile(nq), tile_t(2 * nq)]
    return pl.pallas_call(
        _proj_kernel,
        grid=(b, nt),
        in_specs=[pl.BlockSpec((None, HALO, d), lambda bi_, i: (bi_, jnp.maximum(i * hb - 1, 0), 0)),
                  tile(d),
                  pl.BlockSpec((None, HALO, d), lambda bi_, i: (bi_, jnp.minimum((i + 1) * hb, t // HALO - 1), 0)),
                  per_b, per_b, const(g1),
                  const(wpool), const(wqk), const(wvt), const(wo), const(wg),
                  const(wi), const(wf), const(wit), const(wft),
                  const(cw), const(cb), const(bi), const(bf), const(bit), const(bft)],
        out_specs=out_specs,
        out_shape=out_shapes,
        scratch_shapes=[pltpu.VMEM((tm + 2 * HALO, d), BF16),
                        pltpu.VMEM((2, tm + 2 * HALO, 512), F32)],
        compiler_params=_params("parallel", "parallel"),
        name="proj",
    )(x, x, x, sh, sc, g1, wpool, wqk, wvt, wo, wg, wi, wf, wit, wft, cw, cb, bi, bf, bit, bft)


def _pool_kernel(u_ref, mix_ref, scale_ref, p_out, pad_scr):
    t, pw = u_ref.shape
    gw = pw // len(POOL_WINDOWS)
    rows = t // GRID_W
    tile = 256
    maxlo = max(POOL_WINDOWS) // 2
    padr = maxlo * GRID_W
    pad_scr[0:padr, :] = jnp.zeros((padr, gw), F32)
    pad_scr[padr + t:padr + t + padr, :] = jnp.zeros((padr, gw), F32)

    ti = lax.broadcasted_iota(jnp.int32, (tile, tile), 0)
    tj = lax.broadcasted_iota(jnp.int32, (tile, tile), 1)
    same_row = _div_pow2(ti, GRID_W) == _div_pow2(tj, GRID_W)
    ci, cj = _mod_pow2(ti, GRID_W), _mod_pow2(tj, GRID_W)
    tok = lax.broadcasted_iota(jnp.int32, (t, gw), 0)
    r_id, c_id = _div_pow2(tok, GRID_W), _mod_pow2(tok, GRID_W)

    for g, side in enumerate(POOL_WINDOWS):
        lo, hi = side // 2, side - side // 2
        cols = slice(g * gw, (g + 1) * gw)
        band = jnp.logical_and(same_row, jnp.logical_and(cj >= ci - lo, cj < ci + hi))
        pw01 = jnp.where(band, 1.0, 0.0).astype(BF16)
        for k in range(t // tile):
            rs = slice(k * tile, (k + 1) * tile)
            pad_scr[padr + k * tile:padr + (k + 1) * tile, :] = _dot(pw01, u_ref[rs, cols])
        tot = pad_scr[padr - lo * GRID_W:padr - lo * GRID_W + t, :]
        for dlt in range(-lo + 1, hi):
            tot = tot + pad_scr[padr + dlt * GRID_W:padr + dlt * GRID_W + t, :]
        cnt = ((jnp.minimum(r_id + hi, rows) - jnp.maximum(r_id - lo, 0))
               * (jnp.minimum(c_id + hi, GRID_W) - jnp.maximum(c_id - lo, 0))).astype(F32)
        a = tot / cnt - u_ref[:, cols].astype(F32)
        p = _dot(a.astype(BF16), mix_ref[g]) * scale_ref[:, cols]
        p_out[:, cols] = p.astype(BF16)


def _pool_call(u, mix, scale):
    b, t, pw = u.shape
    gw = pw // len(POOL_WINDOWS)
    padr = (max(POOL_WINDOWS) // 2) * GRID_W
    return pl.pallas_call(
        _pool_kernel,
        grid=(b,),
        in_specs=[pl.BlockSpec((None, t, pw), lambda i: (i, 0, 0)),
                  pl.BlockSpec(mix.shape, lambda i: (0, 0, 0)),
                  pl.BlockSpec((1, pw), lambda i: (0, 0))],
        out_specs=pl.BlockSpec((None, t, pw), lambda i: (i, 0, 0)),
        out_shape=jax.ShapeDtypeStruct((b, t, pw), BF16),
        scratch_shapes=[pltpu.VMEM((t + 2 * padr, gw), F32)],
        compiler_params=_params("parallel"),
        name="pool",
    )(u, mix, scale)


def _mlstm_dir(q_ref, k_ref, vt_ref, cq_ref, rows_ref, c_scr, m_scr, reverse):
    L, d = q_ref.shape
    dh = d // N_HEADS
    nhalf = dh // LANES
    nq = N_DIRS * N_HEADS
    si = lax.broadcasted_iota(jnp.int32, (L, L), 0)
    tj = lax.broadcasted_iota(jnp.int32, (L, L), 1)
    mask = (si >= tj) if reverse else (si <= tj)
    ones = jnp.ones((AUG_ROWS, L), BF16)
    end = 0 if reverse else L - 1
    off = N_HEADS if reverse else 0

    hs_all = []
    for h in range(N_HEADS):
        hs = slice(h * dh, (h + 1) * dh)
        st = off + h
        q = q_ref[:, hs]
        k = k_ref[:, hs]
        vt_aug = jnp.concatenate([vt_ref[hs, :], ones], axis=0)
        c_c = cq_ref[:, st:st + 1]
        b_r = rows_ref[st:st + 1, :]
        cm_r = rows_ref[nq + st:nq + st + 1, :]
        m_prev = m_scr[st][:, 0:1]
        ct_prev = c_scr[st]

        mm = jnp.maximum(cm_r, m_prev)
        w_inter = jnp.exp(m_prev - mm)
        st_mat = (_dot(k, q, NT) * jnp.exp(jnp.where(mask, c_c - mm, -jnp.inf))).astype(BF16)
        intra = _dot(vt_aug, st_mat)
        inter = _dot(ct_prev.astype(BF16), q, NT)
        den = w_inter * inter[dh:dh + 1, :] + intra[dh:dh + 1, :]
        inv = 1.0 / jnp.maximum(jnp.abs(den), jnp.exp(-(b_r + mm)))
        hs_all.append((w_inter * inter[0:dh, :] + intra[0:dh, :]) * inv)

        g_tot = b_r[:, end:end + 1]
        m_new = g_tot + jnp.maximum(m_prev, cm_r[:, end:end + 1])
        decay = jnp.exp(g_tot + m_prev - m_new)
        wk = (k.astype(F32) * jnp.exp(g_tot + c_c - m_new)).astype(BF16)
        c_scr[st] = decay * ct_prev + _dot(vt_aug, wk)
        m_scr[st] = jnp.broadcast_to(m_new, (1, LANES))
    return hs_all


def _mlstm_kernel(qf_ref, kf_ref, vf_ref, ogf_ref, cqf_ref, rwf_ref,
                  qb_ref, kb_ref, vb_ref, ogb_ref, cqb_ref, rwb_ref,
                  c0_ref, m0_ref, ng_ref, o_ref,
                  c_scr, m_scr, hf_scr, hb_scr):
    L, d = qf_ref.shape
    dh = d // N_HEADS
    nhalf = dh // LANES
    s = pl.program_id(1)
    nch = pl.num_programs(1)
    half = nch // 2

    @pl.when(s == 0)
    def _():
        for j in range(N_DIRS * N_HEADS):
            c_scr[j] = c0_ref[j // N_HEADS, j % N_HEADS]
            m_scr[j] = m0_ref[j // N_HEADS, j % N_HEADS]

    h_f = _mlstm_dir(qf_ref, kf_ref, vf_ref, cqf_ref, rwf_ref, c_scr, m_scr, False)
    h_b = _mlstm_dir(qb_ref, kb_ref, vb_ref, cqb_ref, rwb_ref, c_scr, m_scr, True)

    @pl.when(s < half)
    def _():
        for h in range(N_HEADS):
            hs = slice(h * dh, (h + 1) * dh)
            hf_scr[s, hs, :] = h_f[h]
            hb_scr[half - 1 - s, hs, :] = h_b[h]

    @pl.when(s >= half)
    def _():
        def finish(ht, h, og_ref, out):
            hs = slice(h * dh, (h + 1) * dh)
            scale = lax.rsqrt(jnp.mean(ht * ht, axis=0, keepdims=True) + NORM_EPS)
            y = jnp.concatenate([ht[:, i * LANES:(i + 1) * LANES] * scale[:, i * LANES:(i + 1) * LANES]
                                 * ng_ref[hs, :] for i in range(L // LANES)], axis=1).T
            out[:, hs] = (y * og_ref[:, hs].astype(F32)).astype(BF16)

        for h in range(N_HEADS):
            hs = slice(h * dh, (h + 1) * dh)
            finish(h_f[h] + hb_scr[s - half, hs, :], h, ogf_ref, o_ref.at[1])
            finish(h_b[h] + hf_scr[nch - 1 - s, hs, :], h, ogb_ref, o_ref.at[0])


def _mlstm_call(q, k, vt, og, cq, rows, c0, m0, norm_g):
    b, t, d = q.shape
    dh = d // N_HEADS
    L = CHUNK
    nch = t // L
    half = nch // 2
    assert nch % 2 == 0
    nq = N_DIRS * N_HEADS

    def specs(chunk):
        seq = lambda w: pl.BlockSpec((None, L, w), lambda bi, s: (bi, chunk(s), 0))
        seq_t = lambda h: pl.BlockSpec((None, h, L), lambda bi, s: (bi, 0, chunk(s)))
        return [seq(d), seq(d), seq_t(d), seq(d), seq(nq), seq_t(2 * nq)]

    state = lambda w0, w1: pl.BlockSpec((None, N_DIRS, N_HEADS, w0, w1), lambda bi, s: (bi, 0, 0, 0, 0))
    return pl.pallas_call(
        _mlstm_kernel,
        grid=(b, nch),
        in_specs=specs(lambda s: s) + specs(lambda s: nch - 1 - s) + [
            state(dh + AUG_ROWS, dh), state(1, LANES), pl.BlockSpec((d, LANES), lambda bi, s: (0, 0))],
        out_specs=pl.BlockSpec((None, 2, None, L, d), lambda bi, s: (bi, 0, jnp.maximum(s - half, 0), 0, 0)),
        out_shape=jax.ShapeDtypeStruct((b, 2, half, L, d), BF16),
        scratch_shapes=[pltpu.VMEM((N_DIRS * N_HEADS, dh + AUG_ROWS, dh), F32),
                        pltpu.VMEM((N_DIRS * N_HEADS, 1, LANES), F32),
                        pltpu.VMEM((half, d, L), F32),
                        pltpu.VMEM((half, d, L), F32)],
        compiler_params=_params("parallel", "arbitrary"),
        name="mlstm",
    )(q, k, vt, og, cq, rows, q, k, vt, og, cq, rows, c0, m0, norm_g)


def _merge_kernel(p_ref, m_ref, gg_ref, x_ref, g1_ref, sh2_ref, sc2_ref, n2_ref,
                  wpo_ref, wmo_ref, wout_ref, wr_ref, x1_out, h2_out, aff_out):
    tm, d = x_ref.shape
    cpt = m_ref.shape[0]
    upper = pl.program_id(1) >= pl.num_programs(1) // 2
    m = jnp.where(upper, jnp.concatenate([m_ref[j] for j in range(cpt)], axis=0),
                  jnp.concatenate([m_ref[cpt - 1 - j] for j in range(cpt)], axis=0))
    a = _dot(p_ref[...], wpo_ref[...])
    mm = _dot(m, wmo_ref[...])
    mixed = gg_ref[:, 0:d].astype(F32) * a + gg_ref[:, d:2 * d].astype(F32) * mm
    x1 = x_ref[...] + g1_ref[...] * _dot(mixed.astype(BF16), wout_ref[...])
    x1_out[...] = x1
    h2 = (x1 * _rms_scale(x1) * n2_ref[...]) * (1.0 + sc2_ref[...]) + sh2_ref[...]
    h2_out[...] = h2.astype(BF16)
    logits = _dot3(wr_ref[...], h2, NT)
    z = jnp.exp(logits - jnp.max(logits, axis=0, keepdims=True))
    aff_out[...] = z / jnp.sum(z, axis=0, keepdims=True)


def _merge_call(p, m, gg, x, gate1, sh2, sc2, n2, wpo, wmo, wout, wr_t):
    b, t, d = x.shape
    tm = TOKEN_TILE
    e = wr_t.shape[0]
    per_b = pl.BlockSpec((None, 1, d), lambda bi, i: (bi, 0, 0))
    tile = lambda w: pl.BlockSpec((None, tm, w), lambda bi, i: (bi, i, 0))
    const = lambda a: pl.BlockSpec(a.shape, lambda bi, i: (0,) * a.ndim)
    cpt = tm // CHUNK
    nth = t // tm // 2
    m_spec = pl.BlockSpec((None, None, cpt, CHUNK, d),
                          lambda bi, i: (bi, i // nth, jnp.where(i >= nth, i - nth, nth - 1 - i), 0, 0))
    return pl.pallas_call(
        _merge_kernel,
        grid=(b, t // tm),
        in_specs=[tile(d // 2), m_spec, tile(2 * d), tile(d), per_b, per_b, per_b, const(n2),
                  const(wpo), const(wmo), const(wout), const(wr_t)],
        out_specs=[tile(d), tile(d), pl.BlockSpec((None, e, tm), lambda bi, i: (bi, 0, i))],
        out_shape=[jax.ShapeDtypeStruct((b, t, d), F32),
                   jax.ShapeDtypeStruct((b, t, d), BF16),
                   jax.ShapeDtypeStruct((b, e, t), F32)],
        compiler_params=_params("parallel", "parallel"),
        name="merge",
    )(p, m, gg, x, gate1, sh2, sc2, n2, wpo, wmo, wout, wr_t)


def _route_kernel(aff_ref, slot_out, lo_out, *, cap):
    e, t = aff_ref.shape
    aff = aff_ref[...]

    def step(i, thr):
        cand = thr | (jnp.int32(1) << (30 - i))
        cnt = jnp.sum(jnp.where(aff >= pltpu.bitcast(cand, F32), 1.0, 0.0), axis=-1, keepdims=True)
        return jnp.where(cnt >= cap, cand, thr)

    thr = pltpu.bitcast(lax.fori_loop(0, 31, step, jnp.zeros((e, 1), jnp.int32)), F32)
    gt = aff > thr
    eq = aff == thr
    need = cap - jnp.sum(jnp.where(gt, 1.0, 0.0), axis=-1, keepdims=True).astype(jnp.int32)

    seg = 256
    t_ge = _tri01(seg, "ge")

    def prefix_incl(x01):
        outs, carries, carry = [], [], jnp.zeros((e, 1), F32)
        for j in range(t // seg):
            p = _dot(x01[:, j * seg:(j + 1) * seg].astype(BF16), t_ge) + carry
            outs.append(p)
            carry = p[:, seg - 1:seg]
            carries.append(carry)
        return jnp.concatenate(outs, axis=1), carries

    eq_f = jnp.where(eq, 1.0, 0.0)
    tie_rank = (prefix_incl(eq_f)[0] - eq_f).astype(jnp.int32)
    sel = jnp.logical_or(gt, jnp.logical_and(eq, tie_rank < need))
    rank, carries = prefix_incl(jnp.where(sel, 1.0, 0.0))
    slot_out[...] = jnp.where(sel, rank.astype(jnp.int32) - 1, -1)

    lane = lax.broadcasted_iota(jnp.int32, (e, LANES), 1)
    lo = jnp.zeros((e, LANES), F32)
    per_tile = TOKEN_TILE // seg
    for c in range(1, t // TOKEN_TILE + 1):
        lo = jnp.where(lane == c, carries[c * per_tile - 1], lo)
    lo_out[...] = lo.astype(jnp.int32)


def _route_call(aff_t, cap):
    b, e, t = aff_t.shape
    return pl.pallas_call(
        functools.partial(_route_kernel, cap=cap),
        grid=(b,),
        in_specs=[pl.BlockSpec((None, e, t), lambda i: (i, 0, 0))],
        out_specs=[pl.BlockSpec((None, e, t), lambda i: (i, 0, 0)),
                   pl.BlockSpec((None, e, LANES), lambda i: (i, 0, 0))],
        out_shape=[jax.ShapeDtypeStruct((b, e, t), jnp.int32),
                   jax.ShapeDtypeStruct((b, e, LANES), jnp.int32)],
        compiler_params=_params("parallel"),
        name="route",
    )(aff_t)


SLOT_WINDOW = 128
SLOT_ALIGN = 16
EXPERT_GROUP = 4


def _aligned(lo):
    return jnp.bitwise_and(lo, -SLOT_ALIGN)


def _window_start(nominal, cap):
    return pl.multiple_of(jnp.minimum(nominal, cap - SLOT_WINDOW), SLOT_ALIGN)


def _n_windows(lo, hi):
    return lax.shift_right_logical(hi - _aligned(lo) + (SLOT_WINDOW - 1), _shift(SLOT_WINDOW))


def _gather_kernel(lo_ref, h2_ref, slot_ref, xe_out, *, cap, n_tiles):
    n_exp = slot_ref.shape[0]
    tc, w = TOKEN_TILE, SLOT_WINDOW
    b = pl.program_id(0)
    stride = n_tiles + 1
    xe_out[...] = jnp.zeros(xe_out.shape, BF16)
    s_id = lax.broadcasted_iota(jnp.int32, (w, tc), 0)

    def add_rows(e, start, z):
        xe_out[e, pl.ds(start, w), :] = xe_out[e, pl.ds(start, w), :] + z.astype(BF16)

    def tile_body(c, carry):
        t0 = pl.multiple_of(c * tc, tc)
        for g0 in range(0, n_exp, EXPERT_GROUP):
            starts, blocks = [], []
            for e in range(g0, g0 + EXPERT_GROUP):
                a0 = _window_start(_aligned(lo_ref[b, e * stride + c]), cap)
                hit = (s_id + a0) == slot_ref[e:e + 1, pl.ds(t0, tc)]
                blocks.append(jnp.where(hit, 1.0, 0.0).astype(BF16))
                starts.append(a0)
            z = _dot(jnp.concatenate(blocks, axis=0), h2_ref[pl.ds(t0, tc), :])
            for j in range(EXPERT_GROUP):
                add_rows(g0 + j, starts[j], z[j * w:(j + 1) * w, :])
        for e in range(n_exp):
            lo, hi = lo_ref[b, e * stride + c], lo_ref[b, e * stride + c + 1]

            def window_body(k, carry2, e=e, lo=lo):
                nominal = _aligned(lo) + k * w
                a = _window_start(nominal, cap)
                srow = slot_ref[e:e + 1, pl.ds(t0, tc)]
                hit = jnp.logical_and((s_id + a) == srow, srow >= nominal)
                add_rows(e, a, _dot(jnp.where(hit, 1.0, 0.0).astype(BF16), h2_ref[pl.ds(t0, tc), :]))
                return carry2

            lax.fori_loop(1, _n_windows(lo, hi), window_body, 0)
        return carry

    lax.fori_loop(0, n_tiles, tile_body, 0)


def _gather_call(lo2, h2, slot_t, cap):
    b, t, d = h2.shape
    e = slot_t.shape[1]
    n_tiles = t // TOKEN_TILE
    grid_spec = pltpu.PrefetchScalarGridSpec(
        num_scalar_prefetch=1,
        grid=(b,),
        in_specs=[pl.BlockSpec((None, t, d), lambda i, lo: (i, 0, 0)),
                  pl.BlockSpec((None, e, t), lambda i, lo: (i, 0, 0))],
        out_specs=pl.BlockSpec((None, e, cap, d), lambda i, lo: (i, 0, 0, 0)),
    )
    return pl.pallas_call(
        functools.partial(_gather_kernel, cap=cap, n_tiles=n_tiles),
        grid_spec=grid_spec,
        out_shape=jax.ShapeDtypeStruct((b, e, cap, d), BF16),
        compiler_params=_params("arbitrary"),
        name="gather",
    )(lo2, h2, slot_t)


def _expert_kernel(xe_ref, wg_ref, wu_ref, wd_ref, ye_out, wg_scr, wu_scr, wd_scr):
    f = wg_ref.shape[1]

    @pl.when(pl.program_id(1) == 0)
    def _():
        wg_scr[...] = wg_ref[...].astype(BF16)
        wu_scr[...] = wu_ref[...].astype(BF16)
        wd_scr[...] = wd_ref[...].astype(BF16)

    xe = xe_ref[...]
    fc = 512
    y = None
    for c in range(f // fc):
        cols = slice(c * fc, (c + 1) * fc)
        hid = _silu(_dot(xe, wg_scr[:, cols])) * _dot(xe, wu_scr[:, cols])
        part = _dot(hid.astype(BF16), wd_scr[cols, :])
        y = part if y is None else y + part
    ye_out[...] = y.astype(BF16)


def _expert_call(xe, wg, wu, wd):
    b, e, cap, d = xe.shape
    f = wg.shape[2]
    return pl.pallas_call(
        _expert_kernel,
        grid=(e, b),
        in_specs=[pl.BlockSpec((None, None, cap, d), lambda ei, bi: (bi, ei, 0, 0)),
                  pl.BlockSpec((None, d, f), lambda ei, bi: (ei, 0, 0)),
                  pl.BlockSpec((None, d, f), lambda ei, bi: (ei, 0, 0)),
                  pl.BlockSpec((None, f, d), lambda ei, bi: (ei, 0, 0))],
        out_specs=pl.BlockSpec((None, None, cap, d), lambda ei, bi: (bi, ei, 0, 0)),
        out_shape=jax.ShapeDtypeStruct((b, e, cap, d), BF16),
        scratch_shapes=[pltpu.VMEM((d, f), BF16), pltpu.VMEM((d, f), BF16), pltpu.VMEM((f, d), BF16)],
        compiler_params=_params("parallel", "arbitrary"),
        name="experts",
    )(xe, wg, wu, wd)


def _combine_kernel(lo_ref, ye_ref, slot_ref, aff_ref, x1_ref, g2_ref, fg_ref, o_ref, acc_scr,
                    *, final_norm, n_tiles):
    n_exp, cap, d = ye_ref.shape
    tm, w = x1_ref.shape[0], SLOT_WINDOW
    b, i = pl.program_id(0), pl.program_id(1)
    stride = n_tiles + 1
    lane = lax.broadcasted_iota(jnp.int32, (tm, w), 1)

    for g0 in range(0, n_exp, EXPERT_GROUP):
        pts, rows = [], []
        for e in range(g0, g0 + EXPERT_GROUP):
            a0 = _window_start(_aligned(lo_ref[b, e * stride + i]), cap)
            hit = (lane + a0) == slot_ref[:, e:e + 1]
            pts.append(jnp.where(hit, aff_ref[:, e:e + 1], 0.0).astype(BF16))
            rows.append(ye_ref[e, pl.ds(a0, w), :])
        part = _dot(jnp.concatenate(pts, axis=1), jnp.concatenate(rows, axis=0))
        if g0 == 0:
            acc_scr[...] = part
        else:
            acc_scr[...] += part

    for e in range(n_exp):
        lo, hi = lo_ref[b, e * stride + i], lo_ref[b, e * stride + i + 1]

        def window_body(k, carry, e=e, lo=lo):
            nominal = _aligned(lo) + k * w
            a = _window_start(nominal, cap)
            sc = slot_ref[:, e:e + 1]
            hit = jnp.logical_and((lane + a) == sc, sc >= nominal)
            pt = jnp.where(hit, aff_ref[:, e:e + 1], 0.0).astype(BF16)
            acc_scr[...] += _dot(pt, ye_ref[e, pl.ds(a, w), :])
            return carry

        lax.fori_loop(1, _n_windows(lo, hi), window_body, 0)

    x2 = x1_ref[...] + g2_ref[...] * acc_scr[...]
    o_ref[...] = x2 * _rms_scale(x2) * fg_ref[...] if final_norm else x2


def _combine_call(lo2, ye, slot_c, aff_c, x1, gate2, final_g, final_norm):
    b, t, d = x1.shape
    e, cap = ye.shape[1], ye.shape[2]
    tm = TOKEN_TILE
    grid_spec = pltpu.PrefetchScalarGridSpec(
        num_scalar_prefetch=1,
        grid=(b, t // tm),
        in_specs=[pl.BlockSpec((None, e, cap, d), lambda bi, i, lo: (bi, 0, 0, 0)),
                  pl.BlockSpec((None, tm, e), lambda bi, i, lo: (bi, i, 0)),
                  pl.BlockSpec((None, tm, e), lambda bi, i, lo: (bi, i, 0)),
                  pl.BlockSpec((None, tm, d), lambda bi, i, lo: (bi, i, 0)),
                  pl.BlockSpec((None, 1, d), lambda bi, i, lo: (bi, 0, 0)),
                  pl.BlockSpec((1, d), lambda bi, i, lo: (0, 0))],
        out_specs=pl.BlockSpec((None, tm, d), lambda bi, i, lo: (bi, i, 0)),
        scratch_shapes=[pltpu.VMEM((tm, d), F32)],
    )
    return pl.pallas_call(
        functools.partial(_combine_kernel, final_norm=final_norm, n_tiles=t // tm),
        grid_spec=grid_spec,
        out_shape=jax.ShapeDtypeStruct((b, t, d), F32),
        compiler_params=_params("parallel", "arbitrary"),
        name="combine",
    )(lo2, ye, slot_c, aff_c, x1, gate2, final_g)


def _layer(x, c, ctx, c_ctx, w_mod, b_mod, norm1_g, norm2_g, w_in, conv_w, conv_b, b_if,
           pool_mix, pool_scale, mlstm_norm_g, w_pool_out, w_mlstm_out, w_out,
           w_router, w_gate, w_up, w_down):
    b, t, d = x.shape
    pw = d // 2
    ng = N_DIRS * 2 * N_HEADS
    q_off, k_off, v_off, o_off = pw, pw + d, pw + 2 * d, pw + 3 * d
    if_off, gate_off = pw + 4 * d, pw + 4 * d + ng
    cap = EC_CAPACITY * t // N_EXPERTS
    row = lambda a: a.reshape(1, -1)

    rows = -(-(b + 1) // 8) * 8
    cvec = jnp.zeros((rows, d), F32).at[:b].set(c).at[b].set(c_ctx)
    mod = _mod_call(cvec, w_mod, row(b_mod))
    shift1, scale1, gate1, shift2, scale2, gate2 = [
        mod[:b, j * d:(j + 1) * d].reshape(b, 1, d) for j in range(6)]
    shift_c, scale_c = mod[b:b + 1, 0:d], mod[b:b + 1, d:2 * d]

    w_in_b = w_in.astype(BF16)
    nq = N_DIRS * N_HEADS
    w_if3 = w_in_b[:, if_off:gate_off].reshape(d, N_DIRS, 2, N_HEADS)
    b_if3 = b_if.reshape(N_DIRS, 2, N_HEADS)
    w_i, w_f = w_if3[:, :, 0, :].reshape(d, nq), w_if3[:, :, 1, :].reshape(d, nq)
    b_i, b_f = b_if3[:, 0, :].reshape(nq), b_if3[:, 1, :].reshape(nq)
    pad_w = lambda w: jnp.zeros((d, LANES), BF16).at[:, :nq].set(w)
    pad_b = lambda v: jnp.zeros((1, LANES), F32).at[0, :nq].set(v)

    w_vt = w_in_b[:, v_off:o_off].T
    c0, m0 = _ctx_call(ctx, shift_c, scale_c, row(norm1_g),
                       w_in_b[:, k_off:v_off], w_vt, pad_w(w_i), pad_w(w_f),
                       conv_w[:, d:], row(conv_b[d:]), pad_b(b_i), pad_b(b_f))

    u, q, k, vt, og, gg, cq, rows = _proj_call(
        x, shift1, scale1, row(norm1_g),
        w_in_b[:, 0:q_off], w_in_b[:, q_off:v_off], w_vt, w_in_b[:, o_off:if_off],
        w_in_b[:, gate_off:], pad_w(w_i), pad_w(w_f), w_i.T, w_f.T,
        conv_w, row(conv_b), pad_b(b_i), pad_b(b_f), b_i.reshape(nq, 1), b_f.reshape(nq, 1))

    p = _pool_call(u, pool_mix.astype(BF16), row(pool_scale))
    m = _mlstm_call(q, k, vt, og, cq, rows, c0, m0, jnp.broadcast_to(mlstm_norm_g[:, None], (d, LANES)))

    x1, h2, aff_t = _merge_call(p, m, gg, x, gate1, shift2, scale2, row(norm2_g),
                                w_pool_out.astype(BF16), w_mlstm_out.astype(BF16), w_out.astype(BF16),
                                w_router.T)
    slot_t, lo = _route_call(aff_t, cap)
    lo2 = lo[:, :, :t // TOKEN_TILE + 1].reshape(b, -1)
    xe = _gather_call(lo2, h2, slot_t, cap)
    ye = _expert_call(xe, w_gate, w_up, w_down)
    return lo2, ye, jnp.swapaxes(slot_t, 1, 2), jnp.swapaxes(aff_t, 1, 2), x1, gate2


def kernel(x, c, ctx, c_ctx, w_mod, b_mod, norm1_g, norm2_g, w_in, conv_w, conv_b, b_if, pool_mix, pool_scale,
           mlstm_norm_g, w_pool_out, w_mlstm_out, w_out, w_router, w_gate, w_up, w_down, final_g):
    depth = w_mod.shape[0]
    for l in range(depth):
        lo2, ye, slot_c, aff_c, x1, gate2 = _layer(
            x, c, ctx, c_ctx, w_mod[l], b_mod[l], norm1_g[l], norm2_g[l], w_in[l], conv_w[l], conv_b[l],
            b_if[l], pool_mix[l], pool_scale[l], mlstm_norm_g[l], w_pool_out[l], w_mlstm_out[l], w_out[l],
            w_router[l], w_gate[l], w_up[l], w_down[l])
        x = _combine_call(lo2, ye, slot_c, aff_c, x1, gate2, final_g.reshape(1, -1), final_norm=l == depth - 1)
    return x
```

```python
import functools

import jax
import jax.numpy as jnp
import numpy as np
from jax import lax
from jax.experimental import pallas as pl
from jax.experimental.pallas import tpu as pltpu

F32 = jnp.float32
BF16 = jnp.bfloat16

GRID_W = 64
POOL_WINDOWS = (2, 4, 8, 16)
N_HEADS = 4
CONV_W = 5
N_DIRS = 2
N_EXPERTS = 16
EC_CAPACITY = 2
NORM_EPS = 1e-6
LOG2E = 1.4426950408889634

CHUNK = 256
TOKEN_TILE = 512
MERGE_TILE = 1024
HALO = 16
LANES = 128
AUG_ROWS = 16
V7X_VMEM_LIMIT_BYTES = 56 * 1024 * 1024

NN = (((1,), (0,)), ((), ()))
NT = (((1,), (1,)), ((), ()))
TN = (((0,), (0,)), ((), ()))


def _dot(a, b, dims=NN):
    return lax.dot_general(a, b, dims, preferred_element_type=F32)


def _split2(a):
    hi = a.astype(BF16)
    lo = (a - hi.astype(F32)).astype(BF16)
    return hi, lo


def _split3(a):
    a1 = a.astype(BF16)
    r1 = a - a1.astype(F32)
    a2 = r1.astype(BF16)
    a3 = (r1 - a2.astype(F32)).astype(BF16)
    return a1, a2, a3


def _dot3(a, b, dims=NN):
    ah, al = _split2(a)
    bh, bl = _split2(b)
    return _dot(ah, bh, dims) + _dot(ah, bl, dims) + _dot(al, bh, dims)


def _dot_left01(t01, a):
    a1, a2, a3 = _split3(a)
    return _dot(t01, a1) + _dot(t01, a2) + _dot(t01, a3)


def _dot_right01(a, t01):
    a1, a2, a3 = _split3(a)
    return _dot(a1, t01) + _dot(a2, t01) + _dot(a3, t01)


def _silu(x):
    return x * jax.nn.sigmoid(x)


def _log_sigmoid(x):
    return jnp.minimum(x, 0.0) - jnp.log1p(jnp.exp(-jnp.abs(x)))


def _rms_scale(x):
    return lax.rsqrt(jnp.mean(x * x, axis=-1, keepdims=True) + NORM_EPS)


def _tri01(n, kind):
    i = lax.broadcasted_iota(jnp.int32, (n, n), 0)
    j = lax.broadcasted_iota(jnp.int32, (n, n), 1)
    cond = {"le": j <= i, "ge": j >= i, "lt": j < i, "gt": j > i}[kind]
    return jnp.where(cond, 1.0, 0.0).astype(BF16)


def _shift(n):
    assert n & (n - 1) == 0, n
    return n.bit_length() - 1


def _div_pow2(x, n):
    return lax.shift_right_logical(x, _shift(n))


def _mod_pow2(x, n):
    return jnp.bitwise_and(x, n - 1)


def _params(*sem, flags=None):
    return pltpu.CompilerParams(dimension_semantics=sem, vmem_limit_bytes=V7X_VMEM_LIMIT_BYTES, flags=flags)


def _resident(shape):
    nd = len(shape)
    return pl.BlockSpec(shape, lambda *_: (0,) * nd)


def _mod_kernel(c_ref, w_ref, b_ref, o_ref):
    o_ref[...] = _dot3(_silu(c_ref[...]), w_ref[...]) + b_ref[...]


def _mod_call(cvec, w_mod, b_mod):
    rows, d = cvec.shape
    n = w_mod.shape[1]
    tn = 1536
    return pl.pallas_call(
        _mod_kernel,
        grid=(n // tn,),
        in_specs=[pl.BlockSpec((rows, d), lambda j: (0, 0)),
                  pl.BlockSpec((d, tn), lambda j: (0, j)),
                  pl.BlockSpec((1, tn), lambda j: (0, j))],
        out_specs=pl.BlockSpec((rows, tn), lambda j: (0, j)),
        out_shape=jax.ShapeDtypeStruct((rows, n), F32),
        compiler_params=_params("parallel"),
        name="mod",
    )(cvec, w_mod, b_mod)


def _ctx_kernel(ctx_ref, sh_ref, sc_ref, g_ref, wk_ref, wvt_ref, wi_ref, wf_ref, cw_ref, cb_ref, bi_ref, bf_ref,
                c_out, m_out):
    lc, d = ctx_ref.shape
    dh = d // N_HEADS
    x = ctx_ref[...]
    hc = (x * _rms_scale(x) * g_ref[...]) * (1.0 + sc_ref[...]) + sh_ref[...]
    hcb = hc.astype(BF16)

    kpre = _dot(hcb, wk_ref[...])
    pad = jnp.zeros((8, d), F32)
    kp = jnp.concatenate([pad, kpre, pad], axis=0)
    cw = cw_ref[...]
    acc = cb_ref[...] + cw[0:1, :] * kp[6:6 + lc, :]
    for j in range(1, CONV_W):
        acc = acc + cw[j:j + 1, :] * kp[6 + j:6 + j + lc, :]
    k = _silu(acc) * (dh ** -0.5)
    vt = _dot(wvt_ref[...], hcb, NT).astype(BF16)

    gi = _dot(hcb, wi_ref[...]) + bi_ref[...]
    lf = _log_sigmoid(_dot(hcb, wf_ref[...]) + bf_ref[...])
    lane = lax.broadcasted_iota(jnp.int32, lf.shape, 1)
    w_all = gi + jnp.where(lane < N_HEADS, _dot_left01(_tri01(lc, "gt"), lf), _dot_left01(_tri01(lc, "lt"), lf))
    ones = jnp.ones((AUG_ROWS, lc), BF16)
    for dr in range(N_DIRS):
        for h in range(N_HEADS):
            col = dr * N_HEADS + h
            w = w_all[:, col:col + 1]
            m = jnp.max(w, axis=0, keepdims=True)
            wk = jnp.exp(w - m) * k[:, h * dh:(h + 1) * dh]
            vt_aug = jnp.concatenate([vt[h * dh:(h + 1) * dh, :], ones], axis=0)
            c_out[dr, h] = _dot(vt_aug, wk.astype(BF16))
            m_out[dr, h] = jnp.broadcast_to(m * LOG2E, (1, LANES))


def _ctx_call(ctx, sh_c, sc_c, g1, wk, wv, wi, wf, cw_k, cb_k, bi, bf):
    b, lc, d = ctx.shape
    dh = d // N_HEADS
    row = lambda w: pl.BlockSpec((1, w), lambda i: (0, 0))
    return pl.pallas_call(
        _ctx_kernel,
        grid=(b,),
        in_specs=[pl.BlockSpec((None, lc, d), lambda i: (i, 0, 0)),
                  row(d), row(d), row(d),
                  pl.BlockSpec((d, d), lambda i: (0, 0)),
                  pl.BlockSpec((d, d), lambda i: (0, 0)),
                  pl.BlockSpec((d, LANES), lambda i: (0, 0)),
                  pl.BlockSpec((d, LANES), lambda i: (0, 0)),
                  pl.BlockSpec((CONV_W, d), lambda i: (0, 0)),
                  row(d), row(LANES), row(LANES)],
        out_specs=[pl.BlockSpec((None, N_DIRS, N_HEADS, dh + AUG_ROWS, dh), lambda i: (i, 0, 0, 0, 0)),
                   pl.BlockSpec((None, N_DIRS, N_HEADS, 1, LANES), lambda i: (i, 0, 0, 0, 0))],
        out_shape=[jax.ShapeDtypeStruct((b, N_DIRS, N_HEADS, dh + AUG_ROWS, dh), F32),
                   jax.ShapeDtypeStruct((b, N_DIRS, N_HEADS, 1, LANES), F32)],
        compiler_params=_params("parallel"),
        name="ctx_states",
    )(ctx, sh_c, sc_c, g1, wk, wv, wi, wf, cw_k, cb_k, bi, bf)


def _cummax_lanes(x, reverse):
    n = x.shape[-1]
    lane = lax.broadcasted_iota(jnp.int32, x.shape, x.ndim - 1)
    k = 1
    while k < n:
        if reverse:
            shifted = jnp.where(lane < n - k, pltpu.roll(x, n - k, axis=x.ndim - 1), -jnp.inf)
        else:
            shifted = jnp.where(lane >= k, pltpu.roll(x, k, axis=x.ndim - 1), -jnp.inf)
        x = jnp.maximum(x, shifted)
        k *= 2
    return x


def _proj_kernel(xp_ref, x_ref, xn_ref, sh_ref, sc_ref, g_ref,
                 wpool_ref, wqk_ref, wvt_ref, wo_ref, wg_ref, wit_ref, wft_ref,
                 cw_ref, cb_ref, bit_ref, bft_ref,
                 u_out, q_out, k_out, vt_out, og_out, gg_out, cq_out, rows_out,
                 hx_scr, r_scr):
    tm, d = x_ref.shape
    dh = d // N_HEADS
    i = pl.program_id(1)
    last = pl.num_programs(1) - 1

    x_ext = jnp.concatenate([xp_ref[...], x_ref[...], xn_ref[...]], axis=0)
    hx = (x_ext * _rms_scale(x_ext) * g_ref[...]) * (1.0 + sc_ref[...]) + sh_ref[...]
    n_ext = tm + 2 * HALO
    r_id = lax.broadcasted_iota(jnp.int32, (n_ext, 1), 0)
    valid = jnp.logical_and(jnp.logical_or(i > 0, r_id >= HALO),
                            jnp.logical_or(i < last, r_id < HALO + tm))
    hx_scr[...] = jnp.where(valid, hx, 0.0).astype(BF16)
    hxc = hx_scr[HALO:HALO + tm, :]

    nc = 512
    half = CONV_W // 2

    def qk_dot(c):
        r_scr[c % 2] = _dot(hx_scr[...], wqk_ref[:, c * nc:(c + 1) * nc])

    def qk_conv(c):
        cols = slice(c * nc, (c + 1) * nc)
        r = r_scr.at[c % 2]
        cw = cw_ref[:, cols]
        acc = cb_ref[:, cols] + cw[0:1, :] * r[HALO - half:HALO - half + tm, :]
        for j in range(1, CONV_W):
            acc = acc + cw[j:j + 1, :] * r[HALO - half + j:HALO - half + j + tm, :]
        y = _silu(acc)
        if c * nc < d:
            q_out[:, cols] = y.astype(BF16)
        else:
            k_out[:, c * nc - d:(c + 1) * nc - d] = (y * (dh ** -0.5)).astype(BF16)

    def v_chunk(c):
        cols = slice(c * nc, (c + 1) * nc)
        vt_out[cols, :] = _dot(wvt_ref[cols, :], hxc, NT).astype(BF16)

    def o_chunk(c):
        cols = slice(c * nc, (c + 1) * nc)
        og_out[:, cols] = jax.nn.sigmoid(_dot(hxc, wo_ref[:, cols])).astype(BF16)

    def g_chunk(c):
        cols = slice(c * nc, (c + 1) * nc)
        gg_out[:, cols] = jax.nn.sigmoid(_dot(hxc, wg_ref[:, cols])).astype(BF16)

    qk_dot(0); qk_dot(1)
    qk_conv(0); v_chunk(0); v_chunk(1); qk_dot(2)
    qk_conv(1); o_chunk(0); o_chunk(1); qk_dot(3)
    qk_conv(2); g_chunk(0); g_chunk(1)
    qk_conv(3); g_chunk(2); g_chunk(3)
    u_out[...] = _dot(hxc, wpool_ref[...]).astype(BF16)

    nq = N_DIRS * N_HEADS
    gi_r = _dot(wit_ref[...], hxc, NT) + bit_ref[...]
    lf_r = _log_sigmoid(_dot(wft_ref[...], hxc, NT) + bft_ref[...])
    fwd_sub = lax.broadcasted_iota(jnp.int32, (nq, CHUNK), 0) < N_HEADS
    t_le, t_ge = _tri01(CHUNK, "le"), _tri01(CHUNK, "ge")
    for j in range(tm // CHUNK):
        rows = slice(j * CHUNK, (j + 1) * CHUNK)
        b_r = jnp.where(fwd_sub, _dot_right01(lf_r[:, rows], t_ge), _dot_right01(lf_r[:, rows], t_le))
        c_r = (gi_r[:, rows] - b_r) * LOG2E
        b_r = b_r * LOG2E
        cq_out[rows, :] = c_r.T
        rows_out[0:nq, rows] = b_r
        rows_out[nq:2 * nq, rows] = jnp.where(fwd_sub, _cummax_lanes(c_r, False), _cummax_lanes(c_r, True))


def _proj_call(x, sh, sc, g1, wpool, wqk, wvt, wo, wg, wit, wft, cw, cb, bit, bft):
    b, t, d = x.shape
    tm = TOKEN_TILE
    nt = t // tm
    hb = tm // HALO
    nq = N_DIRS * N_HEADS
    per_b = pl.BlockSpec((None, 1, d), lambda bi_, i: (bi_, 0, 0))
    tile = lambda w: pl.BlockSpec((None, tm, w), lambda bi_, i: (bi_, i, 0))
    tile_t = lambda h: pl.BlockSpec((None, h, tm), lambda bi_, i: (bi_, 0, i))
    const = lambda a: pl.BlockSpec(a.shape, lambda bi_, i: (0,) * a.ndim)
    out_shapes = [jax.ShapeDtypeStruct((b, t, d // 2), BF16),
                  jax.ShapeDtypeStruct((b, t, d), BF16),
                  jax.ShapeDtypeStruct((b, t, d), BF16),
                  jax.ShapeDtypeStruct((b, d, t), BF16),
                  jax.ShapeDtypeStruct((b, t, d), BF16),
                  jax.ShapeDtypeStruct((b, t, 2 * d), BF16),
                  jax.ShapeDtypeStruct((b, t, nq), F32),
                  jax.ShapeDtypeStruct((b, 2 * nq, t), F32)]
    out_specs = [tile(d // 2), tile(d), tile(d), tile_t(d), tile(d), tile(2 * d),
                 tile(nq), tile_t(2 * nq)]
    return pl.pallas_call(
        _proj_kernel,
        grid=(b, nt),
        in_specs=[pl.BlockSpec((None, HALO, d), lambda bi_, i: (bi_, jnp.maximum(i * hb - 1, 0), 0)),
                  tile(d),
                  pl.BlockSpec((None, HALO, d), lambda bi_, i: (bi_, jnp.minimum((i + 1) * hb, t // HALO - 1), 0)),
                  per_b, per_b, const(g1),
                  const(wpool), const(wqk), const(wvt), const(wo), const(wg),
                  const(wit), const(wft),
                  const(cw), const(cb), const(bit), const(bft)],
        out_specs=out_specs,
        out_shape=out_shapes,
        scratch_shapes=[pltpu.VMEM((tm + 2 * HALO, d), BF16),
                        pltpu.VMEM((2, tm + 2 * HALO, 512), F32)],
        compiler_params=_params("parallel", "parallel"),
        name="proj",
    )(x, x, x, sh, sc, g1, wpool, wqk, wvt, wo, wg, wit, wft, cw, cb, bit, bft)


def _pool_kernel(u_ref, mix_ref, scale_ref, inv_ref, p_out, pad_scr):
    t, pw = u_ref.shape
    gw = pw // len(POOL_WINDOWS)
    tile = 256
    maxlo = max(POOL_WINDOWS) // 2
    padr = maxlo * GRID_W
    ti = lax.broadcasted_iota(jnp.int32, (tile, tile), 0)
    tj = lax.broadcasted_iota(jnp.int32, (tile, tile), 1)
    same_row = _div_pow2(ti, GRID_W) == _div_pow2(tj, GRID_W)
    ci, cj = _mod_pow2(ti, GRID_W), _mod_pow2(tj, GRID_W)

    def span(dlt, ext):
        return pad_scr[padr + (dlt - ext) * GRID_W:padr + (dlt + ext) * GRID_W + t, :]

    for g, side in enumerate(POOL_WINDOWS):
        lo, hi = side // 2, side - side // 2
        assert lo == hi and side & (side - 1) == 0
        cols = slice(g * gw, (g + 1) * gw)
        pad_scr[0:padr, :] = jnp.zeros((padr, gw), F32)
        pad_scr[padr + t:padr + t + padr, :] = jnp.zeros((padr, gw), F32)
        band = jnp.logical_and(same_row, jnp.logical_and(cj >= ci - lo, cj < ci + hi))
        pw01 = jnp.where(band, 1.0, 0.0).astype(BF16)
        for k in range(t // tile):
            rs = slice(k * tile, (k + 1) * tile)
            pad_scr[padr + k * tile:padr + (k + 1) * tile, :] = _dot(pw01, u_ref[rs, cols])
        ext = (side - 2) // 2
        tot = span(-1, ext) + span(0, ext)
        k = 2
        while k < side:
            pad_scr[padr - ext * GRID_W:padr + ext * GRID_W + t, :] = tot
            ext = (side - 2 * k) // 2
            tot = span(-(k // 2), ext) + span(k // 2, ext)
            k *= 2
        a = tot * inv_ref[g] - u_ref[:, cols].astype(F32)
        p = _dot(a.astype(BF16), mix_ref[g]) * scale_ref[:, cols]
        p_out[:, cols] = p.astype(BF16)


def _pool_inv_counts(t, gw):
    rows = t // GRID_W
    r, c = np.arange(t) // GRID_W, np.arange(t) % GRID_W
    out = []
    for side in POOL_WINDOWS:
        lo, hi = side // 2, side - side // 2
        cnt = ((np.minimum(r + hi, rows) - np.maximum(r - lo, 0))
               * (np.minimum(c + hi, GRID_W) - np.maximum(c - lo, 0)))
        out.append(np.broadcast_to((1.0 / cnt).astype(np.float32)[:, None], (t, gw)))
    return jnp.asarray(np.stack(out))


def _pool_call(u, mix, scale):
    b, t, pw = u.shape
    ng = len(POOL_WINDOWS)
    gw = pw // ng
    padr = (max(POOL_WINDOWS) // 2) * GRID_W
    return pl.pallas_call(
        _pool_kernel,
        grid=(b,),
        in_specs=[pl.BlockSpec((None, t, pw), lambda i: (i, 0, 0)),
                  pl.BlockSpec(mix.shape, lambda i: (0, 0, 0)),
                  pl.BlockSpec((1, pw), lambda i: (0, 0)),
                  pl.BlockSpec((ng, t, gw), lambda i: (0, 0, 0))],
        out_specs=pl.BlockSpec((None, t, pw), lambda i: (i, 0, 0)),
        out_shape=jax.ShapeDtypeStruct((b, t, pw), BF16),
        scratch_shapes=[pltpu.VMEM((t + 2 * padr, gw), F32)],
        compiler_params=_params("parallel"),
        name="pool",
    )(u, mix, scale, _pool_inv_counts(t, gw))


def _mlstm_dir(q_ref, k_ref, vt_ref, cq_ref, rows_ref, c_scr, m_scr, reverse):
    L, d = q_ref.shape
    dh = d // N_HEADS
    nhalf = dh // LANES
    nq = N_DIRS * N_HEADS
    si = lax.broadcasted_iota(jnp.int32, (L, L), 0)
    tj = lax.broadcasted_iota(jnp.int32, (L, L), 1)
    mask = (si >= tj) if reverse else (si <= tj)
    ones = jnp.ones((AUG_ROWS, L), BF16)
    end = 0 if reverse else L - 1
    off = N_HEADS if reverse else 0

    hs_all = []
    for h in range(N_HEADS):
        hs = slice(h * dh, (h + 1) * dh)
        st = off + h
        q = q_ref[:, hs]
        k = k_ref[:, hs]
        vt_aug = jnp.concatenate([vt_ref[hs, :], ones], axis=0)
        c_c = cq_ref[:, st:st + 1]
        b_r = rows_ref[st:st + 1, :]
        cm_r = rows_ref[nq + st:nq + st + 1, :]
        m_prev = m_scr[st][:, 0:1]
        ct_prev = c_scr[st]

        mm = jnp.maximum(cm_r, m_prev)
        w_inter = jnp.exp2(m_prev - mm)
        st_mat = (_dot(k, q, NT) * jnp.exp2(jnp.where(mask, c_c - mm, -jnp.inf))).astype(BF16)
        intra = _dot(vt_aug, st_mat)
        inter = _dot(ct_prev.astype(BF16), q, NT)
        den = w_inter * inter[dh:dh + 1, :] + intra[dh:dh + 1, :]
        inv = 1.0 / jnp.maximum(jnp.abs(den), jnp.exp2(-(b_r + mm)))
        hs_all.append((w_inter * inter[0:dh, :] + intra[0:dh, :]) * inv)

        g_tot = b_r[:, end:end + 1]
        m_new = g_tot + jnp.maximum(m_prev, cm_r[:, end:end + 1])
        decay = jnp.exp2(g_tot + m_prev - m_new)
        wk = (k.astype(F32) * jnp.exp2(g_tot + c_c - m_new)).astype(BF16)
        c_scr[st] = decay * ct_prev + _dot(vt_aug, wk)
        m_scr[st] = jnp.broadcast_to(m_new, (1, LANES))
    return hs_all


def _mlstm_kernel(qf_ref, kf_ref, vf_ref, ogf_ref, cqf_ref, rwf_ref,
                  qb_ref, kb_ref, vb_ref, ogb_ref, cqb_ref, rwb_ref,
                  c0_ref, m0_ref, ng_ref, o_ref,
                  c_scr, m_scr, hf_scr, hb_scr):
    L, d = qf_ref.shape
    dh = d // N_HEADS
    nhalf = dh // LANES
    s = pl.program_id(1)
    nch = pl.num_programs(1)
    half = nch // 2

    @pl.when(s == 0)
    def _():
        for j in range(N_DIRS * N_HEADS):
            c_scr[j] = c0_ref[j // N_HEADS, j % N_HEADS]
            m_scr[j] = m0_ref[j // N_HEADS, j % N_HEADS]

    h_f = _mlstm_dir(qf_ref, kf_ref, vf_ref, cqf_ref, rwf_ref, c_scr, m_scr, False)
    h_b = _mlstm_dir(qb_ref, kb_ref, vb_ref, cqb_ref, rwb_ref, c_scr, m_scr, True)

    @pl.when(s < half)
    def _():
        for h in range(N_HEADS):
            hs = slice(h * dh, (h + 1) * dh)
            hf_scr[s, hs, :] = h_f[h]
            hb_scr[half - 1 - s, hs, :] = h_b[h]

    @pl.when(s >= half)
    def _():
        def finish(ht, h, og_ref, out):
            hs = slice(h * dh, (h + 1) * dh)
            scale = lax.rsqrt(jnp.mean(ht * ht, axis=0, keepdims=True) + NORM_EPS)
            y = jnp.concatenate([ht[:, i * LANES:(i + 1) * LANES] * scale[:, i * LANES:(i + 1) * LANES]
                                 * ng_ref[hs, :] for i in range(L // LANES)], axis=1).T
            out[:, hs] = (y * og_ref[:, hs].astype(F32)).astype(BF16)

        for h in range(N_HEADS):
            hs = slice(h * dh, (h + 1) * dh)
            finish(h_f[h] + hb_scr[s - half, hs, :], h, ogf_ref, o_ref.at[1])
            finish(h_b[h] + hf_scr[nch - 1 - s, hs, :], h, ogb_ref, o_ref.at[0])


def _mlstm_call(q, k, vt, og, cq, rows, c0, m0, norm_g):
    b, t, d = q.shape
    dh = d // N_HEADS
    L = CHUNK
    nch = t // L
    half = nch // 2
    assert nch % 2 == 0
    nq = N_DIRS * N_HEADS

    def specs(chunk):
        seq = lambda w: pl.BlockSpec((None, L, w), lambda bi, s: (bi, chunk(s), 0))
        seq_t = lambda h: pl.BlockSpec((None, h, L), lambda bi, s: (bi, 0, chunk(s)))
        return [seq(d), seq(d), seq_t(d), seq(d), seq(nq), seq_t(2 * nq)]

    state = lambda w0, w1: pl.BlockSpec((None, N_DIRS, N_HEADS, w0, w1), lambda bi, s: (bi, 0, 0, 0, 0))
    return pl.pallas_call(
        _mlstm_kernel,
        grid=(b, nch),
        in_specs=specs(lambda s: s) + specs(lambda s: nch - 1 - s) + [
            state(dh + AUG_ROWS, dh), state(1, LANES), pl.BlockSpec((d, LANES), lambda bi, s: (0, 0))],
        out_specs=pl.BlockSpec((None, 2, None, L, d), lambda bi, s: (bi, 0, jnp.maximum(s - half, 0), 0, 0)),
        out_shape=jax.ShapeDtypeStruct((b, 2, half, L, d), BF16),
        scratch_shapes=[pltpu.VMEM((N_DIRS * N_HEADS, dh + AUG_ROWS, dh), F32),
                        pltpu.VMEM((N_DIRS * N_HEADS, 1, LANES), F32),
                        pltpu.VMEM((half, d, L), F32),
                        pltpu.VMEM((half, d, L), F32)],
        compiler_params=_params("parallel", "arbitrary"),
        name="mlstm",
    )(q, k, vt, og, cq, rows, q, k, vt, og, cq, rows, c0, m0, norm_g)


def _merge_kernel(p_ref, m_ref, gg_ref, x_ref, g1_ref, sh2_ref, sc2_ref, n2_ref,
                  wpo_ref, wmo_ref, wout_ref, wr_ref, x1_out, h2_out, aff_out):
    tm, d = x_ref.shape
    cpt = m_ref.shape[0]
    L = m_ref.shape[1]
    upper = pl.program_id(1) >= pl.num_programs(1) // 2
    sub = TOKEN_TILE
    cps = sub // L
    streams = [slice(r * sub, (r + 1) * sub) for r in range(tm // sub)]

    def branches(r):
        m = jnp.where(upper,
                      jnp.concatenate([m_ref[r * cps + j] for j in range(cps)], axis=0),
                      jnp.concatenate([m_ref[cpt - 1 - r * cps - j] for j in range(cps)], axis=0))
        return _dot(p_ref[streams[r], :], wpo_ref[...]), _dot(m, wmo_ref[...])

    def mix(r, a, mm):
        rows = streams[r]
        return (gg_ref[rows, 0:d].astype(F32) * a + gg_ref[rows, d:2 * d].astype(F32) * mm).astype(BF16)

    def residual(r, mixed):
        x1 = x_ref[streams[r], :] + g1_ref[...] * _dot(mixed, wout_ref[...])
        x1_out[streams[r], :] = x1
        return x1

    def tail(r, x1):
        rows = streams[r]
        h2 = (x1 * _rms_scale(x1) * n2_ref[...]) * (1.0 + sc2_ref[...]) + sh2_ref[...]
        h2_out[rows, :] = h2.astype(BF16)
        logits = _dot3(wr_ref[...], h2, NT)
        z = jnp.exp(logits - jnp.max(logits, axis=0, keepdims=True))
        aff_out[:, rows] = z / jnp.sum(z, axis=0, keepdims=True)

    n = len(streams)
    ab = [branches(r) for r in range(n)]
    mixed = [mix(r, *ab[r]) for r in range(n)]
    x1s = [residual(r, mixed[r]) for r in range(n)]
    for r in range(n):
        tail(r, x1s[r])


def _merge_call(p, m, gg, x, gate1, sh2, sc2, n2, wpo, wmo, wout, wr_t):
    b, t, d = x.shape
    tm = min(MERGE_TILE, t // 2)
    e = wr_t.shape[0]
    per_b = pl.BlockSpec((None, 1, d), lambda bi, i: (bi, 0, 0))
    tile = lambda w: pl.BlockSpec((None, tm, w), lambda bi, i: (bi, i, 0))
    const = lambda a: pl.BlockSpec(a.shape, lambda bi, i: (0,) * a.ndim)
    cpt = tm // CHUNK
    nth = t // tm // 2
    m_spec = pl.BlockSpec((None, None, cpt, CHUNK, d),
                          lambda bi, i: (bi, i // nth, jnp.where(i >= nth, i - nth, nth - 1 - i), 0, 0))
    return pl.pallas_call(
        _merge_kernel,
        grid=(b, t // tm),
        in_specs=[tile(d // 2), m_spec, tile(2 * d), tile(d), per_b, per_b, per_b, const(n2),
                  const(wpo), const(wmo), const(wout), const(wr_t)],
        out_specs=[tile(d), tile(d), pl.BlockSpec((None, e, tm), lambda bi, i: (bi, 0, i))],
        out_shape=[jax.ShapeDtypeStruct((b, t, d), F32),
                   jax.ShapeDtypeStruct((b, t, d), BF16),
                   jax.ShapeDtypeStruct((b, e, t), F32)],
        compiler_params=_params("parallel", "parallel"),
        name="merge",
    )(p, m, gg, x, gate1, sh2, sc2, n2, wpo, wmo, wout, wr_t)


def _route_kernel(aff_ref, slot_out, lo_out, *, cap):
    e, t = aff_ref.shape
    aff = aff_ref[...]

    def step(i, thr):
        cand = thr | (jnp.int32(1) << (30 - i))
        cnt = jnp.sum(jnp.where(aff >= pltpu.bitcast(cand, F32), 1.0, 0.0), axis=-1, keepdims=True)
        return jnp.where(cnt >= cap, cand, thr)

    thr = pltpu.bitcast(lax.fori_loop(0, 31, step, jnp.zeros((e, 1), jnp.int32)), F32)
    gt = aff > thr
    eq = aff == thr
    need = cap - jnp.sum(jnp.where(gt, 1.0, 0.0), axis=-1, keepdims=True).astype(jnp.int32)

    seg = 256
    t_ge = _tri01(seg, "ge")

    def prefix_incl(x01):
        outs, carries, carry = [], [], jnp.zeros((e, 1), F32)
        for j in range(t // seg):
            p = _dot(x01[:, j * seg:(j + 1) * seg].astype(BF16), t_ge) + carry
            outs.append(p)
            carry = p[:, seg - 1:seg]
            carries.append(carry)
        return jnp.concatenate(outs, axis=1), carries

    eq_f = jnp.where(eq, 1.0, 0.0)
    tie_rank = (prefix_incl(eq_f)[0] - eq_f).astype(jnp.int32)
    sel = jnp.logical_or(gt, jnp.logical_and(eq, tie_rank < need))
    rank, carries = prefix_incl(jnp.where(sel, 1.0, 0.0))
    slot_out[...] = jnp.where(sel, rank.astype(jnp.int32) - 1, -1)

    lane = lax.broadcasted_iota(jnp.int32, (e, LANES), 1)
    lo = jnp.zeros((e, LANES), F32)
    per_tile = TOKEN_TILE // seg
    for c in range(1, t // TOKEN_TILE + 1):
        lo = jnp.where(lane == c, carries[c * per_tile - 1], lo)
    lo_out[...] = lo.astype(jnp.int32)


def _route_call(aff_t, cap):
    b, e, t = aff_t.shape
    n = b * e
    slot, lo = pl.pallas_call(
        functools.partial(_route_kernel, cap=cap),
        grid=(1,),
        in_specs=[pl.BlockSpec((n, t), lambda i: (0, 0))],
        out_specs=[pl.BlockSpec((n, t), lambda i: (0, 0)),
                   pl.BlockSpec((n, LANES), lambda i: (0, 0))],
        out_shape=[jax.ShapeDtypeStruct((n, t), jnp.int32),
                   jax.ShapeDtypeStruct((n, LANES), jnp.int32)],
        compiler_params=_params("arbitrary"),
        name="route",
    )(aff_t.reshape(n, t))
    return slot.reshape(b, e, t), lo.reshape(b, e, LANES)


SLOT_WINDOW = 128
SLOT_ALIGN = 16
EXPERT_GROUP = 4


def _aligned(lo):
    return jnp.bitwise_and(lo, -SLOT_ALIGN)


def _window_start(nominal, cap):
    return pl.multiple_of(jnp.minimum(nominal, cap - SLOT_WINDOW), SLOT_ALIGN)


def _n_windows(lo, hi):
    return lax.shift_right_logical(hi - _aligned(lo) + (SLOT_WINDOW - 1), _shift(SLOT_WINDOW))


def _gather_kernel(lo_ref, h2_ref, slot_ref, xe_out, *, cap, n_tiles):
    n_exp = slot_ref.shape[0]
    tc, w = TOKEN_TILE, SLOT_WINDOW
    b = pl.program_id(0)
    stride = n_tiles + 1
    xe_out[...] = jnp.zeros(xe_out.shape, BF16)
    s_id = lax.broadcasted_iota(jnp.int32, (w, tc), 0)

    def add_rows(e, start, z):
        xe_out[e, pl.ds(start, w), :] = xe_out[e, pl.ds(start, w), :] + z.astype(BF16)

    def tile_body(c, carry):
        t0 = pl.multiple_of(c * tc, tc)
        for g0 in range(0, n_exp, EXPERT_GROUP):
            starts, blocks = [], []
            for e in range(g0, g0 + EXPERT_GROUP):
                a0 = _window_start(_aligned(lo_ref[b, e * stride + c]), cap)
                hit = (s_id + a0) == slot_ref[e:e + 1, pl.ds(t0, tc)]
                blocks.append(jnp.where(hit, 1.0, 0.0).astype(BF16))
                starts.append(a0)
            z = _dot(jnp.concatenate(blocks, axis=0), h2_ref[pl.ds(t0, tc), :])
            for j in range(EXPERT_GROUP):
                add_rows(g0 + j, starts[j], z[j * w:(j + 1) * w, :])
        for e in range(n_exp):
            lo, hi = lo_ref[b, e * stride + c], lo_ref[b, e * stride + c + 1]

            def window_body(k, carry2, e=e, lo=lo):
                nominal = _aligned(lo) + k * w
                a = _window_start(nominal, cap)
                srow = slot_ref[e:e + 1, pl.ds(t0, tc)]
                hit = jnp.logical_and((s_id + a) == srow, srow >= nominal)
                add_rows(e, a, _dot(jnp.where(hit, 1.0, 0.0).astype(BF16), h2_ref[pl.ds(t0, tc), :]))
                return carry2

            lax.fori_loop(1, _n_windows(lo, hi), window_body, 0)
        return carry

    lax.fori_loop(0, n_tiles, tile_body, 0)


def _gather_call(lo2, h2, slot_t, cap):
    b, t, d = h2.shape
    e = slot_t.shape[1]
    n_tiles = t // TOKEN_TILE
    grid_spec = pltpu.PrefetchScalarGridSpec(
        num_scalar_prefetch=1,
        grid=(b,),
        in_specs=[pl.BlockSpec((None, t, d), lambda i, lo: (i, 0, 0)),
                  pl.BlockSpec((None, e, t), lambda i, lo: (i, 0, 0))],
        out_specs=pl.BlockSpec((None, e, cap, d), lambda i, lo: (i, 0, 0, 0)),
    )
    return pl.pallas_call(
        functools.partial(_gather_kernel, cap=cap, n_tiles=n_tiles),
        grid_spec=grid_spec,
        out_shape=jax.ShapeDtypeStruct((b, e, cap, d), BF16),
        compiler_params=_params("arbitrary"),
        name="gather",
    )(lo2, h2, slot_t)


def _expert_kernel(xe_ref, wg_ref, wu_ref, wd_ref, ye_out, wg_scr, wu_scr, wd_scr):
    e, b = pl.program_id(0), pl.program_id(1)
    n_exp = pl.num_programs(0) - 1
    slab = wg_ref.shape[0]
    f = wg_scr.shape[2]

    @pl.when(e < n_exp)
    def _():
        slot = lax.rem(e, 2)
        rows = pl.ds(pl.multiple_of(b * slab, slab), slab)
        wg_scr[slot, rows, :] = wg_ref[...].astype(BF16)
        wu_scr[slot, rows, :] = wu_ref[...].astype(BF16)
        wd_scr[slot, rows, :] = wd_ref[...].astype(BF16)

    @pl.when(e > 0)
    def _():
        slot = lax.rem(e + 1, 2)
        xe = xe_ref[...]
        fc = 512
        y = None
        for c in range(f // fc):
            cols = slice(c * fc, (c + 1) * fc)
            hid = _silu(_dot(xe, wg_scr[slot, :, cols])) * _dot(xe, wu_scr[slot, :, cols])
            part = _dot(hid.astype(BF16), wd_scr[slot, cols, :])
            y = part if y is None else y + part
        ye_out[...] = y.astype(BF16)


def _expert_call(xe, wg, wu, wd):
    b, e, cap, d = xe.shape
    f = wg.shape[2]
    assert d % b == 0 and f % b == 0
    w_spec = lambda rows, cols: pl.BlockSpec((None, rows // b, cols),
                                             lambda ei, bi: (jnp.minimum(ei, e - 1), bi, 0))
    return pl.pallas_call(
        _expert_kernel,
        grid=(e + 1, b),
        in_specs=[pl.BlockSpec((None, None, cap, d), lambda ei, bi: (bi, jnp.maximum(ei - 1, 0), 0, 0)),
                  w_spec(d, f), w_spec(d, f), w_spec(f, d)],
        out_specs=pl.BlockSpec((None, None, cap, d),
                               lambda ei, bi: (jnp.where(ei == 0, 0, bi), jnp.maximum(ei - 1, 0), 0, 0)),
        out_shape=jax.ShapeDtypeStruct((b, e, cap, d), BF16),
        scratch_shapes=[pltpu.VMEM((2, d, f), BF16), pltpu.VMEM((2, d, f), BF16), pltpu.VMEM((2, f, d), BF16)],
        compiler_params=_params("arbitrary", "arbitrary"),
        name="experts",
    )(xe, wg, wu, wd)


def _combine_kernel(lo_ref, ye_ref, slot_ref, aff_ref, x1_ref, g2_ref, fg_ref, o_ref, acc_scr,
                    *, final_norm, n_tiles):
    n_exp, cap, d = ye_ref.shape
    tm, w = x1_ref.shape[0], SLOT_WINDOW
    b, i = pl.program_id(0), pl.program_id(1)
    stride = n_tiles + 1
    lane = lax.broadcasted_iota(jnp.int32, (tm, w), 1)

    for g0 in range(0, n_exp, EXPERT_GROUP):
        pts, rows = [], []
        for e in range(g0, g0 + EXPERT_GROUP):
            a0 = _window_start(_aligned(lo_ref[b, e * stride + i]), cap)
            hit = (lane + a0) == slot_ref[:, e:e + 1]
            pts.append(jnp.where(hit, aff_ref[:, e:e + 1], 0.0).astype(BF16))
            rows.append(ye_ref[e, pl.ds(a0, w), :])
        part = _dot(jnp.concatenate(pts, axis=1), jnp.concatenate(rows, axis=0))
        if g0 == 0:
            acc_scr[...] = part
        else:
            acc_scr[...] += part

    for e in range(n_exp):
        lo, hi = lo_ref[b, e * stride + i], lo_ref[b, e * stride + i + 1]

        def window_body(k, carry, e=e, lo=lo):
            nominal = _aligned(lo) + k * w
            a = _window_start(nominal, cap)
            sc = slot_ref[:, e:e + 1]
            hit = jnp.logical_and((lane + a) == sc, sc >= nominal)
            pt = jnp.where(hit, aff_ref[:, e:e + 1], 0.0).astype(BF16)
            acc_scr[...] += _dot(pt, ye_ref[e, pl.ds(a, w), :])
            return carry

        lax.fori_loop(1, _n_windows(lo, hi), window_body, 0)

    x2 = x1_ref[...] + g2_ref[...] * acc_scr[...]
    o_ref[...] = x2 * _rms_scale(x2) * fg_ref[...] if final_norm else x2


def _combine_call(lo2, ye, slot_c, aff_c, x1, gate2, final_g, final_norm):
    b, t, d = x1.shape
    e, cap = ye.shape[1], ye.shape[2]
    tm = TOKEN_TILE
    grid_spec = pltpu.PrefetchScalarGridSpec(
        num_scalar_prefetch=1,
        grid=(b, t // tm),
        in_specs=[pl.BlockSpec((None, e, cap, d), lambda bi, i, lo: (bi, 0, 0, 0)),
                  pl.BlockSpec((None, tm, e), lambda bi, i, lo: (bi, i, 0)),
                  pl.BlockSpec((None, tm, e), lambda bi, i, lo: (bi, i, 0)),
                  pl.BlockSpec((None, tm, d), lambda bi, i, lo: (bi, i, 0)),
                  pl.BlockSpec((None, 1, d), lambda bi, i, lo: (bi, 0, 0)),
                  pl.BlockSpec((1, d), lambda bi, i, lo: (0, 0))],
        out_specs=pl.BlockSpec((None, tm, d), lambda bi, i, lo: (bi, i, 0)),
        scratch_shapes=[pltpu.VMEM((tm, d), F32)],
    )
    return pl.pallas_call(
        functools.partial(_combine_kernel, final_norm=final_norm, n_tiles=t // tm),
        grid_spec=grid_spec,
        out_shape=jax.ShapeDtypeStruct((b, t, d), F32),
        compiler_params=_params("parallel", "arbitrary"),
        name="combine",
    )(lo2, ye, slot_c, aff_c, x1, gate2, final_g)


def _layer(x, c, ctx, c_ctx, w_mod, b_mod, norm1_g, norm2_g, w_in, conv_w, conv_b, b_if,
           pool_mix, pool_scale, mlstm_norm_g, w_pool_out, w_mlstm_out, w_out,
           w_router, w_gate, w_up, w_down):
    b, t, d = x.shape
    pw = d // 2
    ng = N_DIRS * 2 * N_HEADS
    q_off, k_off, v_off, o_off = pw, pw + d, pw + 2 * d, pw + 3 * d
    if_off, gate_off = pw + 4 * d, pw + 4 * d + ng
    cap = EC_CAPACITY * t // N_EXPERTS
    row = lambda a: a.reshape(1, -1)

    rows = -(-(b + 1) // 8) * 8
    cvec = jnp.zeros((rows, d), F32).at[:b].set(c).at[b].set(c_ctx)
    mod = _mod_call(cvec, w_mod, row(b_mod))
    shift1, scale1, gate1, shift2, scale2, gate2 = [
        mod[:b, j * d:(j + 1) * d].reshape(b, 1, d) for j in range(6)]
    shift_c, scale_c = mod[b:b + 1, 0:d], mod[b:b + 1, d:2 * d]

    w_in_b = w_in.astype(BF16)
    nq = N_DIRS * N_HEADS
    w_if3 = w_in_b[:, if_off:gate_off].reshape(d, N_DIRS, 2, N_HEADS)
    b_if3 = b_if.reshape(N_DIRS, 2, N_HEADS)
    w_i, w_f = w_if3[:, :, 0, :].reshape(d, nq), w_if3[:, :, 1, :].reshape(d, nq)
    b_i, b_f = b_if3[:, 0, :].reshape(nq), b_if3[:, 1, :].reshape(nq)
    pad_w = lambda w: jnp.zeros((d, LANES), BF16).at[:, :nq].set(w)
    pad_b = lambda v: jnp.zeros((1, LANES), F32).at[0, :nq].set(v)

    w_vt = w_in_b[:, v_off:o_off].T
    c0, m0 = _ctx_call(ctx, shift_c, scale_c, row(norm1_g),
                       w_in_b[:, k_off:v_off], w_vt, pad_w(w_i), pad_w(w_f),
                       conv_w[:, d:], row(conv_b[d:]), pad_b(b_i), pad_b(b_f))

    u, q, k, vt, og, gg, cq, rows = _proj_call(
        x, shift1, scale1, row(norm1_g),
        w_in_b[:, 0:q_off], w_in_b[:, q_off:v_off], w_vt, w_in_b[:, o_off:if_off],
        w_in_b[:, gate_off:], w_i.T, w_f.T,
        conv_w, row(conv_b), b_i.reshape(nq, 1), b_f.reshape(nq, 1))

    p = _pool_call(u, pool_mix.astype(BF16), row(pool_scale))
    m = _mlstm_call(q, k, vt, og, cq, rows, c0, m0, jnp.broadcast_to(mlstm_norm_g[:, None], (d, LANES)))

    x1, h2, aff_t = _merge_call(p, m, gg, x, gate1, shift2, scale2, row(norm2_g),
                                w_pool_out.astype(BF16), w_mlstm_out.astype(BF16), w_out.astype(BF16),
                                w_router.T)
    slot_t, lo = _route_call(aff_t, cap)
    lo2 = lo[:, :, :t // TOKEN_TILE + 1].reshape(b, -1)
    xe = _gather_call(lo2, h2, slot_t, cap)
    ye = _expert_call(xe, w_gate, w_up, w_down)
    return lo2, ye, jnp.swapaxes(slot_t, 1, 2), jnp.swapaxes(aff_t, 1, 2), x1, gate2


def kernel(x, c, ctx, c_ctx, w_mod, b_mod, norm1_g, norm2_g, w_in, conv_w, conv_b, b_if, pool_mix, pool_scale,
           mlstm_norm_g, w_pool_out, w_mlstm_out, w_out, w_router, w_gate, w_up, w_down, final_g):
    depth = w_mod.shape[0]
    for l in range(depth):
        lo2, ye, slot_c, aff_c, x1, gate2 = _layer(
            x, c, ctx, c_ctx, w_mod[l], b_mod[l], norm1_g[l], norm2_g[l], w_in[l], conv_w[l], conv_b[l],
            b_if[l], pool_mix[l], pool_scale[l], mlstm_norm_g[l], w_pool_out[l], w_mlstm_out[l], w_out[l],
            w_router[l], w_gate[l], w_up[l], w_down[l])
        x = _combine_call(lo2, ye, slot_c, aff_c, x1, gate2, final_g.reshape(1, -1), final_norm=l == depth - 1)
    return x
```

```python
import functools

import jax
import jax.numpy as jnp
import numpy as np
from jax import lax
from jax.experimental import pallas as pl
from jax.experimental.pallas import tpu as pltpu

F32 = jnp.float32
BF16 = jnp.bfloat16

GRID_W = 64
POOL_WINDOWS = (2, 4, 8, 16)
N_HEADS = 4
CONV_W = 5
N_DIRS = 2
N_EXPERTS = 16
EC_CAPACITY = 2
NORM_EPS = 1e-6
LOG2E = 1.4426950408889634

CHUNK = 256
TOKEN_TILE = 512
MERGE_TILE = 1024
HALO = 16
LANES = 128
AUG_ROWS = 16
V7X_VMEM_LIMIT_BYTES = 56 * 1024 * 1024

NN = (((1,), (0,)), ((), ()))
NT = (((1,), (1,)), ((), ()))
TN = (((0,), (0,)), ((), ()))


def _dot(a, b, dims=NN):
    return lax.dot_general(a, b, dims, preferred_element_type=F32)


def _split2(a):
    hi = a.astype(BF16)
    lo = (a - hi.astype(F32)).astype(BF16)
    return hi, lo


def _split3(a):
    a1 = a.astype(BF16)
    r1 = a - a1.astype(F32)
    a2 = r1.astype(BF16)
    a3 = (r1 - a2.astype(F32)).astype(BF16)
    return a1, a2, a3


def _dot3(a, b, dims=NN):
    ah, al = _split2(a)
    bh, bl = _split2(b)
    return _dot(ah, bh, dims) + _dot(ah, bl, dims) + _dot(al, bh, dims)


def _dot_left01(t01, a):
    a1, a2, a3 = _split3(a)
    return _dot(t01, a1) + _dot(t01, a2) + _dot(t01, a3)


def _dot_right01(a, t01):
    a1, a2, a3 = _split3(a)
    return _dot(a1, t01) + _dot(a2, t01) + _dot(a3, t01)


def _silu(x):
    return x * jax.nn.sigmoid(x)


def _log_sigmoid(x):
    return jnp.minimum(x, 0.0) - jnp.log1p(jnp.exp(-jnp.abs(x)))


def _rms_scale(x):
    return lax.rsqrt(jnp.mean(x * x, axis=-1, keepdims=True) + NORM_EPS)


def _tri01(n, kind):
    i = lax.broadcasted_iota(jnp.int32, (n, n), 0)
    j = lax.broadcasted_iota(jnp.int32, (n, n), 1)
    cond = {"le": j <= i, "ge": j >= i, "lt": j < i, "gt": j > i}[kind]
    return jnp.where(cond, 1.0, 0.0).astype(BF16)


def _shift(n):
    assert n & (n - 1) == 0, n
    return n.bit_length() - 1


def _div_pow2(x, n):
    return lax.shift_right_logical(x, _shift(n))


def _mod_pow2(x, n):
    return jnp.bitwise_and(x, n - 1)


def _params(*sem, flags=None):
    return pltpu.CompilerParams(dimension_semantics=sem, vmem_limit_bytes=V7X_VMEM_LIMIT_BYTES, flags=flags)


def _resident(shape):
    nd = len(shape)
    return pl.BlockSpec(shape, lambda *_: (0,) * nd)


def _mod_kernel(c_ref, w_ref, b_ref, o_ref):
    o_ref[...] = _dot3(_silu(c_ref[...]), w_ref[...]) + b_ref[...]


def _mod_call(cvec, w_mod, b_mod):
    rows, d = cvec.shape
    n = w_mod.shape[1]
    tn = 1536
    return pl.pallas_call(
        _mod_kernel,
        grid=(n // tn,),
        in_specs=[pl.BlockSpec((rows, d), lambda j: (0, 0)),
                  pl.BlockSpec((d, tn), lambda j: (0, j)),
                  pl.BlockSpec((1, tn), lambda j: (0, j))],
        out_specs=pl.BlockSpec((rows, tn), lambda j: (0, j)),
        out_shape=jax.ShapeDtypeStruct((rows, n), F32),
        compiler_params=_params("parallel"),
        name="mod",
    )(cvec, w_mod, b_mod)


def _ctx_kernel(ctx_ref, sh_ref, sc_ref, g_ref, wk_ref, wvt_ref, wi_ref, wf_ref, cw_ref, cb_ref, bi_ref, bf_ref,
                c_out, m_out):
    lc, d = ctx_ref.shape
    dh = d // N_HEADS
    x = ctx_ref[...]
    hc = (x * _rms_scale(x) * g_ref[...]) * (1.0 + sc_ref[...]) + sh_ref[...]
    hcb = hc.astype(BF16)

    kpre = _dot(hcb, wk_ref[...])
    pad = jnp.zeros((8, d), F32)
    kp = jnp.concatenate([pad, kpre, pad], axis=0)
    cw = cw_ref[...]
    acc = cb_ref[...] + cw[0:1, :] * kp[6:6 + lc, :]
    for j in range(1, CONV_W):
        acc = acc + cw[j:j + 1, :] * kp[6 + j:6 + j + lc, :]
    k = _silu(acc) * (dh ** -0.5)
    vt = _dot(wvt_ref[...], hcb, NT).astype(BF16)

    gi = _dot(hcb, wi_ref[...]) + bi_ref[...]
    lf = _log_sigmoid(_dot(hcb, wf_ref[...]) + bf_ref[...])
    lane = lax.broadcasted_iota(jnp.int32, lf.shape, 1)
    w_all = gi + jnp.where(lane < N_HEADS, _dot_left01(_tri01(lc, "gt"), lf), _dot_left01(_tri01(lc, "lt"), lf))
    ones = jnp.ones((AUG_ROWS, lc), BF16)
    for dr in range(N_DIRS):
        for h in range(N_HEADS):
            col = dr * N_HEADS + h
            w = w_all[:, col:col + 1]
            m = jnp.max(w, axis=0, keepdims=True)
            wk = jnp.exp(w - m) * k[:, h * dh:(h + 1) * dh]
            vt_aug = jnp.concatenate([vt[h * dh:(h + 1) * dh, :], ones], axis=0)
            c_out[dr, h] = _dot(vt_aug, wk.astype(BF16))
            m_out[dr, h] = jnp.broadcast_to(m * LOG2E, (1, LANES))


def _ctx_call(ctx, sh_c, sc_c, g1, wk, wv, wi, wf, cw_k, cb_k, bi, bf):
    b, lc, d = ctx.shape
    dh = d // N_HEADS
    row = lambda w: pl.BlockSpec((1, w), lambda i: (0, 0))
    return pl.pallas_call(
        _ctx_kernel,
        grid=(b,),
        in_specs=[pl.BlockSpec((None, lc, d), lambda i: (i, 0, 0)),
                  row(d), row(d), row(d),
                  pl.BlockSpec((d, d), lambda i: (0, 0)),
                  pl.BlockSpec((d, d), lambda i: (0, 0)),
                  pl.BlockSpec((d, LANES), lambda i: (0, 0)),
                  pl.BlockSpec((d, LANES), lambda i: (0, 0)),
                  pl.BlockSpec((CONV_W, d), lambda i: (0, 0)),
                  row(d), row(LANES), row(LANES)],
        out_specs=[pl.BlockSpec((None, N_DIRS, N_HEADS, dh + AUG_ROWS, dh), lambda i: (i, 0, 0, 0, 0)),
                   pl.BlockSpec((None, N_DIRS, N_HEADS, 1, LANES), lambda i: (i, 0, 0, 0, 0))],
        out_shape=[jax.ShapeDtypeStruct((b, N_DIRS, N_HEADS, dh + AUG_ROWS, dh), F32),
                   jax.ShapeDtypeStruct((b, N_DIRS, N_HEADS, 1, LANES), F32)],
        compiler_params=_params("parallel"),
        name="ctx_states",
    )(ctx, sh_c, sc_c, g1, wk, wv, wi, wf, cw_k, cb_k, bi, bf)


def _cummax_lanes(x, reverse):
    n = x.shape[-1]
    lane = lax.broadcasted_iota(jnp.int32, x.shape, x.ndim - 1)
    k = 1
    while k < n:
        if reverse:
            shifted = jnp.where(lane < n - k, pltpu.roll(x, n - k, axis=x.ndim - 1), -jnp.inf)
        else:
            shifted = jnp.where(lane >= k, pltpu.roll(x, k, axis=x.ndim - 1), -jnp.inf)
        x = jnp.maximum(x, shifted)
        k *= 2
    return x


def _proj_kernel(xp_ref, x_ref, xn_ref, sh_ref, sc_ref, g_ref,
                 wpool_ref, wqk_ref, wvt_ref, wo_ref, wg_ref, wit_ref, wft_ref,
                 cw_ref, cb_ref, bit_ref, bft_ref,
                 u_out, q_out, k_out, vt_out, og_out, gg_out, cq_out, rows_out,
                 hx_scr, r_scr):
    tm, d = x_ref.shape
    dh = d // N_HEADS
    i = pl.program_id(1)
    last = pl.num_programs(1) - 1

    x_ext = jnp.concatenate([xp_ref[...], x_ref[...], xn_ref[...]], axis=0)
    hx = (x_ext * _rms_scale(x_ext) * g_ref[...]) * (1.0 + sc_ref[...]) + sh_ref[...]
    n_ext = tm + 2 * HALO
    r_id = lax.broadcasted_iota(jnp.int32, (n_ext, 1), 0)
    valid = jnp.logical_and(jnp.logical_or(i > 0, r_id >= HALO),
                            jnp.logical_or(i < last, r_id < HALO + tm))
    hx_scr[...] = jnp.where(valid, hx, 0.0).astype(BF16)
    hxc = hx_scr[HALO:HALO + tm, :]

    nc = 512
    half = CONV_W // 2

    def qk_dot(c):
        r_scr[c % 2] = _dot(hx_scr[...], wqk_ref[:, c * nc:(c + 1) * nc])

    def qk_conv(c):
        cols = slice(c * nc, (c + 1) * nc)
        r = r_scr.at[c % 2]
        cw = cw_ref[:, cols]
        acc = cb_ref[:, cols] + cw[0:1, :] * r[HALO - half:HALO - half + tm, :]
        for j in range(1, CONV_W):
            acc = acc + cw[j:j + 1, :] * r[HALO - half + j:HALO - half + j + tm, :]
        y = _silu(acc)
        if c * nc < d:
            q_out[:, cols] = y.astype(BF16)
        else:
            k_out[:, c * nc - d:(c + 1) * nc - d] = (y * (dh ** -0.5)).astype(BF16)

    def v_chunk(c):
        cols = slice(c * nc, (c + 1) * nc)
        vt_out[cols, :] = _dot(wvt_ref[cols, :], hxc, NT).astype(BF16)

    def o_chunk(c):
        cols = slice(c * nc, (c + 1) * nc)
        og_out[:, cols] = jax.nn.sigmoid(_dot(hxc, wo_ref[:, cols])).astype(BF16)

    def g_chunk(c):
        cols = slice(c * nc, (c + 1) * nc)
        gg_out[:, cols] = jax.nn.sigmoid(_dot(hxc, wg_ref[:, cols])).astype(BF16)

    qk_dot(0); qk_dot(1)
    qk_conv(0); v_chunk(0); v_chunk(1); qk_dot(2)
    qk_conv(1); o_chunk(0); o_chunk(1); qk_dot(3)
    qk_conv(2); g_chunk(0); g_chunk(1)
    qk_conv(3); g_chunk(2); g_chunk(3)
    u_out[...] = _dot(hxc, wpool_ref[...]).astype(BF16)

    nq = N_DIRS * N_HEADS
    gi_r = _dot(wit_ref[...], hxc, NT) + bit_ref[...]
    lf_r = _log_sigmoid(_dot(wft_ref[...], hxc, NT) + bft_ref[...])
    fwd_sub = lax.broadcasted_iota(jnp.int32, (nq, CHUNK), 0) < N_HEADS
    t_le, t_ge = _tri01(CHUNK, "le"), _tri01(CHUNK, "ge")
    for j in range(tm // CHUNK):
        rows = slice(j * CHUNK, (j + 1) * CHUNK)
        b_r = jnp.where(fwd_sub, _dot_right01(lf_r[:, rows], t_ge), _dot_right01(lf_r[:, rows], t_le))
        c_r = (gi_r[:, rows] - b_r) * LOG2E
        b_r = b_r * LOG2E
        cq_out[rows, :] = c_r.T
        rows_out[0:nq, rows] = b_r
        rows_out[nq:2 * nq, rows] = jnp.where(fwd_sub, _cummax_lanes(c_r, False), _cummax_lanes(c_r, True))


def _proj_call(x, sh, sc, g1, wpool, wqk, wvt, wo, wg, wit, wft, cw, cb, bit, bft):
    b, t, d = x.shape
    tm = TOKEN_TILE
    nt = t // tm
    hb = tm // HALO
    nq = N_DIRS * N_HEADS
    per_b = pl.BlockSpec((None, 1, d), lambda bi_, i: (bi_, 0, 0))
    tile = lambda w: pl.BlockSpec((None, tm, w), lambda bi_, i: (bi_, i, 0))
    tile_t = lambda h: pl.BlockSpec((None, h, tm), lambda bi_, i: (bi_, 0, i))
    const = lambda a: pl.BlockSpec(a.shape, lambda bi_, i: (0,) * a.ndim)
    out_shapes = [jax.ShapeDtypeStruct((b, t, d // 2), BF16),
                  jax.ShapeDtypeStruct((b, t, d), BF16),
                  jax.ShapeDtypeStruct((b, t, d), BF16),
                  jax.ShapeDtypeStruct((b, d, t), BF16),
                  jax.ShapeDtypeStruct((b, t, d), BF16),
                  jax.ShapeDtypeStruct((b, t, 2 * d), BF16),
                  jax.ShapeDtypeStruct((b, t, nq), F32),
                  jax.ShapeDtypeStruct((b, 2 * nq, t), F32)]
    out_specs = [tile(d // 2), tile(d), tile(d), tile_t(d), tile(d), tile(2 * d),
                 tile(nq), tile_t(2 * nq)]
    return pl.pallas_call(
        _proj_kernel,
        grid=(b, nt),
        in_specs=[pl.BlockSpec((None, HALO, d), lambda bi_, i: (bi_, jnp.maximum(i * hb - 1, 0), 0)),
                  tile(d),
                  pl.BlockSpec((None, HALO, d), lambda bi_, i: (bi_, jnp.minimum((i + 1) * hb, t // HALO - 1), 0)),
                  per_b, per_b, const(g1),
                  const(wpool), const(wqk), const(wvt), const(wo), const(wg),
                  const(wit), const(wft),
                  const(cw), const(cb), const(bit), const(bft)],
        out_specs=out_specs,
        out_shape=out_shapes,
        scratch_shapes=[pltpu.VMEM((tm + 2 * HALO, d), BF16),
                        pltpu.VMEM((2, tm + 2 * HALO, 512), F32)],
        compiler_params=_params("parallel", "parallel"),
        name="proj",
    )(x, x, x, sh, sc, g1, wpool, wqk, wvt, wo, wg, wit, wft, cw, cb, bit, bft)


def _pool_kernel(u_ref, mix_ref, scale_ref, inv_ref, p_out, pad_scr):
    t, pw = u_ref.shape
    gw = pw // len(POOL_WINDOWS)
    tile = 256
    maxlo = max(POOL_WINDOWS) // 2
    padr = maxlo * GRID_W
    ti = lax.broadcasted_iota(jnp.int32, (tile, tile), 0)
    tj = lax.broadcasted_iota(jnp.int32, (tile, tile), 1)
    same_row = _div_pow2(ti, GRID_W) == _div_pow2(tj, GRID_W)
    ci, cj = _mod_pow2(ti, GRID_W), _mod_pow2(tj, GRID_W)

    def span(dlt, ext):
        return pad_scr[padr + (dlt - ext) * GRID_W:padr + (dlt + ext) * GRID_W + t, :]

    for g, side in enumerate(POOL_WINDOWS):
        lo, hi = side // 2, side - side // 2
        assert lo == hi and side & (side - 1) == 0
        cols = slice(g * gw, (g + 1) * gw)
        pad_scr[0:padr, :] = jnp.zeros((padr, gw), F32)
        pad_scr[padr + t:padr + t + padr, :] = jnp.zeros((padr, gw), F32)
        band = jnp.logical_and(same_row, jnp.logical_and(cj >= ci - lo, cj < ci + hi))
        pw01 = jnp.where(band, 1.0, 0.0).astype(BF16)
        for k in range(t // tile):
            rs = slice(k * tile, (k + 1) * tile)
            pad_scr[padr + k * tile:padr + (k + 1) * tile, :] = _dot(pw01, u_ref[rs, cols])
        ext = (side - 2) // 2
        tot = span(-1, ext) + span(0, ext)
        k = 2
        while k < side:
            pad_scr[padr - ext * GRID_W:padr + ext * GRID_W + t, :] = tot
            ext = (side - 2 * k) // 2
            tot = span(-(k // 2), ext) + span(k // 2, ext)
            k *= 2
        a = tot * inv_ref[g] - u_ref[:, cols].astype(F32)
        p = _dot(a.astype(BF16), mix_ref[g]) * scale_ref[:, cols]
        p_out[:, cols] = p.astype(BF16)


def _pool_inv_counts(t, gw):
    rows = t // GRID_W
    r, c = np.arange(t) // GRID_W, np.arange(t) % GRID_W
    out = []
    for side in POOL_WINDOWS:
        lo, hi = side // 2, side - side // 2
        cnt = ((np.minimum(r + hi, rows) - np.maximum(r - lo, 0))
               * (np.minimum(c + hi, GRID_W) - np.maximum(c - lo, 0)))
        out.append(np.broadcast_to((1.0 / cnt).astype(np.float32)[:, None], (t, gw)))
    return jnp.asarray(np.stack(out))


def _pool_call(u, mix, scale):
    b, t, pw = u.shape
    ng = len(POOL_WINDOWS)
    gw = pw // ng
    padr = (max(POOL_WINDOWS) // 2) * GRID_W
    return pl.pallas_call(
        _pool_kernel,
        grid=(b,),
        in_specs=[pl.BlockSpec((None, t, pw), lambda i: (i, 0, 0)),
                  pl.BlockSpec(mix.shape, lambda i: (0, 0, 0)),
                  pl.BlockSpec((1, pw), lambda i: (0, 0)),
                  pl.BlockSpec((ng, t, gw), lambda i: (0, 0, 0))],
        out_specs=pl.BlockSpec((None, t, pw), lambda i: (i, 0, 0)),
        out_shape=jax.ShapeDtypeStruct((b, t, pw), BF16),
        scratch_shapes=[pltpu.VMEM((t + 2 * padr, gw), F32)],
        compiler_params=_params("parallel"),
        name="pool",
    )(u, mix, scale, _pool_inv_counts(t, gw))


def _mlstm_dir(q_ref, k_ref, vt_ref, cq_ref, rows_ref, c_scr, m_scr, reverse):
    L, d = q_ref.shape
    dh = d // N_HEADS
    nhalf = dh // LANES
    nq = N_DIRS * N_HEADS
    si = lax.broadcasted_iota(jnp.int32, (L, L), 0)
    tj = lax.broadcasted_iota(jnp.int32, (L, L), 1)
    mask = (si >= tj) if reverse else (si <= tj)
    ones = jnp.ones((AUG_ROWS, L), BF16)
    end = 0 if reverse else L - 1
    off = N_HEADS if reverse else 0

    hs_all = []
    for h in range(N_HEADS):
        hs = slice(h * dh, (h + 1) * dh)
        st = off + h
        q = q_ref[:, hs]
        k = k_ref[:, hs]
        vt_aug = jnp.concatenate([vt_ref[hs, :], ones], axis=0)
        c_c = cq_ref[:, st:st + 1]
        b_r = rows_ref[st:st + 1, :]
        cm_r = rows_ref[nq + st:nq + st + 1, :]
        m_prev = m_scr[st][:, 0:1]
        ct_prev = c_scr[st]

        mm = jnp.maximum(cm_r, m_prev)
        w_inter = jnp.exp2(m_prev - mm)
        st_mat = (_dot(k, q, NT) * jnp.exp2(jnp.where(mask, c_c - mm, -jnp.inf))).astype(BF16)
        intra = _dot(vt_aug, st_mat)
        inter = _dot(ct_prev.astype(BF16), q, NT)
        den = w_inter * inter[dh:dh + 1, :] + intra[dh:dh + 1, :]
        inv = 1.0 / jnp.maximum(jnp.abs(den), jnp.exp2(-(b_r + mm)))
        hs_all.append((w_inter * inter[0:dh, :] + intra[0:dh, :]) * inv)

        g_tot = b_r[:, end:end + 1]
        m_new = g_tot + jnp.maximum(m_prev, cm_r[:, end:end + 1])
        decay = jnp.exp2(g_tot + m_prev - m_new)
        wk = (k.astype(F32) * jnp.exp2(g_tot + c_c - m_new)).astype(BF16)
        c_scr[st] = decay * ct_prev + _dot(vt_aug, wk)
        m_scr[st] = jnp.broadcast_to(m_new, (1, LANES))
    return hs_all


def _mlstm_kernel(qf_ref, kf_ref, vf_ref, ogf_ref, cqf_ref, rwf_ref,
                  qb_ref, kb_ref, vb_ref, ogb_ref, cqb_ref, rwb_ref,
                  c0_ref, m0_ref, ng_ref, o_ref,
                  c_scr, m_scr, hf_scr, hb_scr):
    L, d = qf_ref.shape
    dh = d // N_HEADS
    nhalf = dh // LANES
    s = pl.program_id(1)
    nch = pl.num_programs(1)
    half = nch // 2

    @pl.when(s == 0)
    def _():
        for j in range(N_DIRS * N_HEADS):
            c_scr[j] = c0_ref[j // N_HEADS, j % N_HEADS]
            m_scr[j] = m0_ref[j // N_HEADS, j % N_HEADS]

    h_f = _mlstm_dir(qf_ref, kf_ref, vf_ref, cqf_ref, rwf_ref, c_scr, m_scr, False)
    h_b = _mlstm_dir(qb_ref, kb_ref, vb_ref, cqb_ref, rwb_ref, c_scr, m_scr, True)

    @pl.when(s < half)
    def _():
        for h in range(N_HEADS):
            hs = slice(h * dh, (h + 1) * dh)
            hf_scr[s, hs, :] = h_f[h]
            hb_scr[half - 1 - s, hs, :] = h_b[h]

    @pl.when(s >= half)
    def _():
        def finish(ht, h, og_ref, out):
            hs = slice(h * dh, (h + 1) * dh)
            scale = lax.rsqrt(jnp.mean(ht * ht, axis=0, keepdims=True) + NORM_EPS)
            y = jnp.concatenate([ht[:, i * LANES:(i + 1) * LANES] * scale[:, i * LANES:(i + 1) * LANES]
                                 * ng_ref[hs, :] for i in range(L // LANES)], axis=1).T
            out[:, hs] = (y * og_ref[:, hs].astype(F32)).astype(BF16)

        for h in range(N_HEADS):
            hs = slice(h * dh, (h + 1) * dh)
            finish(h_f[h] + hb_scr[s - half, hs, :], h, ogf_ref, o_ref.at[1])
            finish(h_b[h] + hf_scr[nch - 1 - s, hs, :], h, ogb_ref, o_ref.at[0])


def _mlstm_call(q, k, vt, og, cq, rows, c0, m0, norm_g):
    b, t, d = q.shape
    dh = d // N_HEADS
    L = CHUNK
    nch = t // L
    half = nch // 2
    assert nch % 2 == 0
    nq = N_DIRS * N_HEADS

    def specs(chunk):
        seq = lambda w: pl.BlockSpec((None, L, w), lambda bi, s: (bi, chunk(s), 0))
        seq_t = lambda h: pl.BlockSpec((None, h, L), lambda bi, s: (bi, 0, chunk(s)))
        return [seq(d), seq(d), seq_t(d), seq(d), seq(nq), seq_t(2 * nq)]

    state = lambda w0, w1: pl.BlockSpec((None, N_DIRS, N_HEADS, w0, w1), lambda bi, s: (bi, 0, 0, 0, 0))
    return pl.pallas_call(
        _mlstm_kernel,
        grid=(b, nch),
        in_specs=specs(lambda s: s) + specs(lambda s: nch - 1 - s) + [
            state(dh + AUG_ROWS, dh), state(1, LANES), pl.BlockSpec((d, LANES), lambda bi, s: (0, 0))],
        out_specs=pl.BlockSpec((None, 2, None, L, d), lambda bi, s: (bi, 0, jnp.maximum(s - half, 0), 0, 0)),
        out_shape=jax.ShapeDtypeStruct((b, 2, half, L, d), BF16),
        scratch_shapes=[pltpu.VMEM((N_DIRS * N_HEADS, dh + AUG_ROWS, dh), F32),
                        pltpu.VMEM((N_DIRS * N_HEADS, 1, LANES), F32),
                        pltpu.VMEM((half, d, L), F32),
                        pltpu.VMEM((half, d, L), F32)],
        compiler_params=_params("parallel", "arbitrary"),
        name="mlstm",
    )(q, k, vt, og, cq, rows, q, k, vt, og, cq, rows, c0, m0, norm_g)


def _merge_kernel(p_ref, m_ref, gg_ref, x_ref, g1_ref, sh2_ref, sc2_ref, n2_ref,
                  wpo_ref, wmo_ref, wout_ref, wr_ref, x1_out, h2_out, aff_out):
    tm, d = x_ref.shape
    cpt = m_ref.shape[0]
    L = m_ref.shape[1]
    upper = pl.program_id(1) >= pl.num_programs(1) // 2
    sub = TOKEN_TILE
    cps = sub // L
    streams = [slice(r * sub, (r + 1) * sub) for r in range(tm // sub)]

    def branches(r):
        m = jnp.where(upper,
                      jnp.concatenate([m_ref[r * cps + j] for j in range(cps)], axis=0),
                      jnp.concatenate([m_ref[cpt - 1 - r * cps - j] for j in range(cps)], axis=0))
        return _dot(p_ref[streams[r], :], wpo_ref[...]), _dot(m, wmo_ref[...])

    def mix(r, a, mm):
        rows = streams[r]
        return (gg_ref[rows, 0:d].astype(F32) * a + gg_ref[rows, d:2 * d].astype(F32) * mm).astype(BF16)

    def residual(r, mixed):
        x1 = x_ref[streams[r], :] + g1_ref[...] * _dot(mixed, wout_ref[...])
        x1_out[streams[r], :] = x1
        return x1

    def tail(r, x1):
        rows = streams[r]
        h2 = (x1 * _rms_scale(x1) * n2_ref[...]) * (1.0 + sc2_ref[...]) + sh2_ref[...]
        h2_out[rows, :] = h2.astype(BF16)
        logits = _dot3(wr_ref[...], h2, NT)
        z = jnp.exp(logits - jnp.max(logits, axis=0, keepdims=True))
        aff_out[:, rows] = z / jnp.sum(z, axis=0, keepdims=True)

    n = len(streams)
    ab = [branches(r) for r in range(n)]
    mixed = [mix(r, *ab[r]) for r in range(n)]
    x1s = [residual(r, mixed[r]) for r in range(n)]
    for r in range(n):
        tail(r, x1s[r])


def _merge_call(p, m, gg, x, gate1, sh2, sc2, n2, wpo, wmo, wout, wr_t):
    b, t, d = x.shape
    tm = min(MERGE_TILE, t // 2)
    e = wr_t.shape[0]
    per_b = pl.BlockSpec((None, 1, d), lambda bi, i: (bi, 0, 0))
    tile = lambda w: pl.BlockSpec((None, tm, w), lambda bi, i: (bi, i, 0))
    const = lambda a: pl.BlockSpec(a.shape, lambda bi, i: (0,) * a.ndim)
    cpt = tm // CHUNK
    nth = t // tm // 2
    m_spec = pl.BlockSpec((None, None, cpt, CHUNK, d),
                          lambda bi, i: (bi, i // nth, jnp.where(i >= nth, i - nth, nth - 1 - i), 0, 0))
    return pl.pallas_call(
        _merge_kernel,
        grid=(b, t // tm),
        in_specs=[tile(d // 2), m_spec, tile(2 * d), tile(d), per_b, per_b, per_b, const(n2),
                  const(wpo), const(wmo), const(wout), const(wr_t)],
        out_specs=[tile(d), tile(d), pl.BlockSpec((None, e, tm), lambda bi, i: (bi, 0, i))],
        out_shape=[jax.ShapeDtypeStruct((b, t, d), F32),
                   jax.ShapeDtypeStruct((b, t, d), BF16),
                   jax.ShapeDtypeStruct((b, e, t), F32)],
        compiler_params=_params("parallel", "parallel"),
        name="merge",
    )(p, m, gg, x, gate1, sh2, sc2, n2, wpo, wmo, wout, wr_t)


def _route_kernel(aff_ref, slot_out, lo_out, *, cap):
    e, t = aff_ref.shape
    aff = aff_ref[...]

    def step(i, thr):
        cand = thr | (jnp.int32(1) << (30 - i))
        cnt = jnp.sum(jnp.where(aff >= pltpu.bitcast(cand, F32), 1.0, 0.0), axis=-1, keepdims=True)
        return jnp.where(cnt >= cap, cand, thr)

    thr = pltpu.bitcast(lax.fori_loop(0, 31, step, jnp.zeros((e, 1), jnp.int32)), F32)
    gt = aff > thr
    eq = aff == thr
    need = cap - jnp.sum(jnp.where(gt, 1.0, 0.0), axis=-1, keepdims=True).astype(jnp.int32)

    seg = 256
    t_ge = _tri01(seg, "ge")

    def prefix_incl(x01):
        outs, carries, carry = [], [], jnp.zeros((e, 1), F32)
        for j in range(t // seg):
            p = _dot(x01[:, j * seg:(j + 1) * seg].astype(BF16), t_ge) + carry
            outs.append(p)
            carry = p[:, seg - 1:seg]
            carries.append(carry)
        return jnp.concatenate(outs, axis=1), carries

    eq_f = jnp.where(eq, 1.0, 0.0)
    tie_rank = (prefix_incl(eq_f)[0] - eq_f).astype(jnp.int32)
    sel = jnp.logical_or(gt, jnp.logical_and(eq, tie_rank < need))
    rank, carries = prefix_incl(jnp.where(sel, 1.0, 0.0))
    slot_out[...] = jnp.where(sel, rank.astype(jnp.int32) - 1, -1)

    lane = lax.broadcasted_iota(jnp.int32, (e, LANES), 1)
    lo = jnp.zeros((e, LANES), F32)
    per_tile = TOKEN_TILE // seg
    for c in range(1, t // TOKEN_TILE + 1):
        lo = jnp.where(lane == c, carries[c * per_tile - 1], lo)
    lo_out[...] = lo.astype(jnp.int32)


def _route_call(aff_t, cap):
    b, e, t = aff_t.shape
    n = b * e
    slot, lo = pl.pallas_call(
        functools.partial(_route_kernel, cap=cap),
        grid=(1,),
        in_specs=[pl.BlockSpec((n, t), lambda i: (0, 0))],
        out_specs=[pl.BlockSpec((n, t), lambda i: (0, 0)),
                   pl.BlockSpec((n, LANES), lambda i: (0, 0))],
        out_shape=[jax.ShapeDtypeStruct((n, t), jnp.int32),
                   jax.ShapeDtypeStruct((n, LANES), jnp.int32)],
        compiler_params=_params("arbitrary"),
        name="route",
    )(aff_t.reshape(n, t))
    return slot.reshape(b, e, t), lo.reshape(b, e, LANES)


SLOT_WINDOW = 96
SLOT_ALIGN = 16
EXPERT_GROUP = 8


def _aligned(lo):
    return jnp.bitwise_and(lo, -SLOT_ALIGN)


def _window_start(nominal, cap):
    return pl.multiple_of(jnp.minimum(nominal, cap - SLOT_WINDOW), SLOT_ALIGN)


def _n_windows(lo, hi):
    return lax.div(hi - _aligned(lo) + (SLOT_WINDOW - 1), SLOT_WINDOW)


def _gather_kernel(lo_ref, h2_ref, slot_ref, xe_out, *, cap, n_tiles):
    n_exp = slot_ref.shape[0]
    tc, w = TOKEN_TILE, SLOT_WINDOW
    b = pl.program_id(0)
    stride = n_tiles + 1
    xe_out[...] = jnp.zeros(xe_out.shape, BF16)
    s_id = lax.broadcasted_iota(jnp.int32, (w, tc), 0)

    def add_rows(e, start, z):
        xe_out[e, pl.ds(start, w), :] = xe_out[e, pl.ds(start, w), :] + z.astype(BF16)

    def tile_body(c, carry):
        t0 = pl.multiple_of(c * tc, tc)
        for g0 in range(0, n_exp, EXPERT_GROUP):
            starts, blocks = [], []
            for e in range(g0, g0 + EXPERT_GROUP):
                a0 = _window_start(_aligned(lo_ref[b, e * stride + c]), cap)
                hit = (s_id + a0) == slot_ref[e:e + 1, pl.ds(t0, tc)]
                blocks.append(jnp.where(hit, 1.0, 0.0).astype(BF16))
                starts.append(a0)
            z = _dot(jnp.concatenate(blocks, axis=0), h2_ref[pl.ds(t0, tc), :])
            for j in range(EXPERT_GROUP):
                add_rows(g0 + j, starts[j], z[j * w:(j + 1) * w, :])
        for e in range(n_exp):
            lo, hi = lo_ref[b, e * stride + c], lo_ref[b, e * stride + c + 1]

            def window_body(k, carry2, e=e, lo=lo):
                nominal = _aligned(lo) + k * w
                a = _window_start(nominal, cap)
                srow = slot_ref[e:e + 1, pl.ds(t0, tc)]
                hit = jnp.logical_and((s_id + a) == srow, srow >= nominal)
                add_rows(e, a, _dot(jnp.where(hit, 1.0, 0.0).astype(BF16), h2_ref[pl.ds(t0, tc), :]))
                return carry2

            lax.fori_loop(1, _n_windows(lo, hi), window_body, 0)
        return carry

    lax.fori_loop(0, n_tiles, tile_body, 0)


def _gather_call(lo2, h2, slot_t, cap):
    b, t, d = h2.shape
    e = slot_t.shape[1]
    n_tiles = t // TOKEN_TILE
    grid_spec = pltpu.PrefetchScalarGridSpec(
        num_scalar_prefetch=1,
        grid=(b,),
        in_specs=[pl.BlockSpec((None, t, d), lambda i, lo: (i, 0, 0)),
                  pl.BlockSpec((None, e, t), lambda i, lo: (i, 0, 0))],
        out_specs=pl.BlockSpec((None, e, cap, d), lambda i, lo: (i, 0, 0, 0)),
    )
    return pl.pallas_call(
        functools.partial(_gather_kernel, cap=cap, n_tiles=n_tiles),
        grid_spec=grid_spec,
        out_shape=jax.ShapeDtypeStruct((b, e, cap, d), BF16),
        compiler_params=_params("arbitrary"),
        name="gather",
    )(lo2, h2, slot_t)


def _expert_kernel(xe_ref, wg_ref, wu_ref, wd_ref, ye_out, wg_scr, wu_scr, wd_scr):
    e, b = pl.program_id(0), pl.program_id(1)
    n_exp = pl.num_programs(0) - 1
    slab = wg_ref.shape[0]
    f = wg_scr.shape[2]

    @pl.when(e < n_exp)
    def _():
        slot = lax.rem(e, 2)
        rows = pl.ds(pl.multiple_of(b * slab, slab), slab)
        wg_scr[slot, rows, :] = wg_ref[...].astype(BF16)
        wu_scr[slot, rows, :] = wu_ref[...].astype(BF16)
        wd_scr[slot, rows, :] = wd_ref[...].astype(BF16)

    @pl.when(e > 0)
    def _():
        slot = lax.rem(e + 1, 2)
        xe = xe_ref[...]
        fc = 512
        y = None
        for c in range(f // fc):
            cols = slice(c * fc, (c + 1) * fc)
            hid = _silu(_dot(xe, wg_scr[slot, :, cols])) * _dot(xe, wu_scr[slot, :, cols])
            part = _dot(hid.astype(BF16), wd_scr[slot, cols, :])
            y = part if y is None else y + part
        ye_out[...] = y.astype(BF16)


def _expert_call(xe, wg, wu, wd):
    b, e, cap, d = xe.shape
    f = wg.shape[2]
    assert d % b == 0 and f % b == 0
    w_spec = lambda rows, cols: pl.BlockSpec((None, rows // b, cols),
                                             lambda ei, bi: (jnp.minimum(ei, e - 1), bi, 0))
    return pl.pallas_call(
        _expert_kernel,
        grid=(e + 1, b),
        in_specs=[pl.BlockSpec((None, None, cap, d), lambda ei, bi: (bi, jnp.maximum(ei - 1, 0), 0, 0)),
                  w_spec(d, f), w_spec(d, f), w_spec(f, d)],
        out_specs=pl.BlockSpec((None, None, cap, d),
                               lambda ei, bi: (jnp.where(ei == 0, 0, bi), jnp.maximum(ei - 1, 0), 0, 0)),
        out_shape=jax.ShapeDtypeStruct((b, e, cap, d), BF16),
        scratch_shapes=[pltpu.VMEM((2, d, f), BF16), pltpu.VMEM((2, d, f), BF16), pltpu.VMEM((2, f, d), BF16)],
        compiler_params=_params("arbitrary", "arbitrary"),
        name="experts",
    )(xe, wg, wu, wd)


def _combine_kernel(lo_ref, ye_ref, slot_ref, aff_ref, x1_ref, g2_ref, fg_ref, o_ref, acc_scr,
                    *, final_norm, n_tiles):
    n_exp, cap, d = ye_ref.shape
    tm, w = x1_ref.shape[0], SLOT_WINDOW
    b, i = pl.program_id(0), pl.program_id(1)
    stride = n_tiles + 1
    s_id = lax.broadcasted_iota(jnp.int32, (w, tm), 0)

    for g0 in range(0, n_exp, EXPERT_GROUP):
        ps, rows = [], []
        for e in range(g0, g0 + EXPERT_GROUP):
            a0 = _window_start(_aligned(lo_ref[b, e * stride + i]), cap)
            hit = (s_id + a0) == slot_ref[e:e + 1, :]
            ps.append(jnp.where(hit, aff_ref[e:e + 1, :], 0.0).astype(BF16))
            rows.append(ye_ref[e, pl.ds(a0, w), :])
        part = _dot(jnp.concatenate(ps, axis=0), jnp.concatenate(rows, axis=0), TN)
        if g0 == 0:
            acc_scr[...] = part
        else:
            acc_scr[...] += part

    for e in range(n_exp):
        lo, hi = lo_ref[b, e * stride + i], lo_ref[b, e * stride + i + 1]

        def window_body(k, carry, e=e, lo=lo):
            nominal = _aligned(lo) + k * w
            a = _window_start(nominal, cap)
            sr = slot_ref[e:e + 1, :]
            hit = jnp.logical_and((s_id + a) == sr, sr >= nominal)
            p = jnp.where(hit, aff_ref[e:e + 1, :], 0.0).astype(BF16)
            acc_scr[...] += _dot(p, ye_ref[e, pl.ds(a, w), :], TN)
            return carry

        lax.fori_loop(1, _n_windows(lo, hi), window_body, 0)

    x2 = x1_ref[...] + g2_ref[...] * acc_scr[...]
    o_ref[...] = x2 * _rms_scale(x2) * fg_ref[...] if final_norm else x2


def _combine_call(lo2, ye, slot_c, aff_c, x1, gate2, final_g, final_norm):
    b, t, d = x1.shape
    e, cap = ye.shape[1], ye.shape[2]
    tm = TOKEN_TILE
    grid_spec = pltpu.PrefetchScalarGridSpec(
        num_scalar_prefetch=1,
        grid=(b, t // tm),
        in_specs=[pl.BlockSpec((None, e, cap, d), lambda bi, i, lo: (bi, 0, 0, 0)),
                  pl.BlockSpec((None, e, tm), lambda bi, i, lo: (bi, 0, i)),
                  pl.BlockSpec((None, e, tm), lambda bi, i, lo: (bi, 0, i)),
                  pl.BlockSpec((None, tm, d), lambda bi, i, lo: (bi, i, 0)),
                  pl.BlockSpec((None, 1, d), lambda bi, i, lo: (bi, 0, 0)),
                  pl.BlockSpec((1, d), lambda bi, i, lo: (0, 0))],
        out_specs=pl.BlockSpec((None, tm, d), lambda bi, i, lo: (bi, i, 0)),
        scratch_shapes=[pltpu.VMEM((tm, d), F32)],
    )
    return pl.pallas_call(
        functools.partial(_combine_kernel, final_norm=final_norm, n_tiles=t // tm),
        grid_spec=grid_spec,
        out_shape=jax.ShapeDtypeStruct((b, t, d), F32),
        compiler_params=_params("parallel", "arbitrary"),
        name="combine",
    )(lo2, ye, slot_c, aff_c, x1, gate2, final_g)


def _layer(x, c, ctx, c_ctx, w_mod, b_mod, norm1_g, norm2_g, w_in, conv_w, conv_b, b_if,
           pool_mix, pool_scale, mlstm_norm_g, w_pool_out, w_mlstm_out, w_out,
           w_router, w_gate, w_up, w_down):
    b, t, d = x.shape
    pw = d // 2
    ng = N_DIRS * 2 * N_HEADS
    q_off, k_off, v_off, o_off = pw, pw + d, pw + 2 * d, pw + 3 * d
    if_off, gate_off = pw + 4 * d, pw + 4 * d + ng
    cap = EC_CAPACITY * t // N_EXPERTS
    row = lambda a: a.reshape(1, -1)

    rows = -(-(b + 1) // 8) * 8
    cvec = jnp.zeros((rows, d), F32).at[:b].set(c).at[b].set(c_ctx)
    mod = _mod_call(cvec, w_mod, row(b_mod))
    shift1, scale1, gate1, shift2, scale2, gate2 = [
        mod[:b, j * d:(j + 1) * d].reshape(b, 1, d) for j in range(6)]
    shift_c, scale_c = mod[b:b + 1, 0:d], mod[b:b + 1, d:2 * d]

    w_in_b = w_in.astype(BF16)
    nq = N_DIRS * N_HEADS
    w_if3 = w_in_b[:, if_off:gate_off].reshape(d, N_DIRS, 2, N_HEADS)
    b_if3 = b_if.reshape(N_DIRS, 2, N_HEADS)
    w_i, w_f = w_if3[:, :, 0, :].reshape(d, nq), w_if3[:, :, 1, :].reshape(d, nq)
    b_i, b_f = b_if3[:, 0, :].reshape(nq), b_if3[:, 1, :].reshape(nq)
    pad_w = lambda w: jnp.zeros((d, LANES), BF16).at[:, :nq].set(w)
    pad_b = lambda v: jnp.zeros((1, LANES), F32).at[0, :nq].set(v)

    w_vt = w_in_b[:, v_off:o_off].T
    c0, m0 = _ctx_call(ctx, shift_c, scale_c, row(norm1_g),
                       w_in_b[:, k_off:v_off], w_vt, pad_w(w_i), pad_w(w_f),
                       conv_w[:, d:], row(conv_b[d:]), pad_b(b_i), pad_b(b_f))

    u, q, k, vt, og, gg, cq, rows = _proj_call(
        x, shift1, scale1, row(norm1_g),
        w_in_b[:, 0:q_off], w_in_b[:, q_off:v_off], w_vt, w_in_b[:, o_off:if_off],
        w_in_b[:, gate_off:], w_i.T, w_f.T,
        conv_w, row(conv_b), b_i.reshape(nq, 1), b_f.reshape(nq, 1))

    p = _pool_call(u, pool_mix.astype(BF16), row(pool_scale))
    m = _mlstm_call(q, k, vt, og, cq, rows, c0, m0, jnp.broadcast_to(mlstm_norm_g[:, None], (d, LANES)))

    x1, h2, aff_t = _merge_call(p, m, gg, x, gate1, shift2, scale2, row(norm2_g),
                                w_pool_out.astype(BF16), w_mlstm_out.astype(BF16), w_out.astype(BF16),
                                w_router.T)
    slot_t, lo = _route_call(aff_t, cap)
    lo2 = lo[:, :, :t // TOKEN_TILE + 1].reshape(b, -1)
    xe = _gather_call(lo2, h2, slot_t, cap)
    ye = _expert_call(xe, w_gate, w_up, w_down)
    return lo2, ye, slot_t, aff_t, x1, gate2


def kernel(x, c, ctx, c_ctx, w_mod, b_mod, norm1_g, norm2_g, w_in, conv_w, conv_b, b_if, pool_mix, pool_scale,
           mlstm_norm_g, w_pool_out, w_mlstm_out, w_out, w_router, w_gate, w_up, w_down, final_g):
    depth = w_mod.shape[0]
    for l in range(depth):
        lo2, ye, slot_c, aff_c, x1, gate2 = _layer(
            x, c, ctx, c_ctx, w_mod[l], b_mod[l], norm1_g[l], norm2_g[l], w_in[l], conv_w[l], conv_b[l],
            b_if[l], pool_mix[l], pool_scale[l], mlstm_norm_g[l], w_pool_out[l], w_mlstm_out[l], w_out[l],
            w_router[l], w_gate[l], w_up[l], w_down[l])
        x = _combine_call(lo2, ye, slot_c, aff_c, x1, gate2, final_g.reshape(1, -1), final_norm=l == depth - 1)
    return x
```

```python
import functools

import jax
import jax.numpy as jnp
import numpy as np
from jax import lax
from jax.experimental import pallas as pl
from jax.experimental.pallas import tpu as pltpu

F32 = jnp.float32
BF16 = jnp.bfloat16

GRID_W = 64
POOL_WINDOWS = (2, 4, 8, 16)
N_HEADS = 4
CONV_W = 5
N_DIRS = 2
N_EXPERTS = 16
EC_CAPACITY = 2
NORM_EPS = 1e-6
LOG2E = 1.4426950408889634

CHUNK = 256
TOKEN_TILE = 512
MERGE_TILE = 1024
HALO = 16
LANES = 128
AUG_ROWS = 16
V7X_VMEM_LIMIT_BYTES = 56 * 1024 * 1024

NN = (((1,), (0,)), ((), ()))
NT = (((1,), (1,)), ((), ()))
TN = (((0,), (0,)), ((), ()))


def _dot(a, b, dims=NN):
    return lax.dot_general(a, b, dims, preferred_element_type=F32)


def _split2(a):
    hi = a.astype(BF16)
    lo = (a - hi.astype(F32)).astype(BF16)
    return hi, lo


def _split3(a):
    a1 = a.astype(BF16)
    r1 = a - a1.astype(F32)
    a2 = r1.astype(BF16)
    a3 = (r1 - a2.astype(F32)).astype(BF16)
    return a1, a2, a3


def _dot3(a, b, dims=NN):
    ah, al = _split2(a)
    bh, bl = _split2(b)
    return _dot(ah, bh, dims) + _dot(ah, bl, dims) + _dot(al, bh, dims)


def _dot_left01(t01, a):
    a1, a2, a3 = _split3(a)
    return _dot(t01, a1) + _dot(t01, a2) + _dot(t01, a3)


def _dot_right01(a, t01):
    a1, a2, a3 = _split3(a)
    return _dot(a1, t01) + _dot(a2, t01) + _dot(a3, t01)


def _silu(x):
    return x * jax.nn.sigmoid(x)


def _log_sigmoid(x):
    return jnp.minimum(x, 0.0) - jnp.log1p(jnp.exp(-jnp.abs(x)))


def _rms_scale(x):
    return lax.rsqrt(jnp.mean(x * x, axis=-1, keepdims=True) + NORM_EPS)


def _tri01(n, kind):
    i = lax.broadcasted_iota(jnp.int32, (n, n), 0)
    j = lax.broadcasted_iota(jnp.int32, (n, n), 1)
    cond = {"le": j <= i, "ge": j >= i, "lt": j < i, "gt": j > i}[kind]
    return jnp.where(cond, 1.0, 0.0).astype(BF16)


def _shift(n):
    assert n & (n - 1) == 0, n
    return n.bit_length() - 1


def _div_pow2(x, n):
    return lax.shift_right_logical(x, _shift(n))


def _mod_pow2(x, n):
    return jnp.bitwise_and(x, n - 1)


def _params(*sem, flags=None):
    return pltpu.CompilerParams(dimension_semantics=sem, vmem_limit_bytes=V7X_VMEM_LIMIT_BYTES, flags=flags)


def _resident(shape):
    nd = len(shape)
    return pl.BlockSpec(shape, lambda *_: (0,) * nd)


def _mod_kernel(c_ref, w_ref, b_ref, o_ref):
    o_ref[...] = _dot3(_silu(c_ref[...]), w_ref[...]) + b_ref[...]


def _mod_call(cvec, w_mod, b_mod):
    rows, d = cvec.shape
    n = w_mod.shape[1]
    tn = 1536
    return pl.pallas_call(
        _mod_kernel,
        grid=(n // tn,),
        in_specs=[pl.BlockSpec((rows, d), lambda j: (0, 0)),
                  pl.BlockSpec((d, tn), lambda j: (0, j)),
                  pl.BlockSpec((1, tn), lambda j: (0, j))],
        out_specs=pl.BlockSpec((rows, tn), lambda j: (0, j)),
        out_shape=jax.ShapeDtypeStruct((rows, n), F32),
        compiler_params=_params("parallel"),
        name="mod",
    )(cvec, w_mod, b_mod)


def _ctx_kernel(ctx_ref, sh_ref, sc_ref, g_ref, wk_ref, wvt_ref, wi_ref, wf_ref, cw_ref, cb_ref, bi_ref, bf_ref,
                c_out, m_out):
    lc, d = ctx_ref.shape
    dh = d // N_HEADS
    x = ctx_ref[...]
    hc = (x * _rms_scale(x) * g_ref[...]) * (1.0 + sc_ref[...]) + sh_ref[...]
    hcb = hc.astype(BF16)

    kpre = _dot(hcb, wk_ref[...])
    pad = jnp.zeros((8, d), F32)
    kp = jnp.concatenate([pad, kpre, pad], axis=0)
    cw = cw_ref[...]
    acc = cb_ref[...] + cw[0:1, :] * kp[6:6 + lc, :]
    for j in range(1, CONV_W):
        acc = acc + cw[j:j + 1, :] * kp[6 + j:6 + j + lc, :]
    k = _silu(acc) * (dh ** -0.5)
    vt = _dot(wvt_ref[...], hcb, NT).astype(BF16)

    gi = _dot(hcb, wi_ref[...]) + bi_ref[...]
    lf = _log_sigmoid(_dot(hcb, wf_ref[...]) + bf_ref[...])
    lane = lax.broadcasted_iota(jnp.int32, lf.shape, 1)
    w_all = gi + jnp.where(lane < N_HEADS, _dot_left01(_tri01(lc, "gt"), lf), _dot_left01(_tri01(lc, "lt"), lf))
    ones = jnp.ones((AUG_ROWS, lc), BF16)
    for dr in range(N_DIRS):
        for h in range(N_HEADS):
            col = dr * N_HEADS + h
            w = w_all[:, col:col + 1]
            m = jnp.max(w, axis=0, keepdims=True)
            wk = jnp.exp(w - m) * k[:, h * dh:(h + 1) * dh]
            vt_aug = jnp.concatenate([vt[h * dh:(h + 1) * dh, :], ones], axis=0)
            c_out[dr, h] = _dot(vt_aug, wk.astype(BF16))
            m_out[dr, h] = jnp.broadcast_to(m * LOG2E, (1, LANES))


def _ctx_call(ctx, sh_c, sc_c, g1, wk, wv, wi, wf, cw_k, cb_k, bi, bf):
    b, lc, d = ctx.shape
    dh = d // N_HEADS
    row = lambda w: pl.BlockSpec((1, w), lambda i: (0, 0))
    return pl.pallas_call(
        _ctx_kernel,
        grid=(b,),
        in_specs=[pl.BlockSpec((None, lc, d), lambda i: (i, 0, 0)),
                  row(d), row(d), row(d),
                  pl.BlockSpec((d, d), lambda i: (0, 0)),
                  pl.BlockSpec((d, d), lambda i: (0, 0)),
                  pl.BlockSpec((d, LANES), lambda i: (0, 0)),
                  pl.BlockSpec((d, LANES), lambda i: (0, 0)),
                  pl.BlockSpec((CONV_W, d), lambda i: (0, 0)),
                  row(d), row(LANES), row(LANES)],
        out_specs=[pl.BlockSpec((None, N_DIRS, N_HEADS, dh + AUG_ROWS, dh), lambda i: (i, 0, 0, 0, 0)),
                   pl.BlockSpec((None, N_DIRS, N_HEADS, 1, LANES), lambda i: (i, 0, 0, 0, 0))],
        out_shape=[jax.ShapeDtypeStruct((b, N_DIRS, N_HEADS, dh + AUG_ROWS, dh), F32),
                   jax.ShapeDtypeStruct((b, N_DIRS, N_HEADS, 1, LANES), F32)],
        compiler_params=_params("parallel"),
        name="ctx_states",
    )(ctx, sh_c, sc_c, g1, wk, wv, wi, wf, cw_k, cb_k, bi, bf)


def _cummax_lanes(x, reverse):
    n = x.shape[-1]
    lane = lax.broadcasted_iota(jnp.int32, x.shape, x.ndim - 1)
    k = 1
    while k < n:
        if reverse:
            shifted = jnp.where(lane < n - k, pltpu.roll(x, n - k, axis=x.ndim - 1), -jnp.inf)
        else:
            shifted = jnp.where(lane >= k, pltpu.roll(x, k, axis=x.ndim - 1), -jnp.inf)
        x = jnp.maximum(x, shifted)
        k *= 2
    return x


def _proj_kernel(xp_ref, x_ref, xn_ref, sh_ref, sc_ref, g_ref,
                 wpool_ref, wqk_ref, wvt_ref, wo_ref, wg_ref, wit_ref, wft_ref,
                 cw_ref, cb_ref, bit_ref, bft_ref,
                 u_out, q_out, k_out, vt_out, og_out, gg_out, cq_out, rows_out,
                 hx_scr, r_scr):
    tm, d = x_ref.shape
    dh = d // N_HEADS
    i = pl.program_id(1)
    last = pl.num_programs(1) - 1

    x_ext = jnp.concatenate([xp_ref[...], x_ref[...], xn_ref[...]], axis=0)
    hx = (x_ext * _rms_scale(x_ext) * g_ref[...]) * (1.0 + sc_ref[...]) + sh_ref[...]
    n_ext = tm + 2 * HALO
    r_id = lax.broadcasted_iota(jnp.int32, (n_ext, 1), 0)
    valid = jnp.logical_and(jnp.logical_or(i > 0, r_id >= HALO),
                            jnp.logical_or(i < last, r_id < HALO + tm))
    hx_scr[...] = jnp.where(valid, hx, 0.0).astype(BF16)
    hxc = hx_scr[HALO:HALO + tm, :]

    nc = 512
    half = CONV_W // 2

    def qk_dot(c):
        r_scr[c % 2] = _dot(hx_scr[...], wqk_ref[:, c * nc:(c + 1) * nc])

    def qk_conv(c):
        cols = slice(c * nc, (c + 1) * nc)
        r = r_scr.at[c % 2]
        cw = cw_ref[:, cols]
        acc = cb_ref[:, cols] + cw[0:1, :] * r[HALO - half:HALO - half + tm, :]
        for j in range(1, CONV_W):
            acc = acc + cw[j:j + 1, :] * r[HALO - half + j:HALO - half + j + tm, :]
        y = _silu(acc)
        if c * nc < d:
            q_out[:, cols] = y.astype(BF16)
        else:
            k_out[:, c * nc - d:(c + 1) * nc - d] = (y * (dh ** -0.5)).astype(BF16)

    def v_chunk(c):
        cols = slice(c * nc, (c + 1) * nc)
        vt_out[cols, :] = _dot(wvt_ref[cols, :], hxc, NT).astype(BF16)

    def o_chunk(c):
        cols = slice(c * nc, (c + 1) * nc)
        og_out[:, cols] = jax.nn.sigmoid(_dot(hxc, wo_ref[:, cols])).astype(BF16)

    def g_chunk(c):
        cols = slice(c * nc, (c + 1) * nc)
        gg_out[:, cols] = jax.nn.sigmoid(_dot(hxc, wg_ref[:, cols])).astype(BF16)

    qk_dot(0); qk_dot(1)
    qk_conv(0); v_chunk(0); v_chunk(1); qk_dot(2)
    qk_conv(1); o_chunk(0); o_chunk(1); qk_dot(3)
    qk_conv(2); g_chunk(0); g_chunk(1)
    qk_conv(3); g_chunk(2); g_chunk(3)
    u_out[...] = _dot(hxc, wpool_ref[...]).astype(BF16)

    nq = N_DIRS * N_HEADS
    gi_r = _dot(wit_ref[...], hxc, NT) + bit_ref[...]
    lf_r = _log_sigmoid(_dot(wft_ref[...], hxc, NT) + bft_ref[...])
    fwd_sub = lax.broadcasted_iota(jnp.int32, (nq, CHUNK), 0) < N_HEADS
    t_le, t_ge = _tri01(CHUNK, "le"), _tri01(CHUNK, "ge")
    for j in range(tm // CHUNK):
        rows = slice(j * CHUNK, (j + 1) * CHUNK)
        b_r = jnp.where(fwd_sub, _dot_right01(lf_r[:, rows], t_ge), _dot_right01(lf_r[:, rows], t_le))
        c_r = (gi_r[:, rows] - b_r) * LOG2E
        b_r = b_r * LOG2E
        cq_out[rows, :] = c_r.T
        rows_out[0:nq, rows] = b_r
        rows_out[nq:2 * nq, rows] = jnp.where(fwd_sub, _cummax_lanes(c_r, False), _cummax_lanes(c_r, True))


def _proj_call(x, sh, sc, g1, wpool, wqk, wvt, wo, wg, wit, wft, cw, cb, bit, bft):
    b, t, d = x.shape
    tm = TOKEN_TILE
    nt = t // tm
    hb = tm // HALO
    nq = N_DIRS * N_HEADS
    per_b = pl.BlockSpec((None, 1, d), lambda bi_, i: (bi_, 0, 0))
    tile = lambda w: pl.BlockSpec((None, tm, w), lambda bi_, i: (bi_, i, 0))
    tile_t = lambda h: pl.BlockSpec((None, h, tm), lambda bi_, i: (bi_, 0, i))
    const = lambda a: pl.BlockSpec(a.shape, lambda bi_, i: (0,) * a.ndim)
    out_shapes = [jax.ShapeDtypeStruct((b, t, d // 2), BF16),
                  jax.ShapeDtypeStruct((b, t, d), BF16),
                  jax.ShapeDtypeStruct((b, t, d), BF16),
                  jax.ShapeDtypeStruct((b, d, t), BF16),
                  jax.ShapeDtypeStruct((b, t, d), BF16),
                  jax.ShapeDtypeStruct((b, t, 2 * d), BF16),
                  jax.ShapeDtypeStruct((b, t, nq), F32),
                  jax.ShapeDtypeStruct((b, 2 * nq, t), F32)]
    out_specs = [tile(d // 2), tile(d), tile(d), tile_t(d), tile(d), tile(2 * d),
                 tile(nq), tile_t(2 * nq)]
    return pl.pallas_call(
        _proj_kernel,
        grid=(b, nt),
        in_specs=[pl.BlockSpec((None, HALO, d), lambda bi_, i: (bi_, jnp.maximum(i * hb - 1, 0), 0)),
                  tile(d),
                  pl.BlockSpec((None, HALO, d), lambda bi_, i: (bi_, jnp.minimum((i + 1) * hb, t // HALO - 1), 0)),
                  per_b, per_b, const(g1),
                  const(wpool), const(wqk), const(wvt), const(wo), const(wg),
                  const(wit), const(wft),
                  const(cw), const(cb), const(bit), const(bft)],
        out_specs=out_specs,
        out_shape=out_shapes,
        scratch_shapes=[pltpu.VMEM((tm + 2 * HALO, d), BF16),
                        pltpu.VMEM((2, tm + 2 * HALO, 512), F32)],
        compiler_params=_params("parallel", "parallel"),
        name="proj",
    )(x, x, x, sh, sc, g1, wpool, wqk, wvt, wo, wg, wit, wft, cw, cb, bit, bft)


def _pool_kernel(u_ref, mix_ref, scale_ref, inv_ref, p_out, pad_scr):
    t, pw = u_ref.shape
    gw = pw // len(POOL_WINDOWS)
    tile = 256
    maxlo = max(POOL_WINDOWS) // 2
    padr = maxlo * GRID_W
    ti = lax.broadcasted_iota(jnp.int32, (tile, tile), 0)
    tj = lax.broadcasted_iota(jnp.int32, (tile, tile), 1)
    same_row = _div_pow2(ti, GRID_W) == _div_pow2(tj, GRID_W)
    ci, cj = _mod_pow2(ti, GRID_W), _mod_pow2(tj, GRID_W)

    def span(dlt, ext):
        return pad_scr[padr + (dlt - ext) * GRID_W:padr + (dlt + ext) * GRID_W + t, :]

    for g, side in enumerate(POOL_WINDOWS):
        lo, hi = side // 2, side - side // 2
        assert lo == hi and side & (side - 1) == 0
        cols = slice(g * gw, (g + 1) * gw)
        pad_scr[0:padr, :] = jnp.zeros((padr, gw), F32)
        pad_scr[padr + t:padr + t + padr, :] = jnp.zeros((padr, gw), F32)
        band = jnp.logical_and(same_row, jnp.logical_and(cj >= ci - lo, cj < ci + hi))
        pw01 = jnp.where(band, 1.0, 0.0).astype(BF16)
        for k in range(t // tile):
            rs = slice(k * tile, (k + 1) * tile)
            pad_scr[padr + k * tile:padr + (k + 1) * tile, :] = _dot(pw01, u_ref[rs, cols])
        ext = (side - 2) // 2
        tot = span(-1, ext) + span(0, ext)
        k = 2
        while k < side:
            pad_scr[padr - ext * GRID_W:padr + ext * GRID_W + t, :] = tot
            ext = (side - 2 * k) // 2
            tot = span(-(k // 2), ext) + span(k // 2, ext)
            k *= 2
        a = tot * inv_ref[g] - u_ref[:, cols].astype(F32)
        p = _dot(a.astype(BF16), mix_ref[g]) * scale_ref[:, cols]
        p_out[:, cols] = p.astype(BF16)


def _pool_inv_counts(t, gw):
    rows = t // GRID_W
    r, c = np.arange(t) // GRID_W, np.arange(t) % GRID_W
    out = []
    for side in POOL_WINDOWS:
        lo, hi = side // 2, side - side // 2
        cnt = ((np.minimum(r + hi, rows) - np.maximum(r - lo, 0))
               * (np.minimum(c + hi, GRID_W) - np.maximum(c - lo, 0)))
        out.append(np.broadcast_to((1.0 / cnt).astype(np.float32)[:, None], (t, gw)))
    return jnp.asarray(np.stack(out))


def _pool_call(u, mix, scale):
    b, t, pw = u.shape
    ng = len(POOL_WINDOWS)
    gw = pw // ng
    padr = (max(POOL_WINDOWS) // 2) * GRID_W
    return pl.pallas_call(
        _pool_kernel,
        grid=(b,),
        in_specs=[pl.BlockSpec((None, t, pw), lambda i: (i, 0, 0)),
                  pl.BlockSpec(mix.shape, lambda i: (0, 0, 0)),
                  pl.BlockSpec((1, pw), lambda i: (0, 0)),
                  pl.BlockSpec((ng, t, gw), lambda i: (0, 0, 0))],
        out_specs=pl.BlockSpec((None, t, pw), lambda i: (i, 0, 0)),
        out_shape=jax.ShapeDtypeStruct((b, t, pw), BF16),
        scratch_shapes=[pltpu.VMEM((t + 2 * padr, gw), F32)],
        compiler_params=_params("parallel"),
        name="pool",
    )(u, mix, scale, _pool_inv_counts(t, gw))


def _mlstm_dir(q_ref, k_ref, vt_ref, cq_ref, rows_ref, c_scr, m_scr, reverse):
    L, d = q_ref.shape
    dh = d // N_HEADS
    nhalf = dh // LANES
    nq = N_DIRS * N_HEADS
    si = lax.broadcasted_iota(jnp.int32, (L, L), 0)
    tj = lax.broadcasted_iota(jnp.int32, (L, L), 1)
    mask = (si >= tj) if reverse else (si <= tj)
    ones = jnp.ones((AUG_ROWS, L), BF16)
    end = 0 if reverse else L - 1
    off = N_HEADS if reverse else 0

    hs_all = []
    for h in range(N_HEADS):
        hs = slice(h * dh, (h + 1) * dh)
        st = off + h
        q = q_ref[:, hs]
        k = k_ref[:, hs]
        vt_aug = jnp.concatenate([vt_ref[hs, :], ones], axis=0)
        c_c = cq_ref[:, st:st + 1]
        b_r = rows_ref[st:st + 1, :]
        cm_r = rows_ref[nq + st:nq + st + 1, :]
        m_prev = m_scr[st][:, 0:1]
        ct_prev = c_scr[st]

        mm = jnp.maximum(cm_r, m_prev)
        w_inter = jnp.exp2(m_prev - mm)
        st_mat = (_dot(k, q, NT) * jnp.exp2(jnp.where(mask, c_c - mm, -jnp.inf))).astype(BF16)
        intra = _dot(vt_aug, st_mat)
        inter = _dot(ct_prev.astype(BF16), q, NT)
        den = w_inter * inter[dh:dh + 1, :] + intra[dh:dh + 1, :]
        inv = 1.0 / jnp.maximum(jnp.abs(den), jnp.exp2(-(b_r + mm)))
        hs_all.append((w_inter * inter[0:dh, :] + intra[0:dh, :]) * inv)

        g_tot = b_r[:, end:end + 1]
        m_new = g_tot + jnp.maximum(m_prev, cm_r[:, end:end + 1])
        decay = jnp.exp2(g_tot + m_prev - m_new)
        wk = (k.astype(F32) * jnp.exp2(g_tot + c_c - m_new)).astype(BF16)
        c_scr[st] = decay * ct_prev + _dot(vt_aug, wk)
        m_scr[st] = jnp.broadcast_to(m_new, (1, LANES))
    return hs_all


def _mlstm_kernel(qf_ref, kf_ref, vf_ref, ogf_ref, cqf_ref, rwf_ref,
                  qb_ref, kb_ref, vb_ref, ogb_ref, cqb_ref, rwb_ref,
                  c0_ref, m0_ref, ng_ref, o_ref,
                  c_scr, m_scr, hf_scr, hb_scr):
    L, d = qf_ref.shape
    dh = d // N_HEADS
    nhalf = dh // LANES
    s = pl.program_id(1)
    nch = pl.num_programs(1)
    half = nch // 2

    @pl.when(s == 0)
    def _():
        for j in range(N_DIRS * N_HEADS):
            c_scr[j] = c0_ref[j // N_HEADS, j % N_HEADS]
            m_scr[j] = m0_ref[j // N_HEADS, j % N_HEADS]

    h_f = _mlstm_dir(qf_ref, kf_ref, vf_ref, cqf_ref, rwf_ref, c_scr, m_scr, False)
    h_b = _mlstm_dir(qb_ref, kb_ref, vb_ref, cqb_ref, rwb_ref, c_scr, m_scr, True)

    @pl.when(s < half)
    def _():
        for h in range(N_HEADS):
            hs = slice(h * dh, (h + 1) * dh)
            hf_scr[s, hs, :] = h_f[h]
            hb_scr[half - 1 - s, hs, :] = h_b[h]

    @pl.when(s >= half)
    def _():
        def finish(ht, h, og_ref, out):
            hs = slice(h * dh, (h + 1) * dh)
            scale = lax.rsqrt(jnp.mean(ht * ht, axis=0, keepdims=True) + NORM_EPS)
            y = jnp.concatenate([ht[:, i * LANES:(i + 1) * LANES] * scale[:, i * LANES:(i + 1) * LANES]
                                 * ng_ref[hs, :] for i in range(L // LANES)], axis=1).T
            out[:, hs] = (y * og_ref[:, hs].astype(F32)).astype(BF16)

        for h in range(N_HEADS):
            hs = slice(h * dh, (h + 1) * dh)
            finish(h_f[h] + hb_scr[s - half, hs, :], h, ogf_ref, o_ref.at[1])
            finish(h_b[h] + hf_scr[nch - 1 - s, hs, :], h, ogb_ref, o_ref.at[0])


def _mlstm_call(q, k, vt, og, cq, rows, c0, m0, norm_g):
    b, t, d = q.shape
    dh = d // N_HEADS
    L = CHUNK
    nch = t // L
    half = nch // 2
    assert nch % 2 == 0
    nq = N_DIRS * N_HEADS

    def specs(chunk):
        seq = lambda w: pl.BlockSpec((None, L, w), lambda bi, s: (bi, chunk(s), 0))
        seq_t = lambda h: pl.BlockSpec((None, h, L), lambda bi, s: (bi, 0, chunk(s)))
        return [seq(d), seq(d), seq_t(d), seq(d), seq(nq), seq_t(2 * nq)]

    state = lambda w0, w1: pl.BlockSpec((None, N_DIRS, N_HEADS, w0, w1), lambda bi, s: (bi, 0, 0, 0, 0))
    return pl.pallas_call(
        _mlstm_kernel,
        grid=(b, nch),
        in_specs=specs(lambda s: s) + specs(lambda s: nch - 1 - s) + [
            state(dh + AUG_ROWS, dh), state(1, LANES), pl.BlockSpec((d, LANES), lambda bi, s: (0, 0))],
        out_specs=pl.BlockSpec((None, 2, None, L, d), lambda bi, s: (bi, 0, jnp.maximum(s - half, 0), 0, 0)),
        out_shape=jax.ShapeDtypeStruct((b, 2, half, L, d), BF16),
        scratch_shapes=[pltpu.VMEM((N_DIRS * N_HEADS, dh + AUG_ROWS, dh), F32),
                        pltpu.VMEM((N_DIRS * N_HEADS, 1, LANES), F32),
                        pltpu.VMEM((half, d, L), F32),
                        pltpu.VMEM((half, d, L), F32)],
        compiler_params=_params("parallel", "arbitrary"),
        name="mlstm",
    )(q, k, vt, og, cq, rows, q, k, vt, og, cq, rows, c0, m0, norm_g)


def _merge_kernel(p_ref, m_ref, gg_ref, x_ref, g1_ref, sh2_ref, sc2_ref, n2_ref,
                  wpo_ref, wmo_ref, wout_ref, wr_ref, x1_out, h2_out, aff_out):
    tm, d = x_ref.shape
    cpt = m_ref.shape[0]
    L = m_ref.shape[1]
    upper = pl.program_id(1) >= pl.num_programs(1) // 2
    sub = TOKEN_TILE
    cps = sub // L
    streams = [slice(r * sub, (r + 1) * sub) for r in range(tm // sub)]

    def branches(r):
        m = jnp.where(upper,
                      jnp.concatenate([m_ref[r * cps + j] for j in range(cps)], axis=0),
                      jnp.concatenate([m_ref[cpt - 1 - r * cps - j] for j in range(cps)], axis=0))
        return _dot(p_ref[streams[r], :], wpo_ref[...]), _dot(m, wmo_ref[...])

    def mix(r, a, mm):
        rows = streams[r]
        return (gg_ref[rows, 0:d].astype(F32) * a + gg_ref[rows, d:2 * d].astype(F32) * mm).astype(BF16)

    def residual(r, mixed):
        x1 = x_ref[streams[r], :] + g1_ref[...] * _dot(mixed, wout_ref[...])
        x1_out[streams[r], :] = x1
        return x1

    def tail(r, x1):
        rows = streams[r]
        h2 = (x1 * _rms_scale(x1) * n2_ref[...]) * (1.0 + sc2_ref[...]) + sh2_ref[...]
        h2_out[rows, :] = h2.astype(BF16)
        logits = _dot3(wr_ref[...], h2, NT)
        z = jnp.exp(logits - jnp.max(logits, axis=0, keepdims=True))
        aff_out[:, rows] = z / jnp.sum(z, axis=0, keepdims=True)

    n = len(streams)
    ab = [branches(r) for r in range(n)]
    mixed = [mix(r, *ab[r]) for r in range(n)]
    x1s = [residual(r, mixed[r]) for r in range(n)]
    for r in range(n):
        tail(r, x1s[r])


def _merge_call(p, m, gg, x, gate1, sh2, sc2, n2, wpo, wmo, wout, wr_t):
    b, t, d = x.shape
    tm = min(MERGE_TILE, t // 2)
    e = wr_t.shape[0]
    per_b = pl.BlockSpec((None, 1, d), lambda bi, i: (bi, 0, 0))
    tile = lambda w: pl.BlockSpec((None, tm, w), lambda bi, i: (bi, i, 0))
    const = lambda a: pl.BlockSpec(a.shape, lambda bi, i: (0,) * a.ndim)
    cpt = tm // CHUNK
    nth = t // tm // 2
    m_spec = pl.BlockSpec((None, None, cpt, CHUNK, d),
                          lambda bi, i: (bi, i // nth, jnp.where(i >= nth, i - nth, nth - 1 - i), 0, 0))
    return pl.pallas_call(
        _merge_kernel,
        grid=(b, t // tm),
        in_specs=[tile(d // 2), m_spec, tile(2 * d), tile(d), per_b, per_b, per_b, const(n2),
                  const(wpo), const(wmo), const(wout), const(wr_t)],
        out_specs=[tile(d), tile(d), pl.BlockSpec((None, e, tm), lambda bi, i: (bi, 0, i))],
        out_shape=[jax.ShapeDtypeStruct((b, t, d), F32),
                   jax.ShapeDtypeStruct((b, t, d), BF16),
                   jax.ShapeDtypeStruct((b, e, t), F32)],
        compiler_params=_params("parallel", "parallel"),
        name="merge",
    )(p, m, gg, x, gate1, sh2, sc2, n2, wpo, wmo, wout, wr_t)


def _route_kernel(aff_ref, slot_out, lo_out, *, cap):
    e, t = aff_ref.shape
    aff = aff_ref[...]

    def step(i, thr):
        cand = thr | (jnp.int32(1) << (30 - i))
        cnt = jnp.sum(jnp.where(aff >= pltpu.bitcast(cand, F32), 1.0, 0.0), axis=-1, keepdims=True)
        return jnp.where(cnt >= cap, cand, thr)

    thr = pltpu.bitcast(lax.fori_loop(0, 31, step, jnp.zeros((e, 1), jnp.int32)), F32)
    gt = aff > thr
    eq = aff == thr
    need = cap - jnp.sum(jnp.where(gt, 1.0, 0.0), axis=-1, keepdims=True).astype(jnp.int32)

    seg = 256
    t_ge = _tri01(seg, "ge")

    def prefix_incl(x01):
        outs, carries, carry = [], [], jnp.zeros((e, 1), F32)
        for j in range(t // seg):
            p = _dot(x01[:, j * seg:(j + 1) * seg].astype(BF16), t_ge) + carry
            outs.append(p)
            carry = p[:, seg - 1:seg]
            carries.append(carry)
        return jnp.concatenate(outs, axis=1), carries

    eq_f = jnp.where(eq, 1.0, 0.0)
    tie_rank = (prefix_incl(eq_f)[0] - eq_f).astype(jnp.int32)
    sel = jnp.logical_or(gt, jnp.logical_and(eq, tie_rank < need))
    rank, carries = prefix_incl(jnp.where(sel, 1.0, 0.0))
    slot_out[...] = jnp.where(sel, rank.astype(jnp.int32) - 1, -1)

    lane = lax.broadcasted_iota(jnp.int32, (e, LANES), 1)
    lo = jnp.zeros((e, LANES), F32)
    per_tile = TOKEN_TILE // seg
    for c in range(1, t // TOKEN_TILE + 1):
        lo = jnp.where(lane == c, carries[c * per_tile - 1], lo)
    lo_out[...] = lo.astype(jnp.int32)


def _route_call(aff_t, cap):
    b, e, t = aff_t.shape
    n = b * e
    slot, lo = pl.pallas_call(
        functools.partial(_route_kernel, cap=cap),
        grid=(1,),
        in_specs=[pl.BlockSpec((n, t), lambda i: (0, 0))],
        out_specs=[pl.BlockSpec((n, t), lambda i: (0, 0)),
                   pl.BlockSpec((n, LANES), lambda i: (0, 0))],
        out_shape=[jax.ShapeDtypeStruct((n, t), jnp.int32),
                   jax.ShapeDtypeStruct((n, LANES), jnp.int32)],
        compiler_params=_params("arbitrary"),
        name="route",
    )(aff_t.reshape(n, t))
    return slot.reshape(b, e, t), lo.reshape(b, e, LANES)


SLOT_WINDOW = 96
SLOT_ALIGN = 16
EXPERT_GROUP = 8


def _aligned(lo):
    return jnp.bitwise_and(lo, -SLOT_ALIGN)


def _window_start(nominal, cap):
    return pl.multiple_of(jnp.minimum(nominal, cap - SLOT_WINDOW), SLOT_ALIGN)


def _n_windows(lo, hi):
    return lax.div(hi - _aligned(lo) + (SLOT_WINDOW - 1), SLOT_WINDOW)


def _gather_kernel(lo_ref, h2_ref, slot_ref, xe_out, *, cap, n_tiles):
    n_exp = slot_ref.shape[0]
    tc, w = TOKEN_TILE, SLOT_WINDOW
    b = pl.program_id(0)
    stride = n_tiles + 1
    xe_out[...] = jnp.zeros(xe_out.shape, BF16)
    s_id = lax.broadcasted_iota(jnp.int32, (w, tc), 0)

    def add_rows(e, start, z):
        xe_out[e, pl.ds(start, w), :] = xe_out[e, pl.ds(start, w), :] + z.astype(BF16)

    def tile_body(c, carry):
        t0 = pl.multiple_of(c * tc, tc)
        for g0 in range(0, n_exp, EXPERT_GROUP):
            starts, blocks = [], []
            for e in range(g0, g0 + EXPERT_GROUP):
                a0 = _window_start(_aligned(lo_ref[b, e * stride + c]), cap)
                hit = (s_id + a0) == slot_ref[e:e + 1, pl.ds(t0, tc)]
                blocks.append(jnp.where(hit, 1.0, 0.0).astype(BF16))
                starts.append(a0)
            z = _dot(jnp.concatenate(blocks, axis=0), h2_ref[pl.ds(t0, tc), :])
            for j in range(EXPERT_GROUP):
                add_rows(g0 + j, starts[j], z[j * w:(j + 1) * w, :])
        for e in range(n_exp):
            lo, hi = lo_ref[b, e * stride + c], lo_ref[b, e * stride + c + 1]

            def window_body(k, carry2, e=e, lo=lo):
                nominal = _aligned(lo) + k * w
                a = _window_start(nominal, cap)
                srow = slot_ref[e:e + 1, pl.ds(t0, tc)]
                hit = jnp.logical_and((s_id + a) == srow, srow >= nominal)
                add_rows(e, a, _dot(jnp.where(hit, 1.0, 0.0).astype(BF16), h2_ref[pl.ds(t0, tc), :]))
                return carry2

            lax.fori_loop(1, _n_windows(lo, hi), window_body, 0)
        return carry

    lax.fori_loop(0, n_tiles, tile_body, 0)


def _gather_call(lo2, h2, slot_t, cap):
    b, t, d = h2.shape
    e = slot_t.shape[1]
    n_tiles = t // TOKEN_TILE
    grid_spec = pltpu.PrefetchScalarGridSpec(
        num_scalar_prefetch=1,
        grid=(b,),
        in_specs=[pl.BlockSpec((None, t, d), lambda i, lo: (i, 0, 0)),
                  pl.BlockSpec((None, e, t), lambda i, lo: (i, 0, 0))],
        out_specs=pl.BlockSpec((None, e, cap, d), lambda i, lo: (i, 0, 0, 0)),
    )
    return pl.pallas_call(
        functools.partial(_gather_kernel, cap=cap, n_tiles=n_tiles),
        grid_spec=grid_spec,
        out_shape=jax.ShapeDtypeStruct((b, e, cap, d), BF16),
        compiler_params=_params("arbitrary"),
        name="gather",
    )(lo2, h2, slot_t)


def _expert_kernel(xe_ref, wg_ref, wu_ref, wd_ref, ye_out, wg_scr, wu_scr, wd_scr):
    e, b = pl.program_id(0), pl.program_id(1)
    n_exp = pl.num_programs(0) - 1
    slab = wg_ref.shape[0]
    f = wg_scr.shape[2]
    ns, cap, d = xe_ref.shape

    @pl.when(e < n_exp)
    def _():
        slot = lax.rem(e, 2)
        rows = pl.ds(pl.multiple_of(b * slab, slab), slab)
        wg_scr[slot, rows, :] = wg_ref[...].astype(BF16)
        wu_scr[slot, rows, :] = wu_ref[...].astype(BF16)
        wd_scr[slot, rows, :] = wd_ref[...].astype(BF16)

    @pl.when(e > 0)
    def _():
        slot = lax.rem(e + 1, 2)
        xe = xe_ref[...].reshape(ns * cap, d)
        fc = 512
        y = None
        for c in range(f // fc):
            cols = slice(c * fc, (c + 1) * fc)
            hid = _silu(_dot(xe, wg_scr[slot, :, cols])) * _dot(xe, wu_scr[slot, :, cols])
            part = _dot(hid.astype(BF16), wd_scr[slot, cols, :])
            y = part if y is None else y + part
        ye_out[...] = y.astype(BF16).reshape(ns, cap, d)


def _expert_call(xe, wg, wu, wd):
    b, e, cap, d = xe.shape
    f = wg.shape[2]
    ns = 2 if b % 2 == 0 else 1
    steps = b // ns
    assert d % steps == 0 and f % steps == 0
    w_spec = lambda rows, cols: pl.BlockSpec((None, rows // steps, cols),
                                             lambda ei, bi: (jnp.minimum(ei, e - 1), bi, 0))
    return pl.pallas_call(
        _expert_kernel,
        grid=(e + 1, steps),
        in_specs=[pl.BlockSpec((ns, None, cap, d), lambda ei, bi: (bi, jnp.maximum(ei - 1, 0), 0, 0)),
                  w_spec(d, f), w_spec(d, f), w_spec(f, d)],
        out_specs=pl.BlockSpec((ns, None, cap, d),
                               lambda ei, bi: (jnp.where(ei == 0, 0, bi), jnp.maximum(ei - 1, 0), 0, 0)),
        out_shape=jax.ShapeDtypeStruct((b, e, cap, d), BF16),
        scratch_shapes=[pltpu.VMEM((2, d, f), BF16), pltpu.VMEM((2, d, f), BF16), pltpu.VMEM((2, f, d), BF16)],
        compiler_params=_params("arbitrary", "arbitrary"),
        name="experts",
    )(xe, wg, wu, wd)


def _combine_kernel(lo_ref, ye_ref, slot_ref, aff_ref, x1_ref, g2_ref, fg_ref, o_ref, acc_scr,
                    *, final_norm, n_tiles):
    n_exp, cap, d = ye_ref.shape
    tm, w = x1_ref.shape[0], SLOT_WINDOW
    b, i = pl.program_id(0), pl.program_id(1)
    stride = n_tiles + 1
    s_id = lax.broadcasted_iota(jnp.int32, (w, tm), 0)

    for g0 in range(0, n_exp, EXPERT_GROUP):
        ps, rows = [], []
        for e in range(g0, g0 + EXPERT_GROUP):
            a0 = _window_start(_aligned(lo_ref[b, e * stride + i]), cap)
            hit = (s_id + a0) == slot_ref[e:e + 1, :]
            ps.append(jnp.where(hit, aff_ref[e:e + 1, :], 0.0).astype(BF16))
            rows.append(ye_ref[e, pl.ds(a0, w), :])
        part = _dot(jnp.concatenate(ps, axis=0), jnp.concatenate(rows, axis=0), TN)
        if g0 == 0:
            acc_scr[...] = part
        else:
            acc_scr[...] += part

    for e in range(n_exp):
        lo, hi = lo_ref[b, e * stride + i], lo_ref[b, e * stride + i + 1]

        def window_body(k, carry, e=e, lo=lo):
            nominal = _aligned(lo) + k * w
            a = _window_start(nominal, cap)
            sr = slot_ref[e:e + 1, :]
            hit = jnp.logical_and((s_id + a) == sr, sr >= nominal)
            p = jnp.where(hit, aff_ref[e:e + 1, :], 0.0).astype(BF16)
            acc_scr[...] += _dot(p, ye_ref[e, pl.ds(a, w), :], TN)
            return carry

        lax.fori_loop(1, _n_windows(lo, hi), window_body, 0)

    x2 = x1_ref[...] + g2_ref[...] * acc_scr[...]
    o_ref[...] = x2 * _rms_scale(x2) * fg_ref[...] if final_norm else x2


def _combine_call(lo2, ye, slot_c, aff_c, x1, gate2, final_g, final_norm):
    b, t, d = x1.shape
    e, cap = ye.shape[1], ye.shape[2]
    tm = TOKEN_TILE
    grid_spec = pltpu.PrefetchScalarGridSpec(
        num_scalar_prefetch=1,
        grid=(b, t // tm),
        in_specs=[pl.BlockSpec((None, e, cap, d), lambda bi, i, lo: (bi, 0, 0, 0)),
                  pl.BlockSpec((None, e, tm), lambda bi, i, lo: (bi, 0, i)),
                  pl.BlockSpec((None, e, tm), lambda bi, i, lo: (bi, 0, i)),
                  pl.BlockSpec((None, tm, d), lambda bi, i, lo: (bi, i, 0)),
                  pl.BlockSpec((None, 1, d), lambda bi, i, lo: (bi, 0, 0)),
                  pl.BlockSpec((1, d), lambda bi, i, lo: (0, 0))],
        out_specs=pl.BlockSpec((None, tm, d), lambda bi, i, lo: (bi, i, 0)),
        scratch_shapes=[pltpu.VMEM((tm, d), F32)],
    )
    return pl.pallas_call(
        functools.partial(_combine_kernel, final_norm=final_norm, n_tiles=t // tm),
        grid_spec=grid_spec,
        out_shape=jax.ShapeDtypeStruct((b, t, d), F32),
        compiler_params=_params("parallel", "arbitrary"),
        name="combine",
    )(lo2, ye, slot_c, aff_c, x1, gate2, final_g)


def _layer(x, c, ctx, c_ctx, w_mod, b_mod, norm1_g, norm2_g, w_in, conv_w, conv_b, b_if,
           pool_mix, pool_scale, mlstm_norm_g, w_pool_out, w_mlstm_out, w_out,
           w_router, w_gate, w_up, w_down):
    b, t, d = x.shape
    pw = d // 2
    ng = N_DIRS * 2 * N_HEADS
    q_off, k_off, v_off, o_off = pw, pw + d, pw + 2 * d, pw + 3 * d
    if_off, gate_off = pw + 4 * d, pw + 4 * d + ng
    cap = EC_CAPACITY * t // N_EXPERTS
    row = lambda a: a.reshape(1, -1)

    rows = -(-(b + 1) // 8) * 8
    cvec = jnp.zeros((rows, d), F32).at[:b].set(c).at[b].set(c_ctx)
    mod = _mod_call(cvec, w_mod, row(b_mod))
    shift1, scale1, gate1, shift2, scale2, gate2 = [
        mod[:b, j * d:(j + 1) * d].reshape(b, 1, d) for j in range(6)]
    shift_c, scale_c = mod[b:b + 1, 0:d], mod[b:b + 1, d:2 * d]

    w_in_b = w_in.astype(BF16)
    nq = N_DIRS * N_HEADS
    w_if3 = w_in_b[:, if_off:gate_off].reshape(d, N_DIRS, 2, N_HEADS)
    b_if3 = b_if.reshape(N_DIRS, 2, N_HEADS)
    w_i, w_f = w_if3[:, :, 0, :].reshape(d, nq), w_if3[:, :, 1, :].reshape(d, nq)
    b_i, b_f = b_if3[:, 0, :].reshape(nq), b_if3[:, 1, :].reshape(nq)
    pad_w = lambda w: jnp.zeros((d, LANES), BF16).at[:, :nq].set(w)
    pad_b = lambda v: jnp.zeros((1, LANES), F32).at[0, :nq].set(v)

    w_vt = w_in_b[:, v_off:o_off].T
    c0, m0 = _ctx_call(ctx, shift_c, scale_c, row(norm1_g),
                       w_in_b[:, k_off:v_off], w_vt, pad_w(w_i), pad_w(w_f),
                       conv_w[:, d:], row(conv_b[d:]), pad_b(b_i), pad_b(b_f))

    u, q, k, vt, og, gg, cq, rows = _proj_call(
        x, shift1, scale1, row(norm1_g),
        w_in_b[:, 0:q_off], w_in_b[:, q_off:v_off], w_vt, w_in_b[:, o_off:if_off],
        w_in_b[:, gate_off:], w_i.T, w_f.T,
        conv_w, row(conv_b), b_i.reshape(nq, 1), b_f.reshape(nq, 1))

    p = _pool_call(u, pool_mix.astype(BF16), row(pool_scale))
    m = _mlstm_call(q, k, vt, og, cq, rows, c0, m0, jnp.broadcast_to(mlstm_norm_g[:, None], (d, LANES)))

    x1, h2, aff_t = _merge_call(p, m, gg, x, gate1, shift2, scale2, row(norm2_g),
                                w_pool_out.astype(BF16), w_mlstm_out.astype(BF16), w_out.astype(BF16),
                                w_router.T)
    slot_t, lo = _route_call(aff_t, cap)
    lo2 = lo[:, :, :t // TOKEN_TILE + 1].reshape(b, -1)
    xe = _gather_call(lo2, h2, slot_t, cap)
    ye = _expert_call(xe, w_gate, w_up, w_down)
    return lo2, ye, slot_t, aff_t, x1, gate2


def kernel(x, c, ctx, c_ctx, w_mod, b_mod, norm1_g, norm2_g, w_in, conv_w, conv_b, b_if, pool_mix, pool_scale,
           mlstm_norm_g, w_pool_out, w_mlstm_out, w_out, w_router, w_gate, w_up, w_down, final_g):
    depth = w_mod.shape[0]
    for l in range(depth):
        lo2, ye, slot_c, aff_c, x1, gate2 = _layer(
            x, c, ctx, c_ctx, w_mod[l], b_mod[l], norm1_g[l], norm2_g[l], w_in[l], conv_w[l], conv_b[l],
            b_if[l], pool_mix[l], pool_scale[l], mlstm_norm_g[l], w_pool_out[l], w_mlstm_out[l], w_out[l],
            w_router[l], w_gate[l], w_up[l], w_down[l])
        x = _combine_call(lo2, ye, slot_c, aff_c, x1, gate2, final_g.reshape(1, -1), final_norm=l == depth - 1)
    return x
```

```python
import functools

import jax
import jax.numpy as jnp
import numpy as np
from jax import lax
from jax.experimental import pallas as pl
from jax.experimental.pallas import tpu as pltpu

F32 = jnp.float32
BF16 = jnp.bfloat16

GRID_W = 64
POOL_WINDOWS = (2, 4, 8, 16)
N_HEADS = 4
CONV_W = 5
N_DIRS = 2
N_EXPERTS = 16
EC_CAPACITY = 2
NORM_EPS = 1e-6
LOG2E = 1.4426950408889634

CHUNK = 256
TOKEN_TILE = 512
MERGE_TILE = 1024
HALO = 16
LANES = 128
AUG_ROWS = 16
V7X_VMEM_LIMIT_BYTES = 56 * 1024 * 1024

NN = (((1,), (0,)), ((), ()))
NT = (((1,), (1,)), ((), ()))
TN = (((0,), (0,)), ((), ()))


def _dot(a, b, dims=NN):
    return lax.dot_general(a, b, dims, preferred_element_type=F32)


def _split2(a):
    hi = a.astype(BF16)
    lo = (a - hi.astype(F32)).astype(BF16)
    return hi, lo


def _split3(a):
    a1 = a.astype(BF16)
    r1 = a - a1.astype(F32)
    a2 = r1.astype(BF16)
    a3 = (r1 - a2.astype(F32)).astype(BF16)
    return a1, a2, a3


def _dot3(a, b, dims=NN):
    ah, al = _split2(a)
    bh, bl = _split2(b)
    return _dot(ah, bh, dims) + _dot(ah, bl, dims) + _dot(al, bh, dims)


def _dot_left01(t01, a):
    a1, a2, a3 = _split3(a)
    return _dot(t01, a1) + _dot(t01, a2) + _dot(t01, a3)


def _dot_right01(a, t01):
    a1, a2, a3 = _split3(a)
    return _dot(a1, t01) + _dot(a2, t01) + _dot(a3, t01)


def _silu(x):
    return x * jax.nn.sigmoid(x)


def _log_sigmoid(x):
    return jnp.minimum(x, 0.0) - jnp.log1p(jnp.exp(-jnp.abs(x)))


def _rms_scale(x):
    return lax.rsqrt(jnp.mean(x * x, axis=-1, keepdims=True) + NORM_EPS)


def _tri01(n, kind):
    i = lax.broadcasted_iota(jnp.int32, (n, n), 0)
    j = lax.broadcasted_iota(jnp.int32, (n, n), 1)
    cond = {"le": j <= i, "ge": j >= i, "lt": j < i, "gt": j > i}[kind]
    return jnp.where(cond, 1.0, 0.0).astype(BF16)


def _shift(n):
    assert n & (n - 1) == 0, n
    return n.bit_length() - 1


def _div_pow2(x, n):
    return lax.shift_right_logical(x, _shift(n))


def _mod_pow2(x, n):
    return jnp.bitwise_and(x, n - 1)


def _params(*sem, flags=None):
    return pltpu.CompilerParams(dimension_semantics=sem, vmem_limit_bytes=V7X_VMEM_LIMIT_BYTES, flags=flags)


def _resident(shape):
    nd = len(shape)
    return pl.BlockSpec(shape, lambda *_: (0,) * nd)


def _mod_kernel(c_ref, w_ref, b_ref, o_ref):
    o_ref[...] = _dot3(_silu(c_ref[...]), w_ref[...]) + b_ref[...]


def _mod_call(cvec, w_mod, b_mod):
    rows, d = cvec.shape
    n = w_mod.shape[1]
    tn = 1536
    return pl.pallas_call(
        _mod_kernel,
        grid=(n // tn,),
        in_specs=[pl.BlockSpec((rows, d), lambda j: (0, 0)),
                  pl.BlockSpec((d, tn), lambda j: (0, j)),
                  pl.BlockSpec((1, tn), lambda j: (0, j))],
        out_specs=pl.BlockSpec((rows, tn), lambda j: (0, j)),
        out_shape=jax.ShapeDtypeStruct((rows, n), F32),
        compiler_params=_params("parallel"),
        name="mod",
    )(cvec, w_mod, b_mod)


def _ctx_kernel(ctx_ref, sh_ref, sc_ref, g_ref, wk_ref, wvt_ref, wi_ref, wf_ref, cw_ref, cb_ref, bi_ref, bf_ref,
                c_out, m_out):
    lc, d = ctx_ref.shape
    dh = d // N_HEADS
    x = ctx_ref[...]
    hc = (x * _rms_scale(x) * g_ref[...]) * (1.0 + sc_ref[...]) + sh_ref[...]
    hcb = hc.astype(BF16)

    kpre = _dot(hcb, wk_ref[...])
    pad = jnp.zeros((8, d), F32)
    kp = jnp.concatenate([pad, kpre, pad], axis=0)
    cw = cw_ref[...]
    acc = cb_ref[...] + cw[0:1, :] * kp[6:6 + lc, :]
    for j in range(1, CONV_W):
        acc = acc + cw[j:j + 1, :] * kp[6 + j:6 + j + lc, :]
    k = _silu(acc) * (dh ** -0.5)
    vt = _dot(wvt_ref[...], hcb, NT).astype(BF16)

    gi = _dot(hcb, wi_ref[...]) + bi_ref[...]
    lf = _log_sigmoid(_dot(hcb, wf_ref[...]) + bf_ref[...])
    lane = lax.broadcasted_iota(jnp.int32, lf.shape, 1)
    w_all = gi + jnp.where(lane < N_HEADS, _dot_left01(_tri01(lc, "gt"), lf), _dot_left01(_tri01(lc, "lt"), lf))
    ones = jnp.ones((AUG_ROWS, lc), BF16)
    for dr in range(N_DIRS):
        for h in range(N_HEADS):
            col = dr * N_HEADS + h
            w = w_all[:, col:col + 1]
            m = jnp.max(w, axis=0, keepdims=True)
            wk = jnp.exp(w - m) * k[:, h * dh:(h + 1) * dh]
            vt_aug = jnp.concatenate([vt[h * dh:(h + 1) * dh, :], ones], axis=0)
            c_out[dr, h] = _dot(vt_aug, wk.astype(BF16))
            m_out[dr, h] = jnp.broadcast_to(m * LOG2E, (1, LANES))


def _ctx_call(ctx, sh_c, sc_c, g1, wk, wv, wi, wf, cw_k, cb_k, bi, bf):
    b, lc, d = ctx.shape
    dh = d // N_HEADS
    row = lambda w: pl.BlockSpec((1, w), lambda i: (0, 0))
    return pl.pallas_call(
        _ctx_kernel,
        grid=(b,),
        in_specs=[pl.BlockSpec((None, lc, d), lambda i: (i, 0, 0)),
                  row(d), row(d), row(d),
                  pl.BlockSpec((d, d), lambda i: (0, 0)),
                  pl.BlockSpec((d, d), lambda i: (0, 0)),
                  pl.BlockSpec((d, LANES), lambda i: (0, 0)),
                  pl.BlockSpec((d, LANES), lambda i: (0, 0)),
                  pl.BlockSpec((CONV_W, d), lambda i: (0, 0)),
                  row(d), row(LANES), row(LANES)],
        out_specs=[pl.BlockSpec((None, N_DIRS, N_HEADS, dh + AUG_ROWS, dh), lambda i: (i, 0, 0, 0, 0)),
                   pl.BlockSpec((None, N_DIRS, N_HEADS, 1, LANES), lambda i: (i, 0, 0, 0, 0))],
        out_shape=[jax.ShapeDtypeStruct((b, N_DIRS, N_HEADS, dh + AUG_ROWS, dh), F32),
                   jax.ShapeDtypeStruct((b, N_DIRS, N_HEADS, 1, LANES), F32)],
        compiler_params=_params("parallel"),
        name="ctx_states",
    )(ctx, sh_c, sc_c, g1, wk, wv, wi, wf, cw_k, cb_k, bi, bf)


def _cummax_lanes(x, reverse):
    n = x.shape[-1]
    lane = lax.broadcasted_iota(jnp.int32, x.shape, x.ndim - 1)
    k = 1
    while k < n:
        if reverse:
            shifted = jnp.where(lane < n - k, pltpu.roll(x, n - k, axis=x.ndim - 1), -jnp.inf)
        else:
            shifted = jnp.where(lane >= k, pltpu.roll(x, k, axis=x.ndim - 1), -jnp.inf)
        x = jnp.maximum(x, shifted)
        k *= 2
    return x


def _proj_kernel(xp_ref, x_ref, xn_ref, sh_ref, sc_ref, g_ref,
                 wpool_ref, wqk_ref, wvt_ref, wo_ref, wg_ref, wit_ref, wft_ref,
                 cw_ref, cb_ref, bit_ref, bft_ref,
                 u_out, q_out, k_out, vt_out, og_out, gg_out, cq_out, rows_out,
                 hx_scr, r_scr):
    tm, d = x_ref.shape
    dh = d // N_HEADS
    i = pl.program_id(1)
    last = pl.num_programs(1) - 1

    x_ext = jnp.concatenate([xp_ref[...], x_ref[...], xn_ref[...]], axis=0)
    hx = (x_ext * _rms_scale(x_ext) * g_ref[...]) * (1.0 + sc_ref[...]) + sh_ref[...]
    n_ext = tm + 2 * HALO
    r_id = lax.broadcasted_iota(jnp.int32, (n_ext, 1), 0)
    valid = jnp.logical_and(jnp.logical_or(i > 0, r_id >= HALO),
                            jnp.logical_or(i < last, r_id < HALO + tm))
    hx_scr[...] = jnp.where(valid, hx, 0.0).astype(BF16)
    hxc = hx_scr[HALO:HALO + tm, :]

    nc = 512
    half = CONV_W // 2

    def qk_dot(c):
        r_scr[c % 2] = _dot(hx_scr[...], wqk_ref[:, c * nc:(c + 1) * nc])

    def qk_conv(c):
        cols = slice(c * nc, (c + 1) * nc)
        r = r_scr.at[c % 2]
        cw = cw_ref[:, cols]
        acc = cb_ref[:, cols] + cw[0:1, :] * r[HALO - half:HALO - half + tm, :]
        for j in range(1, CONV_W):
            acc = acc + cw[j:j + 1, :] * r[HALO - half + j:HALO - half + j + tm, :]
        y = _silu(acc)
        if c * nc < d:
            q_out[:, cols] = y.astype(BF16)
        else:
            k_out[:, c * nc - d:(c + 1) * nc - d] = (y * (dh ** -0.5)).astype(BF16)

    def v_chunk(c):
        cols = slice(c * nc, (c + 1) * nc)
        vt_out[cols, :] = _dot(wvt_ref[cols, :], hxc, NT).astype(BF16)

    def o_chunk(c):
        cols = slice(c * nc, (c + 1) * nc)
        og_out[:, cols] = jax.nn.sigmoid(_dot(hxc, wo_ref[:, cols])).astype(BF16)

    def g_chunk(c):
        cols = slice(c * nc, (c + 1) * nc)
        gg_out[:, cols] = jax.nn.sigmoid(_dot(hxc, wg_ref[:, cols])).astype(BF16)

    qk_dot(0); qk_dot(1)
    qk_conv(0); v_chunk(0); v_chunk(1); qk_dot(2)
    qk_conv(1); o_chunk(0); o_chunk(1); qk_dot(3)
    qk_conv(2); g_chunk(0); g_chunk(1)
    qk_conv(3); g_chunk(2); g_chunk(3)
    u_out[...] = _dot(hxc, wpool_ref[...]).astype(BF16)

    nq = N_DIRS * N_HEADS
    gi_r = _dot(wit_ref[...], hxc, NT) + bit_ref[...]
    lf_r = _log_sigmoid(_dot(wft_ref[...], hxc, NT) + bft_ref[...])
    fwd_sub = lax.broadcasted_iota(jnp.int32, (nq, CHUNK), 0) < N_HEADS
    t_le, t_ge = _tri01(CHUNK, "le"), _tri01(CHUNK, "ge")
    for j in range(tm // CHUNK):
        rows = slice(j * CHUNK, (j + 1) * CHUNK)
        b_r = jnp.where(fwd_sub, _dot_right01(lf_r[:, rows], t_ge), _dot_right01(lf_r[:, rows], t_le))
        c_r = (gi_r[:, rows] - b_r) * LOG2E
        b_r = b_r * LOG2E
        cq_out[rows, :] = c_r.T
        rows_out[0:nq, rows] = b_r
        rows_out[nq:2 * nq, rows] = jnp.where(fwd_sub, _cummax_lanes(c_r, False), _cummax_lanes(c_r, True))


def _proj_call(x, sh, sc, g1, wpool, wqk, wvt, wo, wg, wit, wft, cw, cb, bit, bft):
    b, t, d = x.shape
    tm = TOKEN_TILE
    nt = t // tm
    hb = tm // HALO
    nq = N_DIRS * N_HEADS
    per_b = pl.BlockSpec((None, 1, d), lambda bi_, i: (bi_, 0, 0))
    tile = lambda w: pl.BlockSpec((None, tm, w), lambda bi_, i: (bi_, i, 0))
    tile_t = lambda h: pl.BlockSpec((None, h, tm), lambda bi_, i: (bi_, 0, i))
    const = lambda a: pl.BlockSpec(a.shape, lambda bi_, i: (0,) * a.ndim)
    out_shapes = [jax.ShapeDtypeStruct((b, t, d // 2), BF16),
                  jax.ShapeDtypeStruct((b, t, d), BF16),
                  jax.ShapeDtypeStruct((b, t, d), BF16),
                  jax.ShapeDtypeStruct((b, d, t), BF16),
                  jax.ShapeDtypeStruct((b, t, d), BF16),
                  jax.ShapeDtypeStruct((b, t, 2 * d), BF16),
                  jax.ShapeDtypeStruct((b, t, nq), F32),
                  jax.ShapeDtypeStruct((b, 2 * nq, t), F32)]
    out_specs = [tile(d // 2), tile(d), tile(d), tile_t(d), tile(d), tile(2 * d),
                 tile(nq), tile_t(2 * nq)]
    return pl.pallas_call(
        _proj_kernel,
        grid=(b, nt),
        in_specs=[pl.BlockSpec((None, HALO, d), lambda bi_, i: (bi_, jnp.maximum(i * hb - 1, 0), 0)),
                  tile(d),
                  pl.BlockSpec((None, HALO, d), lambda bi_, i: (bi_, jnp.minimum((i + 1) * hb, t // HALO - 1), 0)),
                  per_b, per_b, const(g1),
                  const(wpool), const(wqk), const(wvt), const(wo), const(wg),
                  const(wit), const(wft),
                  const(cw), const(cb), const(bit), const(bft)],
        out_specs=out_specs,
        out_shape=out_shapes,
        scratch_shapes=[pltpu.VMEM((tm + 2 * HALO, d), BF16),
                        pltpu.VMEM((2, tm + 2 * HALO, 512), F32)],
        compiler_params=_params("parallel", "parallel"),
        name="proj",
    )(x, x, x, sh, sc, g1, wpool, wqk, wvt, wo, wg, wit, wft, cw, cb, bit, bft)


def _pool_kernel(u_ref, mix_ref, scale_ref, inv_ref, p_out, pad_scr):
    t, pw = u_ref.shape
    gw = pw // len(POOL_WINDOWS)
    tile = 256
    maxlo = max(POOL_WINDOWS) // 2
    padr = maxlo * GRID_W
    ti = lax.broadcasted_iota(jnp.int32, (tile, tile), 0)
    tj = lax.broadcasted_iota(jnp.int32, (tile, tile), 1)
    same_row = _div_pow2(ti, GRID_W) == _div_pow2(tj, GRID_W)
    ci, cj = _mod_pow2(ti, GRID_W), _mod_pow2(tj, GRID_W)

    def span(dlt, ext):
        return pad_scr[padr + (dlt - ext) * GRID_W:padr + (dlt + ext) * GRID_W + t, :]

    for g, side in enumerate(POOL_WINDOWS):
        lo, hi = side // 2, side - side // 2
        assert lo == hi and side & (side - 1) == 0
        cols = slice(g * gw, (g + 1) * gw)
        pad_scr[0:padr, :] = jnp.zeros((padr, gw), F32)
        pad_scr[padr + t:padr + t + padr, :] = jnp.zeros((padr, gw), F32)
        band = jnp.logical_and(same_row, jnp.logical_and(cj >= ci - lo, cj < ci + hi))
        pw01 = jnp.where(band, 1.0, 0.0).astype(BF16)
        for k in range(t // tile):
            rs = slice(k * tile, (k + 1) * tile)
            pad_scr[padr + k * tile:padr + (k + 1) * tile, :] = _dot(pw01, u_ref[rs, cols])
        ext = (side - 2) // 2
        tot = span(-1, ext) + span(0, ext)
        k = 2
        while k < side:
            pad_scr[padr - ext * GRID_W:padr + ext * GRID_W + t, :] = tot
            ext = (side - 2 * k) // 2
            tot = span(-(k // 2), ext) + span(k // 2, ext)
            k *= 2
        a = tot * inv_ref[g] - u_ref[:, cols].astype(F32)
        p = _dot(a.astype(BF16), mix_ref[g]) * scale_ref[:, cols]
        p_out[:, cols] = p.astype(BF16)


def _pool_inv_counts(t, gw):
    rows = t // GRID_W
    r, c = np.arange(t) // GRID_W, np.arange(t) % GRID_W
    out = []
    for side in POOL_WINDOWS:
        lo, hi = side // 2, side - side // 2
        cnt = ((np.minimum(r + hi, rows) - np.maximum(r - lo, 0))
               * (np.minimum(c + hi, GRID_W) - np.maximum(c - lo, 0)))
        out.append(np.broadcast_to((1.0 / cnt).astype(np.float32)[:, None], (t, gw)))
    return jnp.asarray(np.stack(out))


def _pool_call(u, mix, scale):
    b, t, pw = u.shape
    ng = len(POOL_WINDOWS)
    gw = pw // ng
    padr = (max(POOL_WINDOWS) // 2) * GRID_W
    return pl.pallas_call(
        _pool_kernel,
        grid=(b,),
        in_specs=[pl.BlockSpec((None, t, pw), lambda i: (i, 0, 0)),
                  pl.BlockSpec(mix.shape, lambda i: (0, 0, 0)),
                  pl.BlockSpec((1, pw), lambda i: (0, 0)),
                  pl.BlockSpec((ng, t, gw), lambda i: (0, 0, 0))],
        out_specs=pl.BlockSpec((None, t, pw), lambda i: (i, 0, 0)),
        out_shape=jax.ShapeDtypeStruct((b, t, pw), BF16),
        scratch_shapes=[pltpu.VMEM((t + 2 * padr, gw), F32)],
        compiler_params=_params("parallel"),
        name="pool",
    )(u, mix, scale, _pool_inv_counts(t, gw))


def _mlstm_dir(q_ref, k_ref, vt_ref, cq_ref, rows_ref, c_scr, m_scr, reverse):
    L, d = q_ref.shape
    dh = d // N_HEADS
    nhalf = dh // LANES
    nq = N_DIRS * N_HEADS
    si = lax.broadcasted_iota(jnp.int32, (L, L), 0)
    tj = lax.broadcasted_iota(jnp.int32, (L, L), 1)
    mask = (si >= tj) if reverse else (si <= tj)
    ones = jnp.ones((AUG_ROWS, L), BF16)
    end = 0 if reverse else L - 1
    off = N_HEADS if reverse else 0

    hs_all = []
    for h in range(N_HEADS):
        hs = slice(h * dh, (h + 1) * dh)
        st = off + h
        q = q_ref[:, hs]
        k = k_ref[:, hs]
        vt_aug = jnp.concatenate([vt_ref[hs, :], ones], axis=0)
        c_c = cq_ref[:, st:st + 1]
        b_r = rows_ref[st:st + 1, :]
        cm_r = rows_ref[nq + st:nq + st + 1, :]
        m_prev = m_scr[st][:, 0:1]
        ct_prev = c_scr[st]

        mm = jnp.maximum(cm_r, m_prev)
        w_inter = jnp.exp2(m_prev - mm)
        st_mat = (_dot(k, q, NT) * jnp.exp2(jnp.where(mask, c_c - mm, -jnp.inf))).astype(BF16)
        intra = _dot(vt_aug, st_mat)
        inter = _dot(ct_prev.astype(BF16), q, NT)
        den = w_inter * inter[dh:dh + 1, :] + intra[dh:dh + 1, :]
        inv = 1.0 / jnp.maximum(jnp.abs(den), jnp.exp2(-(b_r + mm)))
        hs_all.append((w_inter * inter[0:dh, :] + intra[0:dh, :]) * inv)

        g_tot = b_r[:, end:end + 1]
        m_new = g_tot + jnp.maximum(m_prev, cm_r[:, end:end + 1])
        decay = jnp.exp2(g_tot + m_prev - m_new)
        wk = (k.astype(F32) * jnp.exp2(g_tot + c_c - m_new)).astype(BF16)
        c_scr[st] = decay * ct_prev + _dot(vt_aug, wk)
        m_scr[st] = jnp.broadcast_to(m_new, (1, LANES))
    return hs_all


def _mlstm_kernel(qf_ref, kf_ref, vf_ref, ogf_ref, cqf_ref, rwf_ref,
                  qb_ref, kb_ref, vb_ref, ogb_ref, cqb_ref, rwb_ref,
                  c0_ref, m0_ref, ng_ref, o_ref,
                  c_scr, m_scr, hf_scr, hb_scr):
    L, d = qf_ref.shape
    dh = d // N_HEADS
    nhalf = dh // LANES
    s = pl.program_id(1)
    nch = pl.num_programs(1)
    half = nch // 2

    @pl.when(s == 0)
    def _():
        for j in range(N_DIRS * N_HEADS):
            c_scr[j] = c0_ref[j // N_HEADS, j % N_HEADS]
            m_scr[j] = m0_ref[j // N_HEADS, j % N_HEADS]

    h_f = _mlstm_dir(qf_ref, kf_ref, vf_ref, cqf_ref, rwf_ref, c_scr, m_scr, False)
    h_b = _mlstm_dir(qb_ref, kb_ref, vb_ref, cqb_ref, rwb_ref, c_scr, m_scr, True)

    @pl.when(s < half)
    def _():
        for h in range(N_HEADS):
            hs = slice(h * dh, (h + 1) * dh)
            hf_scr[s, hs, :] = h_f[h]
            hb_scr[half - 1 - s, hs, :] = h_b[h]

    @pl.when(s >= half)
    def _():
        def finish(ht, h, og_ref, out):
            hs = slice(h * dh, (h + 1) * dh)
            scale = lax.rsqrt(jnp.mean(ht * ht, axis=0, keepdims=True) + NORM_EPS)
            y = jnp.concatenate([ht[:, i * LANES:(i + 1) * LANES] * scale[:, i * LANES:(i + 1) * LANES]
                                 * ng_ref[hs, :] for i in range(L // LANES)], axis=1).T
            out[:, hs] = (y * og_ref[:, hs].astype(F32)).astype(BF16)

        for h in range(N_HEADS):
            hs = slice(h * dh, (h + 1) * dh)
            finish(h_f[h] + hb_scr[s - half, hs, :], h, ogf_ref, o_ref.at[1])
            finish(h_b[h] + hf_scr[nch - 1 - s, hs, :], h, ogb_ref, o_ref.at[0])


def _mlstm_call(q, k, vt, og, cq, rows, c0, m0, norm_g):
    b, t, d = q.shape
    dh = d // N_HEADS
    L = CHUNK
    nch = t // L
    half = nch // 2
    assert nch % 2 == 0
    nq = N_DIRS * N_HEADS

    def specs(chunk):
        seq = lambda w: pl.BlockSpec((None, L, w), lambda bi, s: (bi, chunk(s), 0))
        seq_t = lambda h: pl.BlockSpec((None, h, L), lambda bi, s: (bi, 0, chunk(s)))
        return [seq(d), seq(d), seq_t(d), seq(d), seq(nq), seq_t(2 * nq)]

    state = lambda w0, w1: pl.BlockSpec((None, N_DIRS, N_HEADS, w0, w1), lambda bi, s: (bi, 0, 0, 0, 0))
    return pl.pallas_call(
        _mlstm_kernel,
        grid=(b, nch),
        in_specs=specs(lambda s: s) + specs(lambda s: nch - 1 - s) + [
            state(dh + AUG_ROWS, dh), state(1, LANES), pl.BlockSpec((d, LANES), lambda bi, s: (0, 0))],
        out_specs=pl.BlockSpec((None, 2, None, L, d), lambda bi, s: (bi, 0, jnp.maximum(s - half, 0), 0, 0)),
        out_shape=jax.ShapeDtypeStruct((b, 2, half, L, d), BF16),
        scratch_shapes=[pltpu.VMEM((N_DIRS * N_HEADS, dh + AUG_ROWS, dh), F32),
                        pltpu.VMEM((N_DIRS * N_HEADS, 1, LANES), F32),
                        pltpu.VMEM((half, d, L), F32),
                        pltpu.VMEM((half, d, L), F32)],
        compiler_params=_params("parallel", "arbitrary"),
        name="mlstm",
    )(q, k, vt, og, cq, rows, q, k, vt, og, cq, rows, c0, m0, norm_g)


def _merge_kernel(p_ref, m_ref, gg_ref, x_ref, g1_ref, sh2_ref, sc2_ref, n2_ref,
                  wpo_ref, wmo_ref, wout_ref, wr_ref, x1_out, h2_out, aff_out):
    tm, d = x_ref.shape
    cpt = m_ref.shape[0]
    L = m_ref.shape[1]
    upper = pl.program_id(1) >= pl.num_programs(1) // 2
    sub = TOKEN_TILE
    cps = sub // L
    streams = [slice(r * sub, (r + 1) * sub) for r in range(tm // sub)]

    def branches(r):
        m = jnp.where(upper,
                      jnp.concatenate([m_ref[r * cps + j] for j in range(cps)], axis=0),
                      jnp.concatenate([m_ref[cpt - 1 - r * cps - j] for j in range(cps)], axis=0))
        return _dot(p_ref[streams[r], :], wpo_ref[...]), _dot(m, wmo_ref[...])

    def mix(r, a, mm):
        rows = streams[r]
        return (gg_ref[rows, 0:d].astype(F32) * a + gg_ref[rows, d:2 * d].astype(F32) * mm).astype(BF16)

    def residual(r, mixed):
        x1 = x_ref[streams[r], :] + g1_ref[...] * _dot(mixed, wout_ref[...])
        x1_out[streams[r], :] = x1
        return x1

    def tail(r, x1):
        rows = streams[r]
        h2 = (x1 * _rms_scale(x1) * n2_ref[...]) * (1.0 + sc2_ref[...]) + sh2_ref[...]
        h2_out[rows, :] = h2.astype(BF16)
        logits = _dot3(wr_ref[...], h2, NT)
        z = jnp.exp(logits - jnp.max(logits, axis=0, keepdims=True))
        aff_out[:, rows] = z / jnp.sum(z, axis=0, keepdims=True)

    n = len(streams)
    ab = [branches(r) for r in range(n)]
    mixed = [mix(r, *ab[r]) for r in range(n)]
    x1s = [residual(r, mixed[r]) for r in range(n)]
    for r in range(n):
        tail(r, x1s[r])


def _merge_call(p, m, gg, x, gate1, sh2, sc2, n2, wpo, wmo, wout, wr_t):
    b, t, d = x.shape
    tm = min(MERGE_TILE, t // 2)
    e = wr_t.shape[0]
    per_b = pl.BlockSpec((None, 1, d), lambda bi, i: (bi, 0, 0))
    tile = lambda w: pl.BlockSpec((None, tm, w), lambda bi, i: (bi, i, 0))
    const = lambda a: pl.BlockSpec(a.shape, lambda bi, i: (0,) * a.ndim)
    cpt = tm // CHUNK
    nth = t // tm // 2
    m_spec = pl.BlockSpec((None, None, cpt, CHUNK, d),
                          lambda bi, i: (bi, i // nth, jnp.where(i >= nth, i - nth, nth - 1 - i), 0, 0))
    return pl.pallas_call(
        _merge_kernel,
        grid=(b, t // tm),
        in_specs=[tile(d // 2), m_spec, tile(2 * d), tile(d), per_b, per_b, per_b, const(n2),
                  const(wpo), const(wmo), const(wout), const(wr_t)],
        out_specs=[tile(d), tile(d), pl.BlockSpec((None, e, tm), lambda bi, i: (bi, 0, i))],
        out_shape=[jax.ShapeDtypeStruct((b, t, d), F32),
                   jax.ShapeDtypeStruct((b, t, d), BF16),
                   jax.ShapeDtypeStruct((b, e, t), F32)],
        compiler_params=_params("parallel", "parallel"),
        name="merge",
    )(p, m, gg, x, gate1, sh2, sc2, n2, wpo, wmo, wout, wr_t)


def _route_kernel(aff_ref, slot_out, lo_out, *, cap):
    e, t = aff_ref.shape
    aff = aff_ref[...]

    def step(i, thr):
        cand = thr | (jnp.int32(1) << (30 - i))
        cnt = jnp.sum(jnp.where(aff >= pltpu.bitcast(cand, F32), 1.0, 0.0), axis=-1, keepdims=True)
        return jnp.where(cnt >= cap, cand, thr)

    thr = pltpu.bitcast(lax.fori_loop(0, 31, step, jnp.zeros((e, 1), jnp.int32)), F32)
    gt = aff > thr
    eq = aff == thr
    need = cap - jnp.sum(jnp.where(gt, 1.0, 0.0), axis=-1, keepdims=True).astype(jnp.int32)

    seg = 256
    t_ge = _tri01(seg, "ge")

    def prefix_incl(x01):
        outs, carries, carry = [], [], jnp.zeros((e, 1), F32)
        for j in range(t // seg):
            p = _dot(x01[:, j * seg:(j + 1) * seg].astype(BF16), t_ge) + carry
            outs.append(p)
            carry = p[:, seg - 1:seg]
            carries.append(carry)
        return jnp.concatenate(outs, axis=1), carries

    eq_f = jnp.where(eq, 1.0, 0.0)
    tie_rank = (prefix_incl(eq_f)[0] - eq_f).astype(jnp.int32)
    sel = jnp.logical_or(gt, jnp.logical_and(eq, tie_rank < need))
    rank, carries = prefix_incl(jnp.where(sel, 1.0, 0.0))
    slot_out[...] = jnp.where(sel, rank.astype(jnp.int32) - 1, -1)

    lane = lax.broadcasted_iota(jnp.int32, (e, LANES), 1)
    lo = jnp.zeros((e, LANES), F32)
    per_tile = GATHER_TILE // seg
    for c in range(1, t // GATHER_TILE + 1):
        lo = jnp.where(lane == c, carries[c * per_tile - 1], lo)
    lo_out[...] = lo.astype(jnp.int32)


def _route_call(aff_t, cap):
    b, e, t = aff_t.shape
    n = b * e
    slot, lo = pl.pallas_call(
        functools.partial(_route_kernel, cap=cap),
        grid=(1,),
        in_specs=[pl.BlockSpec((n, t), lambda i: (0, 0))],
        out_specs=[pl.BlockSpec((n, t), lambda i: (0, 0)),
                   pl.BlockSpec((n, LANES), lambda i: (0, 0))],
        out_shape=[jax.ShapeDtypeStruct((n, t), jnp.int32),
                   jax.ShapeDtypeStruct((n, LANES), jnp.int32)],
        compiler_params=_params("arbitrary"),
        name="route",
    )(aff_t.reshape(n, t))
    return slot.reshape(b, e, t), lo.reshape(b, e, LANES)


SLOT_WINDOW = 96
GATHER_TILE = 256
GATHER_WINDOW = 64
SLOT_ALIGN = 16
EXPERT_GROUP = 8


def _aligned(lo):
    return jnp.bitwise_and(lo, -SLOT_ALIGN)


def _window_start(nominal, cap, w):
    return pl.multiple_of(jnp.minimum(nominal, cap - w), SLOT_ALIGN)


def _n_windows(lo, hi, w):
    return lax.div(hi - _aligned(lo) + (w - 1), w)


def _gather_kernel(lo_ref, h2_ref, slot_ref, xe_out, *, cap, n_tiles):
    n_exp = slot_ref.shape[0]
    tc, w = GATHER_TILE, GATHER_WINDOW
    b = pl.program_id(0)
    stride = n_tiles + 1
    xe_out[...] = jnp.zeros(xe_out.shape, BF16)
    s_id = lax.broadcasted_iota(jnp.int32, (w, tc), 0)

    def add_rows(e, start, z):
        xe_out[e, pl.ds(start, w), :] = xe_out[e, pl.ds(start, w), :] + z.astype(BF16)

    def tile_body(c, carry):
        t0 = pl.multiple_of(c * tc, tc)
        for g0 in range(0, n_exp, EXPERT_GROUP):
            starts, blocks = [], []
            for e in range(g0, g0 + EXPERT_GROUP):
                a0 = _window_start(_aligned(lo_ref[b, e * stride + c]), cap, w)
                hit = (s_id + a0) == slot_ref[e:e + 1, pl.ds(t0, tc)]
                blocks.append(jnp.where(hit, 1.0, 0.0).astype(BF16))
                starts.append(a0)
            z = _dot(jnp.concatenate(blocks, axis=0), h2_ref[pl.ds(t0, tc), :])
            for j in range(EXPERT_GROUP):
                add_rows(g0 + j, starts[j], z[j * w:(j + 1) * w, :])
        for e in range(n_exp):
            lo, hi = lo_ref[b, e * stride + c], lo_ref[b, e * stride + c + 1]

            def window_body(k, carry2, e=e, lo=lo):
                nominal = _aligned(lo) + k * w
                a = _window_start(nominal, cap, w)
                srow = slot_ref[e:e + 1, pl.ds(t0, tc)]
                hit = jnp.logical_and((s_id + a) == srow, srow >= nominal)
                add_rows(e, a, _dot(jnp.where(hit, 1.0, 0.0).astype(BF16), h2_ref[pl.ds(t0, tc), :]))
                return carry2

            lax.fori_loop(1, _n_windows(lo, hi, w), window_body, 0)
        return carry

    lax.fori_loop(0, n_tiles, tile_body, 0)


def _gather_call(lo2, h2, slot_t, cap):
    b, t, d = h2.shape
    e = slot_t.shape[1]
    n_tiles = t // GATHER_TILE
    grid_spec = pltpu.PrefetchScalarGridSpec(
        num_scalar_prefetch=1,
        grid=(b,),
        in_specs=[pl.BlockSpec((None, t, d), lambda i, lo: (i, 0, 0)),
                  pl.BlockSpec((None, e, t), lambda i, lo: (i, 0, 0))],
        out_specs=pl.BlockSpec((None, e, cap, d), lambda i, lo: (i, 0, 0, 0)),
    )
    return pl.pallas_call(
        functools.partial(_gather_kernel, cap=cap, n_tiles=n_tiles),
        grid_spec=grid_spec,
        out_shape=jax.ShapeDtypeStruct((b, e, cap, d), BF16),
        compiler_params=_params("arbitrary"),
        name="gather",
    )(lo2, h2, slot_t)


def _expert_kernel(xe_ref, wg_ref, wu_ref, wd_ref, ye_out, wg_scr, wu_scr, wd_scr):
    e, b = pl.program_id(0), pl.program_id(1)
    n_exp = pl.num_programs(0) - 1
    slab = wg_ref.shape[0]
    f = wg_scr.shape[2]
    ns, cap, d = xe_ref.shape

    @pl.when(e < n_exp)
    def _():
        slot = lax.rem(e, 2)
        rows = pl.ds(pl.multiple_of(b * slab, slab), slab)
        wg_scr[slot, rows, :] = wg_ref[...].astype(BF16)
        wu_scr[slot, rows, :] = wu_ref[...].astype(BF16)
        wd_scr[slot, rows, :] = wd_ref[...].astype(BF16)

    @pl.when(e > 0)
    def _():
        slot = lax.rem(e + 1, 2)
        xe = xe_ref[...].reshape(ns * cap, d)
        fc = 512
        y = None
        for c in range(f // fc):
            cols = slice(c * fc, (c + 1) * fc)
            hid = _silu(_dot(xe, wg_scr[slot, :, cols])) * _dot(xe, wu_scr[slot, :, cols])
            part = _dot(hid.astype(BF16), wd_scr[slot, cols, :])
            y = part if y is None else y + part
        ye_out[...] = y.astype(BF16).reshape(ns, cap, d)


def _expert_call(xe, wg, wu, wd):
    b, e, cap, d = xe.shape
    f = wg.shape[2]
    ns = 2 if b % 2 == 0 else 1
    steps = b // ns
    assert d % steps == 0 and f % steps == 0
    w_spec = lambda rows, cols: pl.BlockSpec((None, rows // steps, cols),
                                             lambda ei, bi: (jnp.minimum(ei, e - 1), bi, 0))
    return pl.pallas_call(
        _expert_kernel,
        grid=(e + 1, steps),
        in_specs=[pl.BlockSpec((ns, None, cap, d), lambda ei, bi: (bi, jnp.maximum(ei - 1, 0), 0, 0)),
                  w_spec(d, f), w_spec(d, f), w_spec(f, d)],
        out_specs=pl.BlockSpec((ns, None, cap, d),
                               lambda ei, bi: (jnp.where(ei == 0, 0, bi), jnp.maximum(ei - 1, 0), 0, 0)),
        out_shape=jax.ShapeDtypeStruct((b, e, cap, d), BF16),
        scratch_shapes=[pltpu.VMEM((2, d, f), BF16), pltpu.VMEM((2, d, f), BF16), pltpu.VMEM((2, f, d), BF16)],
        compiler_params=_params("arbitrary", "arbitrary"),
        name="experts",
    )(xe, wg, wu, wd)


def _combine_kernel(lo_ref, ye_ref, slot_ref, aff_ref, x1_ref, g2_ref, fg_ref, o_ref, acc_scr,
                    *, final_norm, n_tiles):
    n_exp, cap, d = ye_ref.shape
    tm, w = x1_ref.shape[0], SLOT_WINDOW
    b, i = pl.program_id(0), pl.program_id(1)
    per = tm // GATHER_TILE
    stride = n_tiles * per + 1
    s_id = lax.broadcasted_iota(jnp.int32, (w, tm), 0)

    for g0 in range(0, n_exp, EXPERT_GROUP):
        ps, rows = [], []
        for e in range(g0, g0 + EXPERT_GROUP):
            a0 = _window_start(_aligned(lo_ref[b, e * stride + i * per]), cap, w)
            hit = (s_id + a0) == slot_ref[e:e + 1, :]
            ps.append(jnp.where(hit, aff_ref[e:e + 1, :], 0.0).astype(BF16))
            rows.append(ye_ref[e, pl.ds(a0, w), :])
        part = _dot(jnp.concatenate(ps, axis=0), jnp.concatenate(rows, axis=0), TN)
        if g0 == 0:
            acc_scr[...] = part
        else:
            acc_scr[...] += part

    for e in range(n_exp):
        lo, hi = lo_ref[b, e * stride + i * per], lo_ref[b, e * stride + (i + 1) * per]

        def window_body(k, carry, e=e, lo=lo):
            nominal = _aligned(lo) + k * w
            a = _window_start(nominal, cap, w)
            sr = slot_ref[e:e + 1, :]
            hit = jnp.logical_and((s_id + a) == sr, sr >= nominal)
            p = jnp.where(hit, aff_ref[e:e + 1, :], 0.0).astype(BF16)
            acc_scr[...] += _dot(p, ye_ref[e, pl.ds(a, w), :], TN)
            return carry

        lax.fori_loop(1, _n_windows(lo, hi, w), window_body, 0)

    x2 = x1_ref[...] + g2_ref[...] * acc_scr[...]
    o_ref[...] = x2 * _rms_scale(x2) * fg_ref[...] if final_norm else x2


def _combine_call(lo2, ye, slot_c, aff_c, x1, gate2, final_g, final_norm):
    b, t, d = x1.shape
    e, cap = ye.shape[1], ye.shape[2]
    tm = TOKEN_TILE
    grid_spec = pltpu.PrefetchScalarGridSpec(
        num_scalar_prefetch=1,
        grid=(b, t // tm),
        in_specs=[pl.BlockSpec((None, e, cap, d), lambda bi, i, lo: (bi, 0, 0, 0)),
                  pl.BlockSpec((None, e, tm), lambda bi, i, lo: (bi, 0, i)),
                  pl.BlockSpec((None, e, tm), lambda bi, i, lo: (bi, 0, i)),
                  pl.BlockSpec((None, tm, d), lambda bi, i, lo: (bi, i, 0)),
                  pl.BlockSpec((None, 1, d), lambda bi, i, lo: (bi, 0, 0)),
                  pl.BlockSpec((1, d), lambda bi, i, lo: (0, 0))],
        out_specs=pl.BlockSpec((None, tm, d), lambda bi, i, lo: (bi, i, 0)),
        scratch_shapes=[pltpu.VMEM((tm, d), F32)],
    )
    return pl.pallas_call(
        functools.partial(_combine_kernel, final_norm=final_norm, n_tiles=t // tm),
        grid_spec=grid_spec,
        out_shape=jax.ShapeDtypeStruct((b, t, d), F32),
        compiler_params=_params("parallel", "arbitrary"),
        name="combine",
    )(lo2, ye, slot_c, aff_c, x1, gate2, final_g)


def _layer(x, c, ctx, c_ctx, w_mod, b_mod, norm1_g, norm2_g, w_in, conv_w, conv_b, b_if,
           pool_mix, pool_scale, mlstm_norm_g, w_pool_out, w_mlstm_out, w_out,
           w_router, w_gate, w_up, w_down):
    b, t, d = x.shape
    pw = d // 2
    ng = N_DIRS * 2 * N_HEADS
    q_off, k_off, v_off, o_off = pw, pw + d, pw + 2 * d, pw + 3 * d
    if_off, gate_off = pw + 4 * d, pw + 4 * d + ng
    cap = EC_CAPACITY * t // N_EXPERTS
    row = lambda a: a.reshape(1, -1)

    rows = -(-(b + 1) // 8) * 8
    cvec = jnp.zeros((rows, d), F32).at[:b].set(c).at[b].set(c_ctx)
    mod = _mod_call(cvec, w_mod, row(b_mod))
    shift1, scale1, gate1, shift2, scale2, gate2 = [
        mod[:b, j * d:(j + 1) * d].reshape(b, 1, d) for j in range(6)]
    shift_c, scale_c = mod[b:b + 1, 0:d], mod[b:b + 1, d:2 * d]

    w_in_b = w_in.astype(BF16)
    nq = N_DIRS * N_HEADS
    w_if3 = w_in_b[:, if_off:gate_off].reshape(d, N_DIRS, 2, N_HEADS)
    b_if3 = b_if.reshape(N_DIRS, 2, N_HEADS)
    w_i, w_f = w_if3[:, :, 0, :].reshape(d, nq), w_if3[:, :, 1, :].reshape(d, nq)
    b_i, b_f = b_if3[:, 0, :].reshape(nq), b_if3[:, 1, :].reshape(nq)
    pad_w = lambda w: jnp.zeros((d, LANES), BF16).at[:, :nq].set(w)
    pad_b = lambda v: jnp.zeros((1, LANES), F32).at[0, :nq].set(v)

    w_vt = w_in_b[:, v_off:o_off].T
    c0, m0 = _ctx_call(ctx, shift_c, scale_c, row(norm1_g),
                       w_in_b[:, k_off:v_off], w_vt, pad_w(w_i), pad_w(w_f),
                       conv_w[:, d:], row(conv_b[d:]), pad_b(b_i), pad_b(b_f))

    u, q, k, vt, og, gg, cq, rows = _proj_call(
        x, shift1, scale1, row(norm1_g),
        w_in_b[:, 0:q_off], w_in_b[:, q_off:v_off], w_vt, w_in_b[:, o_off:if_off],
        w_in_b[:, gate_off:], w_i.T, w_f.T,
        conv_w, row(conv_b), b_i.reshape(nq, 1), b_f.reshape(nq, 1))

    p = _pool_call(u, pool_mix.astype(BF16), row(pool_scale))
    m = _mlstm_call(q, k, vt, og, cq, rows, c0, m0, jnp.broadcast_to(mlstm_norm_g[:, None], (d, LANES)))

    x1, h2, aff_t = _merge_call(p, m, gg, x, gate1, shift2, scale2, row(norm2_g),
                                w_pool_out.astype(BF16), w_mlstm_out.astype(BF16), w_out.astype(BF16),
                                w_router.T)
    slot_t, lo = _route_call(aff_t, cap)
    lo2 = lo[:, :, :t // GATHER_TILE + 1].reshape(b, -1)
    xe = _gather_call(lo2, h2, slot_t, cap)
    ye = _expert_call(xe, w_gate, w_up, w_down)
    return lo2, ye, slot_t, aff_t, x1, gate2


def kernel(x, c, ctx, c_ctx, w_mod, b_mod, norm1_g, norm2_g, w_in, conv_w, conv_b, b_if, pool_mix, pool_scale,
           mlstm_norm_g, w_pool_out, w_mlstm_out, w_out, w_router, w_gate, w_up, w_down, final_g):
    depth = w_mod.shape[0]
    for l in range(depth):
        lo2, ye, slot_c, aff_c, x1, gate2 = _layer(
            x, c, ctx, c_ctx, w_mod[l], b_mod[l], norm1_g[l], norm2_g[l], w_in[l], conv_w[l], conv_b[l],
            b_if[l], pool_mix[l], pool_scale[l], mlstm_norm_g[l], w_pool_out[l], w_mlstm_out[l], w_out[l],
            w_router[l], w_gate[l], w_up[l], w_down[l])
        x = _combine_call(lo2, ye, slot_c, aff_c, x1, gate2, final_g.reshape(1, -1), final_norm=l == depth - 1)
    return x
```

```python
import functools

import jax
import jax.numpy as jnp
import numpy as np
from jax import lax
from jax.experimental import pallas as pl
from jax.experimental.pallas import tpu as pltpu

F32 = jnp.float32
BF16 = jnp.bfloat16

GRID_W = 64
POOL_WINDOWS = (2, 4, 8, 16)
N_HEADS = 4
CONV_W = 5
N_DIRS = 2
N_EXPERTS = 16
EC_CAPACITY = 2
NORM_EPS = 1e-6
LOG2E = 1.4426950408889634

CHUNK = 256
TOKEN_TILE = 512
MERGE_TILE = 1024
HALO = 16
LANES = 128
AUG_ROWS = 16
V7X_VMEM_LIMIT_BYTES = 56 * 1024 * 1024

NN = (((1,), (0,)), ((), ()))
NT = (((1,), (1,)), ((), ()))
TN = (((0,), (0,)), ((), ()))


def _dot(a, b, dims=NN):
    return lax.dot_general(a, b, dims, preferred_element_type=F32)


def _split2(a):
    hi = a.astype(BF16)
    lo = (a - hi.astype(F32)).astype(BF16)
    return hi, lo


def _split3(a):
    a1 = a.astype(BF16)
    r1 = a - a1.astype(F32)
    a2 = r1.astype(BF16)
    a3 = (r1 - a2.astype(F32)).astype(BF16)
    return a1, a2, a3


def _dot3(a, b, dims=NN):
    ah, al = _split2(a)
    bh, bl = _split2(b)
    return _dot(ah, bh, dims) + _dot(ah, bl, dims) + _dot(al, bh, dims)


def _dot_left01(t01, a):
    a1, a2, a3 = _split3(a)
    return _dot(t01, a1) + _dot(t01, a2) + _dot(t01, a3)


def _dot_right01(a, t01):
    a1, a2, a3 = _split3(a)
    return _dot(a1, t01) + _dot(a2, t01) + _dot(a3, t01)


def _silu(x):
    return x * jax.nn.sigmoid(x)


def _log_sigmoid(x):
    return jnp.minimum(x, 0.0) - jnp.log1p(jnp.exp(-jnp.abs(x)))


def _rms_scale(x):
    return lax.rsqrt(jnp.mean(x * x, axis=-1, keepdims=True) + NORM_EPS)


def _tri01(n, kind):
    i = lax.broadcasted_iota(jnp.int32, (n, n), 0)
    j = lax.broadcasted_iota(jnp.int32, (n, n), 1)
    cond = {"le": j <= i, "ge": j >= i, "lt": j < i, "gt": j > i}[kind]
    return jnp.where(cond, 1.0, 0.0).astype(BF16)


def _shift(n):
    assert n & (n - 1) == 0, n
    return n.bit_length() - 1


def _div_pow2(x, n):
    return lax.shift_right_logical(x, _shift(n))


def _mod_pow2(x, n):
    return jnp.bitwise_and(x, n - 1)


def _params(*sem):
    return pltpu.CompilerParams(dimension_semantics=sem, vmem_limit_bytes=V7X_VMEM_LIMIT_BYTES)


def _mod_kernel(c_ref, w_ref, b_ref, o_ref):
    o_ref[...] = _dot3(_silu(c_ref[...]), w_ref[...]) + b_ref[...]


def _mod_call(cvec, w_mod, b_mod):
    rows, d = cvec.shape
    n = w_mod.shape[1]
    tn = 1536
    return pl.pallas_call(
        _mod_kernel,
        grid=(n // tn,),
        in_specs=[pl.BlockSpec((rows, d), lambda j: (0, 0)),
                  pl.BlockSpec((d, tn), lambda j: (0, j)),
                  pl.BlockSpec((1, tn), lambda j: (0, j))],
        out_specs=pl.BlockSpec((rows, tn), lambda j: (0, j)),
        out_shape=jax.ShapeDtypeStruct((rows, n), F32),
        compiler_params=_params("parallel"),
        name="mod",
    )(cvec, w_mod, b_mod)


def _ctx_kernel(ctx_ref, sh_ref, sc_ref, g_ref, wk_ref, wvt_ref, wi_ref, wf_ref, cw_ref, cb_ref, bi_ref, bf_ref,
                c_out, m_out):
    lc, d = ctx_ref.shape
    dh = d // N_HEADS
    x = ctx_ref[...]
    hc = (x * _rms_scale(x) * g_ref[...]) * (1.0 + sc_ref[...]) + sh_ref[...]
    hcb = hc.astype(BF16)

    kpre = _dot(hcb, wk_ref[...])
    pad = jnp.zeros((8, d), F32)
    kp = jnp.concatenate([pad, kpre, pad], axis=0)
    cw = cw_ref[...]
    acc = cb_ref[...] + cw[0:1, :] * kp[6:6 + lc, :]
    for j in range(1, CONV_W):
        acc = acc + cw[j:j + 1, :] * kp[6 + j:6 + j + lc, :]
    k = _silu(acc) * (dh ** -0.5)
    vt = _dot(wvt_ref[...], hcb, NT).astype(BF16)

    gi = _dot(hcb, wi_ref[...]) + bi_ref[...]
    lf = _log_sigmoid(_dot(hcb, wf_ref[...]) + bf_ref[...])
    lane = lax.broadcasted_iota(jnp.int32, lf.shape, 1)
    w_all = gi + jnp.where(lane < N_HEADS, _dot_left01(_tri01(lc, "gt"), lf), _dot_left01(_tri01(lc, "lt"), lf))
    ones = jnp.ones((AUG_ROWS, lc), BF16)
    for dr in range(N_DIRS):
        for h in range(N_HEADS):
            col = dr * N_HEADS + h
            w = w_all[:, col:col + 1]
            m = jnp.max(w, axis=0, keepdims=True)
            wk = jnp.exp(w - m) * k[:, h * dh:(h + 1) * dh]
            vt_aug = jnp.concatenate([vt[h * dh:(h + 1) * dh, :], ones], axis=0)
            c_out[dr, h] = _dot(vt_aug, wk.astype(BF16))
            m_out[dr, h] = jnp.broadcast_to(m * LOG2E, (1, LANES))


def _ctx_call(ctx, sh_c, sc_c, g1, wk, wvt, wi, wf, cw_k, cb_k, bi, bf):
    b, lc, d = ctx.shape
    dh = d // N_HEADS
    row = lambda w: pl.BlockSpec((1, w), lambda i: (0, 0))
    return pl.pallas_call(
        _ctx_kernel,
        grid=(b,),
        in_specs=[pl.BlockSpec((None, lc, d), lambda i: (i, 0, 0)),
                  row(d), row(d), row(d),
                  pl.BlockSpec((d, d), lambda i: (0, 0)),
                  pl.BlockSpec((d, d), lambda i: (0, 0)),
                  pl.BlockSpec((d, LANES), lambda i: (0, 0)),
                  pl.BlockSpec((d, LANES), lambda i: (0, 0)),
                  pl.BlockSpec((CONV_W, d), lambda i: (0, 0)),
                  row(d), row(LANES), row(LANES)],
        out_specs=[pl.BlockSpec((None, N_DIRS, N_HEADS, dh + AUG_ROWS, dh), lambda i: (i, 0, 0, 0, 0)),
                   pl.BlockSpec((None, N_DIRS, N_HEADS, 1, LANES), lambda i: (i, 0, 0, 0, 0))],
        out_shape=[jax.ShapeDtypeStruct((b, N_DIRS, N_HEADS, dh + AUG_ROWS, dh), F32),
                   jax.ShapeDtypeStruct((b, N_DIRS, N_HEADS, 1, LANES), F32)],
        compiler_params=_params("parallel"),
        name="ctx_states",
    )(ctx, sh_c, sc_c, g1, wk, wvt, wi, wf, cw_k, cb_k, bi, bf)


def _cummax_lanes(x, reverse):
    n = x.shape[-1]
    lane = lax.broadcasted_iota(jnp.int32, x.shape, x.ndim - 1)
    k = 1
    while k < n:
        if reverse:
            shifted = jnp.where(lane < n - k, pltpu.roll(x, n - k, axis=x.ndim - 1), -jnp.inf)
        else:
            shifted = jnp.where(lane >= k, pltpu.roll(x, k, axis=x.ndim - 1), -jnp.inf)
        x = jnp.maximum(x, shifted)
        k *= 2
    return x


def _proj_kernel(xp_ref, x_ref, xn_ref, sh_ref, sc_ref, g_ref,
                 wpool_ref, wqk_ref, wvt_ref, wo_ref, wg_ref, wit_ref, wft_ref,
                 cw_ref, cb_ref, bit_ref, bft_ref,
                 u_out, q_out, k_out, vt_out, og_out, gg_out, cq_out, rows_out,
                 hx_scr, r_scr):
    tm, d = x_ref.shape
    dh = d // N_HEADS
    i = pl.program_id(1)
    last = pl.num_programs(1) - 1

    x_ext = jnp.concatenate([xp_ref[...], x_ref[...], xn_ref[...]], axis=0)
    hx = (x_ext * _rms_scale(x_ext) * g_ref[...]) * (1.0 + sc_ref[...]) + sh_ref[...]
    n_ext = tm + 2 * HALO
    r_id = lax.broadcasted_iota(jnp.int32, (n_ext, 1), 0)
    valid = jnp.logical_and(jnp.logical_or(i > 0, r_id >= HALO),
                            jnp.logical_or(i < last, r_id < HALO + tm))
    hx_scr[...] = jnp.where(valid, hx, 0.0).astype(BF16)
    hxc = hx_scr[HALO:HALO + tm, :]

    nc = 512
    half = CONV_W // 2

    def qk_dot(c):
        r_scr[c % 2] = _dot(hx_scr[...], wqk_ref[:, c * nc:(c + 1) * nc])

    def qk_conv(c):
        cols = slice(c * nc, (c + 1) * nc)
        r = r_scr.at[c % 2]
        cw = cw_ref[:, cols]
        acc = cb_ref[:, cols] + cw[0:1, :] * r[HALO - half:HALO - half + tm, :]
        for j in range(1, CONV_W):
            acc = acc + cw[j:j + 1, :] * r[HALO - half + j:HALO - half + j + tm, :]
        y = _silu(acc)
        if c * nc < d:
            q_out[:, cols] = y.astype(BF16)
        else:
            k_out[:, c * nc - d:(c + 1) * nc - d] = (y * (dh ** -0.5)).astype(BF16)

    def v_chunk(c):
        cols = slice(c * nc, (c + 1) * nc)
        vt_out[cols, :] = _dot(wvt_ref[cols, :], hxc, NT).astype(BF16)

    def o_chunk(c):
        cols = slice(c * nc, (c + 1) * nc)
        og_out[:, cols] = jax.nn.sigmoid(_dot(hxc, wo_ref[:, cols])).astype(BF16)

    def g_chunk(c):
        cols = slice(c * nc, (c + 1) * nc)
        gg_out[:, cols] = jax.nn.sigmoid(_dot(hxc, wg_ref[:, cols])).astype(BF16)

    qk_dot(0); qk_dot(1)
    qk_conv(0); v_chunk(0); v_chunk(1); qk_dot(2)
    qk_conv(1); o_chunk(0); o_chunk(1); qk_dot(3)
    qk_conv(2); g_chunk(0); g_chunk(1)
    qk_conv(3); g_chunk(2); g_chunk(3)
    u_out[...] = _dot(hxc, wpool_ref[...]).astype(BF16)

    nq = N_DIRS * N_HEADS
    gi_r = _dot(wit_ref[...], hxc, NT) + bit_ref[...]
    lf_r = _log_sigmoid(_dot(wft_ref[...], hxc, NT) + bft_ref[...])
    fwd_sub = lax.broadcasted_iota(jnp.int32, (nq, CHUNK), 0) < N_HEADS
    t_le, t_ge = _tri01(CHUNK, "le"), _tri01(CHUNK, "ge")
    for j in range(tm // CHUNK):
        rows = slice(j * CHUNK, (j + 1) * CHUNK)
        b_r = jnp.where(fwd_sub, _dot_right01(lf_r[:, rows], t_ge), _dot_right01(lf_r[:, rows], t_le))
        c_r = (gi_r[:, rows] - b_r) * LOG2E
        b_r = b_r * LOG2E
        cq_out[rows, :] = c_r.T
        rows_out[0:nq, rows] = b_r
        rows_out[nq:2 * nq, rows] = jnp.where(fwd_sub, _cummax_lanes(c_r, False), _cummax_lanes(c_r, True))


def _proj_call(x, sh, sc, g1, wpool, wqk, wvt, wo, wg, wit, wft, cw, cb, bit, bft):
    b, t, d = x.shape
    tm = TOKEN_TILE
    nt = t // tm
    hb = tm // HALO
    nq = N_DIRS * N_HEADS
    per_b = pl.BlockSpec((None, 1, d), lambda bi_, i: (bi_, 0, 0))
    tile = lambda w: pl.BlockSpec((None, tm, w), lambda bi_, i: (bi_, i, 0))
    tile_t = lambda h: pl.BlockSpec((None, h, tm), lambda bi_, i: (bi_, 0, i))
    const = lambda a: pl.BlockSpec(a.shape, lambda bi_, i: (0,) * a.ndim)
    out_shapes = [jax.ShapeDtypeStruct((b, t, d // 2), BF16),
                  jax.ShapeDtypeStruct((b, t, d), BF16),
                  jax.ShapeDtypeStruct((b, t, d), BF16),
                  jax.ShapeDtypeStruct((b, d, t), BF16),
                  jax.ShapeDtypeStruct((b, t, d), BF16),
                  jax.ShapeDtypeStruct((b, t, 2 * d), BF16),
                  jax.ShapeDtypeStruct((b, t, nq), F32),
                  jax.ShapeDtypeStruct((b, 2 * nq, t), F32)]
    out_specs = [tile(d // 2), tile(d), tile(d), tile_t(d), tile(d), tile(2 * d),
                 tile(nq), tile_t(2 * nq)]
    return pl.pallas_call(
        _proj_kernel,
        grid=(b, nt),
        in_specs=[pl.BlockSpec((None, HALO, d), lambda bi_, i: (bi_, jnp.maximum(i * hb - 1, 0), 0)),
                  tile(d),
                  pl.BlockSpec((None, HALO, d), lambda bi_, i: (bi_, jnp.minimum((i + 1) * hb, t // HALO - 1), 0)),
                  per_b, per_b, const(g1),
                  const(wpool), const(wqk), const(wvt), const(wo), const(wg),
                  const(wit), const(wft),
                  const(cw), const(cb), const(bit), const(bft)],
        out_specs=out_specs,
        out_shape=out_shapes,
        scratch_shapes=[pltpu.VMEM((tm + 2 * HALO, d), BF16),
                        pltpu.VMEM((2, tm + 2 * HALO, 512), F32)],
        compiler_params=_params("parallel", "parallel"),
        name="proj",
    )(x, x, x, sh, sc, g1, wpool, wqk, wvt, wo, wg, wit, wft, cw, cb, bit, bft)


def _pool_kernel(u_ref, mix_ref, scale_ref, inv_ref, p_out, pad_scr):
    t, pw = u_ref.shape
    gw = pw // len(POOL_WINDOWS)
    tile = 256
    maxlo = max(POOL_WINDOWS) // 2
    padr = maxlo * GRID_W
    ti = lax.broadcasted_iota(jnp.int32, (tile, tile), 0)
    tj = lax.broadcasted_iota(jnp.int32, (tile, tile), 1)
    same_row = _div_pow2(ti, GRID_W) == _div_pow2(tj, GRID_W)
    ci, cj = _mod_pow2(ti, GRID_W), _mod_pow2(tj, GRID_W)

    def span(dlt, ext):
        return pad_scr[padr + (dlt - ext) * GRID_W:padr + (dlt + ext) * GRID_W + t, :]

    for g, side in enumerate(POOL_WINDOWS):
        lo, hi = side // 2, side - side // 2
        assert lo == hi and side & (side - 1) == 0
        cols = slice(g * gw, (g + 1) * gw)
        pad_scr[0:padr, :] = jnp.zeros((padr, gw), F32)
        pad_scr[padr + t:padr + t + padr, :] = jnp.zeros((padr, gw), F32)
        band = jnp.logical_and(same_row, jnp.logical_and(cj >= ci - lo, cj < ci + hi))
        pw01 = jnp.where(band, 1.0, 0.0).astype(BF16)
        for k in range(t // tile):
            rs = slice(k * tile, (k + 1) * tile)
            pad_scr[padr + k * tile:padr + (k + 1) * tile, :] = _dot(pw01, u_ref[rs, cols])
        ext = (side - 2) // 2
        tot = span(-1, ext) + span(0, ext)
        k = 2
        while k < side:
            pad_scr[padr - ext * GRID_W:padr + ext * GRID_W + t, :] = tot
            ext = (side - 2 * k) // 2
            tot = span(-(k // 2), ext) + span(k // 2, ext)
            k *= 2
        a = tot * inv_ref[g] - u_ref[:, cols].astype(F32)
        p = _dot(a.astype(BF16), mix_ref[g]) * scale_ref[:, cols]
        p_out[:, cols] = p.astype(BF16)


def _pool_inv_counts(t, gw):
    rows = t // GRID_W
    r, c = np.arange(t) // GRID_W, np.arange(t) % GRID_W
    out = []
    for side in POOL_WINDOWS:
        lo, hi = side // 2, side - side // 2
        cnt = ((np.minimum(r + hi, rows) - np.maximum(r - lo, 0))
               * (np.minimum(c + hi, GRID_W) - np.maximum(c - lo, 0)))
        out.append(np.broadcast_to((1.0 / cnt).astype(np.float32)[:, None], (t, gw)))
    return jnp.asarray(np.stack(out))


def _pool_call(u, mix, scale):
    b, t, pw = u.shape
    ng = len(POOL_WINDOWS)
    gw = pw // ng
    padr = (max(POOL_WINDOWS) // 2) * GRID_W
    return pl.pallas_call(
        _pool_kernel,
        grid=(b,),
        in_specs=[pl.BlockSpec((None, t, pw), lambda i: (i, 0, 0)),
                  pl.BlockSpec(mix.shape, lambda i: (0, 0, 0)),
                  pl.BlockSpec((1, pw), lambda i: (0, 0)),
                  pl.BlockSpec((ng, t, gw), lambda i: (0, 0, 0))],
        out_specs=pl.BlockSpec((None, t, pw), lambda i: (i, 0, 0)),
        out_shape=jax.ShapeDtypeStruct((b, t, pw), BF16),
        scratch_shapes=[pltpu.VMEM((t + 2 * padr, gw), F32)],
        compiler_params=_params("parallel"),
        name="pool",
    )(u, mix, scale, _pool_inv_counts(t, gw))


def _mlstm_dir(q_ref, k_ref, vt_ref, cq_ref, rows_ref, c_scr, m_scr, reverse):
    L, d = q_ref.shape
    dh = d // N_HEADS
    nhalf = dh // LANES
    nq = N_DIRS * N_HEADS
    si = lax.broadcasted_iota(jnp.int32, (L, L), 0)
    tj = lax.broadcasted_iota(jnp.int32, (L, L), 1)
    mask = (si >= tj) if reverse else (si <= tj)
    ones = jnp.ones((AUG_ROWS, L), BF16)
    end = 0 if reverse else L - 1
    off = N_HEADS if reverse else 0

    hs_all = []
    for h in range(N_HEADS):
        hs = slice(h * dh, (h + 1) * dh)
        st = off + h
        q = q_ref[:, hs]
        k = k_ref[:, hs]
        vt_aug = jnp.concatenate([vt_ref[hs, :], ones], axis=0)
        c_c = cq_ref[:, st:st + 1]
        b_r = rows_ref[st:st + 1, :]
        cm_r = rows_ref[nq + st:nq + st + 1, :]
        m_prev = m_scr[st][:, 0:1]
        ct_prev = c_scr[st]

        mm = jnp.maximum(cm_r, m_prev)
        w_inter = jnp.exp2(m_prev - mm)
        st_mat = (_dot(k, q, NT) * jnp.exp2(jnp.where(mask, c_c - mm, -jnp.inf))).astype(BF16)
        intra = _dot(vt_aug, st_mat)
        inter = _dot(ct_prev.astype(BF16), q, NT)
        den = w_inter * inter[dh:dh + 1, :] + intra[dh:dh + 1, :]
        inv = 1.0 / jnp.maximum(jnp.abs(den), jnp.exp2(-(b_r + mm)))
        hs_all.append((w_inter * inter[0:dh, :] + intra[0:dh, :]) * inv)

        g_tot = b_r[:, end:end + 1]
        m_new = g_tot + jnp.maximum(m_prev, cm_r[:, end:end + 1])
        decay = jnp.exp2(g_tot + m_prev - m_new)
        wk = (k.astype(F32) * jnp.exp2(g_tot + c_c - m_new)).astype(BF16)
        c_scr[st] = decay * ct_prev + _dot(vt_aug, wk)
        m_scr[st] = jnp.broadcast_to(m_new, (1, LANES))
    return hs_all


def _mlstm_kernel(qf_ref, kf_ref, vf_ref, ogf_ref, cqf_ref, rwf_ref,
                  qb_ref, kb_ref, vb_ref, ogb_ref, cqb_ref, rwb_ref,
                  c0_ref, m0_ref, ng_ref, o_ref,
                  c_scr, m_scr, hf_scr, hb_scr):
    L, d = qf_ref.shape
    dh = d // N_HEADS
    nhalf = dh // LANES
    s = pl.program_id(1)
    nch = pl.num_programs(1)
    half = nch // 2

    @pl.when(s == 0)
    def _():
        for j in range(N_DIRS * N_HEADS):
            c_scr[j] = c0_ref[j // N_HEADS, j % N_HEADS]
            m_scr[j] = m0_ref[j // N_HEADS, j % N_HEADS]

    h_f = _mlstm_dir(qf_ref, kf_ref, vf_ref, cqf_ref, rwf_ref, c_scr, m_scr, False)
    h_b = _mlstm_dir(qb_ref, kb_ref, vb_ref, cqb_ref, rwb_ref, c_scr, m_scr, True)

    @pl.when(s < half)
    def _():
        for h in range(N_HEADS):
            hs = slice(h * dh, (h + 1) * dh)
            hf_scr[s, hs, :] = h_f[h]
            hb_scr[half - 1 - s, hs, :] = h_b[h]

    @pl.when(s >= half)
    def _():
        def finish(ht, h, og_ref, out):
            hs = slice(h * dh, (h + 1) * dh)
            scale = lax.rsqrt(jnp.mean(ht * ht, axis=0, keepdims=True) + NORM_EPS)
            y = jnp.concatenate([ht[:, i * LANES:(i + 1) * LANES] * scale[:, i * LANES:(i + 1) * LANES]
                                 * ng_ref[hs, :] for i in range(L // LANES)], axis=1).T
            out[:, hs] = (y * og_ref[:, hs].astype(F32)).astype(BF16)

        for h in range(N_HEADS):
            hs = slice(h * dh, (h + 1) * dh)
            finish(h_f[h] + hb_scr[s - half, hs, :], h, ogf_ref, o_ref.at[1])
            finish(h_b[h] + hf_scr[nch - 1 - s, hs, :], h, ogb_ref, o_ref.at[0])


def _mlstm_call(q, k, vt, og, cq, rows, c0, m0, norm_g):
    b, t, d = q.shape
    dh = d // N_HEADS
    L = CHUNK
    nch = t // L
    half = nch // 2
    assert nch % 2 == 0
    nq = N_DIRS * N_HEADS

    def specs(chunk):
        seq = lambda w: pl.BlockSpec((None, L, w), lambda bi, s: (bi, chunk(s), 0))
        seq_t = lambda h: pl.BlockSpec((None, h, L), lambda bi, s: (bi, 0, chunk(s)))
        return [seq(d), seq(d), seq_t(d), seq(d), seq(nq), seq_t(2 * nq)]

    state = lambda w0, w1: pl.BlockSpec((None, N_DIRS, N_HEADS, w0, w1), lambda bi, s: (bi, 0, 0, 0, 0))
    return pl.pallas_call(
        _mlstm_kernel,
        grid=(b, nch),
        in_specs=specs(lambda s: s) + specs(lambda s: nch - 1 - s) + [
            state(dh + AUG_ROWS, dh), state(1, LANES), pl.BlockSpec((d, LANES), lambda bi, s: (0, 0))],
        out_specs=pl.BlockSpec((None, 2, None, L, d), lambda bi, s: (bi, 0, jnp.maximum(s - half, 0), 0, 0)),
        out_shape=jax.ShapeDtypeStruct((b, 2, half, L, d), BF16),
        scratch_shapes=[pltpu.VMEM((N_DIRS * N_HEADS, dh + AUG_ROWS, dh), F32),
                        pltpu.VMEM((N_DIRS * N_HEADS, 1, LANES), F32),
                        pltpu.VMEM((half, d, L), F32),
                        pltpu.VMEM((half, d, L), F32)],
        compiler_params=_params("parallel", "arbitrary"),
        name="mlstm",
    )(q, k, vt, og, cq, rows, q, k, vt, og, cq, rows, c0, m0, norm_g)


def _merge_kernel(p_ref, m_ref, gg_ref, x_ref, g1_ref, sh2_ref, sc2_ref, n2_ref,
                  wpo_ref, wmo_ref, wout_ref, wr_ref, x1_out, h2_out, aff_out):
    tm, d = x_ref.shape
    cpt = m_ref.shape[0]
    L = m_ref.shape[1]
    upper = pl.program_id(1) >= pl.num_programs(1) // 2
    sub = TOKEN_TILE
    cps = sub // L
    streams = [slice(r * sub, (r + 1) * sub) for r in range(tm // sub)]

    def branches(r):
        m = jnp.where(upper,
                      jnp.concatenate([m_ref[r * cps + j] for j in range(cps)], axis=0),
                      jnp.concatenate([m_ref[cpt - 1 - r * cps - j] for j in range(cps)], axis=0))
        return _dot(p_ref[streams[r], :], wpo_ref[...]), _dot(m, wmo_ref[...])

    def mix(r, a, mm):
        rows = streams[r]
        return (gg_ref[rows, 0:d].astype(F32) * a + gg_ref[rows, d:2 * d].astype(F32) * mm).astype(BF16)

    def residual(r, mixed):
        x1 = x_ref[streams[r], :] + g1_ref[...] * _dot(mixed, wout_ref[...])
        x1_out[streams[r], :] = x1
        return x1

    def tail(r, x1):
        rows = streams[r]
        h2 = (x1 * _rms_scale(x1) * n2_ref[...]) * (1.0 + sc2_ref[...]) + sh2_ref[...]
        h2_out[rows, :] = h2.astype(BF16)
        logits = _dot3(wr_ref[...], h2, NT)
        z = jnp.exp(logits - jnp.max(logits, axis=0, keepdims=True))
        aff_out[:, rows] = z / jnp.sum(z, axis=0, keepdims=True)

    n = len(streams)
    ab = [branches(r) for r in range(n)]
    mixed = [mix(r, *ab[r]) for r in range(n)]
    x1s = [residual(r, mixed[r]) for r in range(n)]
    for r in range(n):
        tail(r, x1s[r])


def _merge_call(p, m, gg, x, gate1, sh2, sc2, n2, wpo, wmo, wout, wr_t):
    b, t, d = x.shape
    tm = min(MERGE_TILE, t // 2)
    e = wr_t.shape[0]
    per_b = pl.BlockSpec((None, 1, d), lambda bi, i: (bi, 0, 0))
    tile = lambda w: pl.BlockSpec((None, tm, w), lambda bi, i: (bi, i, 0))
    const = lambda a: pl.BlockSpec(a.shape, lambda bi, i: (0,) * a.ndim)
    cpt = tm // CHUNK
    nth = t // tm // 2
    m_spec = pl.BlockSpec((None, None, cpt, CHUNK, d),
                          lambda bi, i: (bi, i // nth, jnp.where(i >= nth, i - nth, nth - 1 - i), 0, 0))
    return pl.pallas_call(
        _merge_kernel,
        grid=(b, t // tm),
        in_specs=[tile(d // 2), m_spec, tile(2 * d), tile(d), per_b, per_b, per_b, const(n2),
                  const(wpo), const(wmo), const(wout), const(wr_t)],
        out_specs=[tile(d), tile(d), pl.BlockSpec((None, e, tm), lambda bi, i: (bi, 0, i))],
        out_shape=[jax.ShapeDtypeStruct((b, t, d), F32),
                   jax.ShapeDtypeStruct((b, t, d), BF16),
                   jax.ShapeDtypeStruct((b, e, t), F32)],
        compiler_params=_params("parallel", "parallel"),
        name="merge",
    )(p, m, gg, x, gate1, sh2, sc2, n2, wpo, wmo, wout, wr_t)


def _route_kernel(aff_ref, slot_out, lo_out, *, cap):
    e, t = aff_ref.shape
    aff = aff_ref[...]

    def step(i, thr):
        cand = thr | (jnp.int32(1) << (30 - i))
        cnt = jnp.sum(jnp.where(aff >= pltpu.bitcast(cand, F32), 1.0, 0.0), axis=-1, keepdims=True)
        return jnp.where(cnt >= cap, cand, thr)

    thr = pltpu.bitcast(lax.fori_loop(0, 31, step, jnp.zeros((e, 1), jnp.int32)), F32)
    gt = aff > thr
    eq = aff == thr
    need = cap - jnp.sum(jnp.where(gt, 1.0, 0.0), axis=-1, keepdims=True).astype(jnp.int32)

    seg = 256
    t_ge = _tri01(seg, "ge")

    def prefix_incl(x01):
        outs, carries, carry = [], [], jnp.zeros((e, 1), F32)
        for j in range(t // seg):
            p = _dot(x01[:, j * seg:(j + 1) * seg].astype(BF16), t_ge) + carry
            outs.append(p)
            carry = p[:, seg - 1:seg]
            carries.append(carry)
        return jnp.concatenate(outs, axis=1), carries

    eq_f = jnp.where(eq, 1.0, 0.0)
    tie_rank = (prefix_incl(eq_f)[0] - eq_f).astype(jnp.int32)
    sel = jnp.logical_or(gt, jnp.logical_and(eq, tie_rank < need))
    rank, carries = prefix_incl(jnp.where(sel, 1.0, 0.0))
    slot_out[...] = jnp.where(sel, rank.astype(jnp.int32) - 1, -1)

    lane = lax.broadcasted_iota(jnp.int32, (e, LANES), 1)
    lo = jnp.zeros((e, LANES), F32)
    per_tile = GATHER_TILE // seg
    for c in range(1, t // GATHER_TILE + 1):
        lo = jnp.where(lane == c, carries[c * per_tile - 1], lo)
    lo_out[...] = lo.astype(jnp.int32)


def _route_call(aff_t, cap):
    b, e, t = aff_t.shape
    n = b * e
    slot, lo = pl.pallas_call(
        functools.partial(_route_kernel, cap=cap),
        grid=(1,),
        in_specs=[pl.BlockSpec((n, t), lambda i: (0, 0))],
        out_specs=[pl.BlockSpec((n, t), lambda i: (0, 0)),
                   pl.BlockSpec((n, LANES), lambda i: (0, 0))],
        out_shape=[jax.ShapeDtypeStruct((n, t), jnp.int32),
                   jax.ShapeDtypeStruct((n, LANES), jnp.int32)],
        compiler_params=_params("arbitrary"),
        name="route",
    )(aff_t.reshape(n, t))
    return slot.reshape(b, e, t), lo.reshape(b, e, LANES)


SLOT_WINDOW = 96
GATHER_TILE = 256
GATHER_WINDOW = 64
SLOT_ALIGN = 16
EXPERT_GROUP = 8


def _aligned(lo):
    return jnp.bitwise_and(lo, -SLOT_ALIGN)


def _window_start(nominal, cap, w):
    return pl.multiple_of(jnp.minimum(nominal, cap - w), SLOT_ALIGN)


def _n_windows(lo, hi, w):
    return lax.div(hi - _aligned(lo) + (w - 1), w)


def _gather_kernel(lo_ref, h2_ref, slot_ref, xe_out, *, cap, n_tiles):
    n_exp = slot_ref.shape[0]
    tc, w = GATHER_TILE, GATHER_WINDOW
    b = pl.program_id(0)
    stride = n_tiles + 1
    xe_out[...] = jnp.zeros(xe_out.shape, BF16)
    s_id = lax.broadcasted_iota(jnp.int32, (w, tc), 0)

    def add_rows(e, start, z):
        xe_out[e, pl.ds(start, w), :] = xe_out[e, pl.ds(start, w), :] + z.astype(BF16)

    def tile_body(c, carry):
        t0 = pl.multiple_of(c * tc, tc)
        for g0 in range(0, n_exp, EXPERT_GROUP):
            starts, blocks = [], []
            for e in range(g0, g0 + EXPERT_GROUP):
                a0 = _window_start(_aligned(lo_ref[b, e * stride + c]), cap, w)
                hit = (s_id + a0) == slot_ref[e:e + 1, pl.ds(t0, tc)]
                blocks.append(jnp.where(hit, 1.0, 0.0).astype(BF16))
                starts.append(a0)
            z = _dot(jnp.concatenate(blocks, axis=0), h2_ref[pl.ds(t0, tc), :])
            for j in range(EXPERT_GROUP):
                add_rows(g0 + j, starts[j], z[j * w:(j + 1) * w, :])
        for e in range(n_exp):
            lo, hi = lo_ref[b, e * stride + c], lo_ref[b, e * stride + c + 1]

            def window_body(k, carry2, e=e, lo=lo):
                nominal = _aligned(lo) + k * w
                a = _window_start(nominal, cap, w)
                srow = slot_ref[e:e + 1, pl.ds(t0, tc)]
                hit = jnp.logical_and((s_id + a) == srow, srow >= nominal)
                add_rows(e, a, _dot(jnp.where(hit, 1.0, 0.0).astype(BF16), h2_ref[pl.ds(t0, tc), :]))
                return carry2

            lax.fori_loop(1, _n_windows(lo, hi, w), window_body, 0)
        return carry

    lax.fori_loop(0, n_tiles, tile_body, 0)


def _gather_call(lo2, h2, slot_t, cap):
    b, t, d = h2.shape
    e = slot_t.shape[1]
    n_tiles = t // GATHER_TILE
    grid_spec = pltpu.PrefetchScalarGridSpec(
        num_scalar_prefetch=1,
        grid=(b,),
        in_specs=[pl.BlockSpec((None, t, d), lambda i, lo: (i, 0, 0)),
                  pl.BlockSpec((None, e, t), lambda i, lo: (i, 0, 0))],
        out_specs=pl.BlockSpec((None, e, cap, d), lambda i, lo: (i, 0, 0, 0)),
    )
    return pl.pallas_call(
        functools.partial(_gather_kernel, cap=cap, n_tiles=n_tiles),
        grid_spec=grid_spec,
        out_shape=jax.ShapeDtypeStruct((b, e, cap, d), BF16),
        compiler_params=_params("arbitrary"),
        name="gather",
    )(lo2, h2, slot_t)


def _expert_kernel(xe_ref, wg_ref, wu_ref, wd_ref, ye_out, wg_scr, wu_scr, wd_scr):
    e, b = pl.program_id(0), pl.program_id(1)
    n_exp = pl.num_programs(0) - 1
    slab = wg_ref.shape[0]
    f = wg_scr.shape[2]
    ns, cap, d = xe_ref.shape

    @pl.when(e < n_exp)
    def _():
        slot = lax.rem(e, 2)
        rows = pl.ds(pl.multiple_of(b * slab, slab), slab)
        wg_scr[slot, rows, :] = wg_ref[...].astype(BF16)
        wu_scr[slot, rows, :] = wu_ref[...].astype(BF16)
        wd_scr[slot, rows, :] = wd_ref[...].astype(BF16)

    @pl.when(e > 0)
    def _():
        slot = lax.rem(e + 1, 2)
        xe = xe_ref[...].reshape(ns * cap, d)
        fc = 512
        y = None
        for c in range(f // fc):
            cols = slice(c * fc, (c + 1) * fc)
            hid = _silu(_dot(xe, wg_scr[slot, :, cols])) * _dot(xe, wu_scr[slot, :, cols])
            part = _dot(hid.astype(BF16), wd_scr[slot, cols, :])
            y = part if y is None else y + part
        ye_out[...] = y.astype(BF16).reshape(ns, cap, d)


def _expert_call(xe, wg, wu, wd):
    b, e, cap, d = xe.shape
    f = wg.shape[2]
    ns = 2 if b % 2 == 0 else 1
    steps = b // ns
    assert d % steps == 0 and f % steps == 0
    w_spec = lambda rows, cols: pl.BlockSpec((None, rows // steps, cols),
                                             lambda ei, bi: (jnp.minimum(ei, e - 1), bi, 0))
    return pl.pallas_call(
        _expert_kernel,
        grid=(e + 1, steps),
        in_specs=[pl.BlockSpec((ns, None, cap, d), lambda ei, bi: (bi, jnp.maximum(ei - 1, 0), 0, 0)),
                  w_spec(d, f), w_spec(d, f), w_spec(f, d)],
        out_specs=pl.BlockSpec((ns, None, cap, d),
                               lambda ei, bi: (jnp.where(ei == 0, 0, bi), jnp.maximum(ei - 1, 0), 0, 0)),
        out_shape=jax.ShapeDtypeStruct((b, e, cap, d), BF16),
        scratch_shapes=[pltpu.VMEM((2, d, f), BF16), pltpu.VMEM((2, d, f), BF16), pltpu.VMEM((2, f, d), BF16)],
        compiler_params=_params("arbitrary", "arbitrary"),
        name="experts",
    )(xe, wg, wu, wd)


def _combine_kernel(lo_ref, ye_ref, slot_ref, aff_ref, x1_ref, g2_ref, fg_ref, o_ref, acc_scr,
                    *, final_norm, n_tiles):
    n_exp, cap, d = ye_ref.shape
    tm, w = x1_ref.shape[0], SLOT_WINDOW
    b, i = pl.program_id(0), pl.program_id(1)
    per = tm // GATHER_TILE
    stride = n_tiles * per + 1
    s_id = lax.broadcasted_iota(jnp.int32, (w, tm), 0)

    for g0 in range(0, n_exp, EXPERT_GROUP):
        ps, rows = [], []
        for e in range(g0, g0 + EXPERT_GROUP):
            a0 = _window_start(_aligned(lo_ref[b, e * stride + i * per]), cap, w)
            hit = (s_id + a0) == slot_ref[e:e + 1, :]
            ps.append(jnp.where(hit, aff_ref[e:e + 1, :], 0.0).astype(BF16))
            rows.append(ye_ref[e, pl.ds(a0, w), :])
        part = _dot(jnp.concatenate(ps, axis=0), jnp.concatenate(rows, axis=0), TN)
        if g0 == 0:
            acc_scr[...] = part
        else:
            acc_scr[...] += part

    for e in range(n_exp):
        lo, hi = lo_ref[b, e * stride + i * per], lo_ref[b, e * stride + (i + 1) * per]

        def window_body(k, carry, e=e, lo=lo):
            nominal = _aligned(lo) + k * w
            a = _window_start(nominal, cap, w)
            sr = slot_ref[e:e + 1, :]
            hit = jnp.logical_and((s_id + a) == sr, sr >= nominal)
            p = jnp.where(hit, aff_ref[e:e + 1, :], 0.0).astype(BF16)
            acc_scr[...] += _dot(p, ye_ref[e, pl.ds(a, w), :], TN)
            return carry

        lax.fori_loop(1, _n_windows(lo, hi, w), window_body, 0)

    x2 = x1_ref[...] + g2_ref[...] * acc_scr[...]
    o_ref[...] = x2 * _rms_scale(x2) * fg_ref[...] if final_norm else x2


def _combine_call(lo2, ye, slot_t, aff_t, x1, gate2, final_g, final_norm):
    b, t, d = x1.shape
    e, cap = ye.shape[1], ye.shape[2]
    tm = TOKEN_TILE
    grid_spec = pltpu.PrefetchScalarGridSpec(
        num_scalar_prefetch=1,
        grid=(b, t // tm),
        in_specs=[pl.BlockSpec((None, e, cap, d), lambda bi, i, lo: (bi, 0, 0, 0)),
                  pl.BlockSpec((None, e, tm), lambda bi, i, lo: (bi, 0, i)),
                  pl.BlockSpec((None, e, tm), lambda bi, i, lo: (bi, 0, i)),
                  pl.BlockSpec((None, tm, d), lambda bi, i, lo: (bi, i, 0)),
                  pl.BlockSpec((None, 1, d), lambda bi, i, lo: (bi, 0, 0)),
                  pl.BlockSpec((1, d), lambda bi, i, lo: (0, 0))],
        out_specs=pl.BlockSpec((None, tm, d), lambda bi, i, lo: (bi, i, 0)),
        scratch_shapes=[pltpu.VMEM((tm, d), F32)],
    )
    return pl.pallas_call(
        functools.partial(_combine_kernel, final_norm=final_norm, n_tiles=t // tm),
        grid_spec=grid_spec,
        out_shape=jax.ShapeDtypeStruct((b, t, d), F32),
        compiler_params=_params("parallel", "arbitrary"),
        name="combine",
    )(lo2, ye, slot_t, aff_t, x1, gate2, final_g)


def _layer(x, c, ctx, c_ctx, w_mod, b_mod, norm1_g, norm2_g, w_in, conv_w, conv_b, b_if,
           pool_mix, pool_scale, mlstm_norm_g, w_pool_out, w_mlstm_out, w_out,
           w_router, w_gate, w_up, w_down):
    b, t, d = x.shape
    pw = d // 2
    ng = N_DIRS * 2 * N_HEADS
    q_off, k_off, v_off, o_off = pw, pw + d, pw + 2 * d, pw + 3 * d
    if_off, gate_off = pw + 4 * d, pw + 4 * d + ng
    cap = EC_CAPACITY * t // N_EXPERTS
    row = lambda a: a.reshape(1, -1)

    rows = -(-(b + 1) // 8) * 8
    cvec = jnp.zeros((rows, d), F32).at[:b].set(c).at[b].set(c_ctx)
    mod = _mod_call(cvec, w_mod, row(b_mod))
    shift1, scale1, gate1, shift2, scale2, gate2 = [
        mod[:b, j * d:(j + 1) * d].reshape(b, 1, d) for j in range(6)]
    shift_c, scale_c = mod[b:b + 1, 0:d], mod[b:b + 1, d:2 * d]

    w_in_b = w_in.astype(BF16)
    nq = N_DIRS * N_HEADS
    w_if3 = w_in_b[:, if_off:gate_off].reshape(d, N_DIRS, 2, N_HEADS)
    b_if3 = b_if.reshape(N_DIRS, 2, N_HEADS)
    w_i, w_f = w_if3[:, :, 0, :].reshape(d, nq), w_if3[:, :, 1, :].reshape(d, nq)
    b_i, b_f = b_if3[:, 0, :].reshape(nq), b_if3[:, 1, :].reshape(nq)
    pad_w = lambda w: jnp.zeros((d, LANES), BF16).at[:, :nq].set(w)
    pad_b = lambda v: jnp.zeros((1, LANES), F32).at[0, :nq].set(v)

    w_vt = w_in_b[:, v_off:o_off].T
    c0, m0 = _ctx_call(ctx, shift_c, scale_c, row(norm1_g),
                       w_in_b[:, k_off:v_off], w_vt, pad_w(w_i), pad_w(w_f),
                       conv_w[:, d:], row(conv_b[d:]), pad_b(b_i), pad_b(b_f))

    u, q, k, vt, og, gg, cq, rows = _proj_call(
        x, shift1, scale1, row(norm1_g),
        w_in_b[:, 0:q_off], w_in_b[:, q_off:v_off], w_vt, w_in_b[:, o_off:if_off],
        w_in_b[:, gate_off:], w_i.T, w_f.T,
        conv_w, row(conv_b), b_i.reshape(nq, 1), b_f.reshape(nq, 1))

    p = _pool_call(u, pool_mix.astype(BF16), row(pool_scale))
    m = _mlstm_call(q, k, vt, og, cq, rows, c0, m0, jnp.broadcast_to(mlstm_norm_g[:, None], (d, LANES)))

    x1, h2, aff_t = _merge_call(p, m, gg, x, gate1, shift2, scale2, row(norm2_g),
                                w_pool_out.astype(BF16), w_mlstm_out.astype(BF16), w_out.astype(BF16),
                                w_router.T)
    slot_t, lo = _route_call(aff_t, cap)
    lo2 = lo[:, :, :t // GATHER_TILE + 1].reshape(b, -1)
    xe = _gather_call(lo2, h2, slot_t, cap)
    ye = _expert_call(xe, w_gate, w_up, w_down)
    return lo2, ye, slot_t, aff_t, x1, gate2


def kernel(x, c, ctx, c_ctx, w_mod, b_mod, norm1_g, norm2_g, w_in, conv_w, conv_b, b_if, pool_mix, pool_scale,
           mlstm_norm_g, w_pool_out, w_mlstm_out, w_out, w_router, w_gate, w_up, w_down, final_g):
    depth = w_mod.shape[0]
    for l in range(depth):
        lo2, ye, slot_t, aff_t, x1, gate2 = _layer(
            x, c, ctx, c_ctx, w_mod[l], b_mod[l], norm1_g[l], norm2_g[l], w_in[l], conv_w[l], conv_b[l],
            b_if[l], pool_mix[l], pool_scale[l], mlstm_norm_g[l], w_pool_out[l], w_mlstm_out[l], w_out[l],
            w_router[l], w_gate[l], w_up[l], w_down[l])
        x = _combine_call(lo2, ye, slot_t, aff_t, x1, gate2, final_g.reshape(1, -1), final_norm=l == depth - 1)
    return x
```

```python
import functools

import jax
import jax.numpy as jnp
import numpy as np
from jax import lax
from jax.experimental import pallas as pl
from jax.experimental.pallas import tpu as pltpu

F32 = jnp.float32
BF16 = jnp.bfloat16

GRID_W = 64
POOL_WINDOWS = (2, 4, 8, 16)
N_HEADS = 4
CONV_W = 5
N_DIRS = 2
N_EXPERTS = 16
EC_CAPACITY = 2
NORM_EPS = 1e-6
LOG2E = 1.4426950408889634

CHUNK = 256
TOKEN_TILE = 512
MERGE_TILE = 1024
HALO = 16
LANES = 128
AUG_ROWS = 16
V7X_VMEM_LIMIT_BYTES = 56 * 1024 * 1024

NN = (((1,), (0,)), ((), ()))
NT = (((1,), (1,)), ((), ()))
TN = (((0,), (0,)), ((), ()))


def _dot(a, b, dims=NN):
    return lax.dot_general(a, b, dims, preferred_element_type=F32)


def _split2(a):
    hi = a.astype(BF16)
    lo = (a - hi.astype(F32)).astype(BF16)
    return hi, lo


def _split3(a):
    a1 = a.astype(BF16)
    r1 = a - a1.astype(F32)
    a2 = r1.astype(BF16)
    a3 = (r1 - a2.astype(F32)).astype(BF16)
    return a1, a2, a3


def _dot3(a, b, dims=NN):
    ah, al = _split2(a)
    bh, bl = _split2(b)
    return _dot(ah, bh, dims) + _dot(ah, bl, dims) + _dot(al, bh, dims)


def _dot_left01(t01, a):
    a1, a2, a3 = _split3(a)
    return _dot(t01, a1) + _dot(t01, a2) + _dot(t01, a3)


def _dot_right01(a, t01):
    a1, a2, a3 = _split3(a)
    return _dot(a1, t01) + _dot(a2, t01) + _dot(a3, t01)


def _silu(x):
    return x * jax.nn.sigmoid(x)


def _log_sigmoid(x):
    return jnp.minimum(x, 0.0) - jnp.log1p(jnp.exp(-jnp.abs(x)))


def _rms_scale(x):
    return lax.rsqrt(jnp.mean(x * x, axis=-1, keepdims=True) + NORM_EPS)


def _tri01(n, kind):
    i = lax.broadcasted_iota(jnp.int32, (n, n), 0)
    j = lax.broadcasted_iota(jnp.int32, (n, n), 1)
    cond = {"le": j <= i, "ge": j >= i, "lt": j < i, "gt": j > i}[kind]
    return jnp.where(cond, 1.0, 0.0).astype(BF16)


def _shift(n):
    assert n & (n - 1) == 0, n
    return n.bit_length() - 1


def _div_pow2(x, n):
    return lax.shift_right_logical(x, _shift(n))


def _mod_pow2(x, n):
    return jnp.bitwise_and(x, n - 1)


def _params(*sem):
    return pltpu.CompilerParams(dimension_semantics=sem, vmem_limit_bytes=V7X_VMEM_LIMIT_BYTES)


def _mod_kernel(c_ref, w_ref, b_ref, o_ref):
    o_ref[...] = _dot3(_silu(c_ref[...]), w_ref[...]) + b_ref[...]


def _mod_call(cvec, w_mod, b_mod):
    rows, d = cvec.shape
    n = w_mod.shape[1]
    tn = 1536
    return pl.pallas_call(
        _mod_kernel,
        grid=(n // tn,),
        in_specs=[pl.BlockSpec((rows, d), lambda j: (0, 0)),
                  pl.BlockSpec((d, tn), lambda j: (0, j)),
                  pl.BlockSpec((1, tn), lambda j: (0, j))],
        out_specs=pl.BlockSpec((rows, tn), lambda j: (0, j)),
        out_shape=jax.ShapeDtypeStruct((rows, n), F32),
        compiler_params=_params("parallel"),
        name="mod",
    )(cvec, w_mod, b_mod)


def _ctx_kernel(ctx_ref, sh_ref, sc_ref, g_ref, wk_ref, wvt_ref, wi_ref, wf_ref, cw_ref, cb_ref, bi_ref, bf_ref,
                c_out, m_out):
    lc, d = ctx_ref.shape
    dh = d // N_HEADS
    x = ctx_ref[...]
    hc = (x * _rms_scale(x) * g_ref[...]) * (1.0 + sc_ref[...]) + sh_ref[...]
    hcb = hc.astype(BF16)

    kpre = _dot(hcb, wk_ref[...])
    pad = jnp.zeros((8, d), F32)
    kp = jnp.concatenate([pad, kpre, pad], axis=0)
    cw = cw_ref[...]
    acc = cb_ref[...] + cw[0:1, :] * kp[6:6 + lc, :]
    for j in range(1, CONV_W):
        acc = acc + cw[j:j + 1, :] * kp[6 + j:6 + j + lc, :]
    k = _silu(acc) * (dh ** -0.5)
    vt = _dot(wvt_ref[...], hcb, NT).astype(BF16)

    gi = _dot(hcb, wi_ref[...]) + bi_ref[...]
    lf = _log_sigmoid(_dot(hcb, wf_ref[...]) + bf_ref[...])
    lane = lax.broadcasted_iota(jnp.int32, lf.shape, 1)
    w_all = gi + jnp.where(lane < N_HEADS, _dot_left01(_tri01(lc, "gt"), lf), _dot_left01(_tri01(lc, "lt"), lf))
    ones = jnp.ones((AUG_ROWS, lc), BF16)
    for dr in range(N_DIRS):
        for h in range(N_HEADS):
            col = dr * N_HEADS + h
            w = w_all[:, col:col + 1]
            m = jnp.max(w, axis=0, keepdims=True)
            wk = jnp.exp(w - m) * k[:, h * dh:(h + 1) * dh]
            vt_aug = jnp.concatenate([vt[h * dh:(h + 1) * dh, :], ones], axis=0)
            c_out[dr, h] = _dot(vt_aug, wk.astype(BF16))
            m_out[dr, h] = jnp.broadcast_to(m * LOG2E, (1, LANES))


def _ctx_call(ctx, sh_c, sc_c, g1, wk, wvt, wi, wf, cw_k, cb_k, bi, bf):
    b, lc, d = ctx.shape
    dh = d // N_HEADS
    row = lambda w: pl.BlockSpec((1, w), lambda i: (0, 0))
    return pl.pallas_call(
        _ctx_kernel,
        grid=(b,),
        in_specs=[pl.BlockSpec((None, lc, d), lambda i: (i, 0, 0)),
                  row(d), row(d), row(d),
                  pl.BlockSpec((d, d), lambda i: (0, 0)),
                  pl.BlockSpec((d, d), lambda i: (0, 0)),
                  pl.BlockSpec((d, LANES), lambda i: (0, 0)),
                  pl.BlockSpec((d, LANES), lambda i: (0, 0)),
                  pl.BlockSpec((CONV_W, d), lambda i: (0, 0)),
                  row(d), row(LANES), row(LANES)],
        out_specs=[pl.BlockSpec((None, N_DIRS, N_HEADS, dh + AUG_ROWS, dh), lambda i: (i, 0, 0, 0, 0)),
                   pl.BlockSpec((None, N_DIRS, N_HEADS, 1, LANES), lambda i: (i, 0, 0, 0, 0))],
        out_shape=[jax.ShapeDtypeStruct((b, N_DIRS, N_HEADS, dh + AUG_ROWS, dh), F32),
                   jax.ShapeDtypeStruct((b, N_DIRS, N_HEADS, 1, LANES), F32)],
        compiler_params=_params("parallel"),
        name="ctx_states",
    )(ctx, sh_c, sc_c, g1, wk, wvt, wi, wf, cw_k, cb_k, bi, bf)


def _cummax_lanes(x, reverse):
    n = x.shape[-1]
    lane = lax.broadcasted_iota(jnp.int32, x.shape, x.ndim - 1)
    k = 1
    while k < n:
        if reverse:
            shifted = jnp.where(lane < n - k, pltpu.roll(x, n - k, axis=x.ndim - 1), -jnp.inf)
        else:
            shifted = jnp.where(lane >= k, pltpu.roll(x, k, axis=x.ndim - 1), -jnp.inf)
        x = jnp.maximum(x, shifted)
        k *= 2
    return x


def _proj_kernel(xp_ref, x_ref, xn_ref, sh_ref, sc_ref, g_ref,
                 wpool_ref, wqk_ref, wvt_ref, wo_ref, wg_ref, wit_ref, wft_ref,
                 cw_ref, cb_ref, bit_ref, bft_ref,
                 u_out, q_out, k_out, vt_out, og_out, gg_out, cq_out, rows_out,
                 hx_scr, r_scr):
    tm, d = x_ref.shape
    dh = d // N_HEADS
    i = pl.program_id(1)
    last = pl.num_programs(1) - 1

    x_ext = jnp.concatenate([xp_ref[...], x_ref[...], xn_ref[...]], axis=0)
    hx = (x_ext * _rms_scale(x_ext) * g_ref[...]) * (1.0 + sc_ref[...]) + sh_ref[...]
    n_ext = tm + 2 * HALO
    r_id = lax.broadcasted_iota(jnp.int32, (n_ext, 1), 0)
    valid = jnp.logical_and(jnp.logical_or(i > 0, r_id >= HALO),
                            jnp.logical_or(i < last, r_id < HALO + tm))
    hx_scr[...] = jnp.where(valid, hx, 0.0).astype(BF16)
    hxc = hx_scr[HALO:HALO + tm, :]

    nc = 512
    half = CONV_W // 2

    def qk_dot(c):
        r_scr[c % 2] = _dot(hx_scr[...], wqk_ref[:, c * nc:(c + 1) * nc])

    def qk_conv(c):
        cols = slice(c * nc, (c + 1) * nc)
        r = r_scr.at[c % 2]
        cw = cw_ref[:, cols]
        acc = cb_ref[:, cols] + cw[0:1, :] * r[HALO - half:HALO - half + tm, :]
        for j in range(1, CONV_W):
            acc = acc + cw[j:j + 1, :] * r[HALO - half + j:HALO - half + j + tm, :]
        y = _silu(acc)
        if c * nc < d:
            q_out[:, cols] = y.astype(BF16)
        else:
            k_out[:, c * nc - d:(c + 1) * nc - d] = (y * (dh ** -0.5)).astype(BF16)

    def v_chunk(c):
        cols = slice(c * nc, (c + 1) * nc)
        vt_out[cols, :] = _dot(wvt_ref[cols, :], hxc, NT).astype(BF16)

    def o_chunk(c):
        cols = slice(c * nc, (c + 1) * nc)
        og_out[:, cols] = jax.nn.sigmoid(_dot(hxc, wo_ref[:, cols])).astype(BF16)

    def g_chunk(c):
        cols = slice(c * nc, (c + 1) * nc)
        gg_out[:, cols] = jax.nn.sigmoid(_dot(hxc, wg_ref[:, cols])).astype(BF16)

    qk_dot(0); qk_dot(1)
    qk_conv(0); v_chunk(0); v_chunk(1); qk_dot(2)
    qk_conv(1); o_chunk(0); o_chunk(1); qk_dot(3)
    qk_conv(2); g_chunk(0); g_chunk(1)
    qk_conv(3); g_chunk(2); g_chunk(3)
    u_out[...] = _dot(hxc, wpool_ref[...]).astype(BF16)

    nq = N_DIRS * N_HEADS
    gi_r = _dot(wit_ref[...], hxc, NT) + bit_ref[...]
    lf_r = _log_sigmoid(_dot(wft_ref[...], hxc, NT) + bft_ref[...])
    fwd_sub = lax.broadcasted_iota(jnp.int32, (nq, CHUNK), 0) < N_HEADS
    t_le, t_ge = _tri01(CHUNK, "le"), _tri01(CHUNK, "ge")
    for j in range(tm // CHUNK):
        rows = slice(j * CHUNK, (j + 1) * CHUNK)
        b_r = jnp.where(fwd_sub, _dot_right01(lf_r[:, rows], t_ge), _dot_right01(lf_r[:, rows], t_le))
        c_r = (gi_r[:, rows] - b_r) * LOG2E
        b_r = b_r * LOG2E
        cq_out[rows, :] = c_r.T
        rows_out[0:nq, rows] = b_r
        rows_out[nq:2 * nq, rows] = jnp.where(fwd_sub, _cummax_lanes(c_r, False), _cummax_lanes(c_r, True))


def _proj_call(x, sh, sc, g1, wpool, wqk, wvt, wo, wg, wit, wft, cw, cb, bit, bft):
    b, t, d = x.shape
    tm = TOKEN_TILE
    nt = t // tm
    hb = tm // HALO
    nq = N_DIRS * N_HEADS
    per_b = pl.BlockSpec((None, 1, d), lambda bi_, i: (bi_, 0, 0))
    tile = lambda w: pl.BlockSpec((None, tm, w), lambda bi_, i: (bi_, i, 0))
    tile_t = lambda h: pl.BlockSpec((None, h, tm), lambda bi_, i: (bi_, 0, i))
    const = lambda a: pl.BlockSpec(a.shape, lambda bi_, i: (0,) * a.ndim)
    out_shapes = [jax.ShapeDtypeStruct((b, t, d // 2), BF16),
                  jax.ShapeDtypeStruct((b, t, d), BF16),
                  jax.ShapeDtypeStruct((b, t, d), BF16),
                  jax.ShapeDtypeStruct((b, d, t), BF16),
                  jax.ShapeDtypeStruct((b, t, d), BF16),
                  jax.ShapeDtypeStruct((b, t, 2 * d), BF16),
                  jax.ShapeDtypeStruct((b, t, nq), F32),
                  jax.ShapeDtypeStruct((b, 2 * nq, t), F32)]
    out_specs = [tile(d // 2), tile(d), tile(d), tile_t(d), tile(d), tile(2 * d),
                 tile(nq), tile_t(2 * nq)]
    return pl.pallas_call(
        _proj_kernel,
        grid=(b, nt),
        in_specs=[pl.BlockSpec((None, HALO, d), lambda bi_, i: (bi_, jnp.maximum(i * hb - 1, 0), 0)),
                  tile(d),
                  pl.BlockSpec((None, HALO, d), lambda bi_, i: (bi_, jnp.minimum((i + 1) * hb, t // HALO - 1), 0)),
                  per_b, per_b, const(g1),
                  const(wpool), const(wqk), const(wvt), const(wo), const(wg),
                  const(wit), const(wft),
                  const(cw), const(cb), const(bit), const(bft)],
        out_specs=out_specs,
        out_shape=out_shapes,
        scratch_shapes=[pltpu.VMEM((tm + 2 * HALO, d), BF16),
                        pltpu.VMEM((2, tm + 2 * HALO, 512), F32)],
        compiler_params=_params("parallel", "parallel"),
        name="proj",
    )(x, x, x, sh, sc, g1, wpool, wqk, wvt, wo, wg, wit, wft, cw, cb, bit, bft)


def _pool_kernel(u_ref, mix_ref, scale_ref, inv_ref, p_out, pad_scr):
    t, pw = u_ref.shape
    gw = pw // len(POOL_WINDOWS)
    tile = 256
    maxlo = max(POOL_WINDOWS) // 2
    padr = maxlo * GRID_W
    ti = lax.broadcasted_iota(jnp.int32, (tile, tile), 0)
    tj = lax.broadcasted_iota(jnp.int32, (tile, tile), 1)
    same_row = _div_pow2(ti, GRID_W) == _div_pow2(tj, GRID_W)
    ci, cj = _mod_pow2(ti, GRID_W), _mod_pow2(tj, GRID_W)

    def span(dlt, ext):
        return pad_scr[padr + (dlt - ext) * GRID_W:padr + (dlt + ext) * GRID_W + t, :]

    for g, side in enumerate(POOL_WINDOWS):
        lo, hi = side // 2, side - side // 2
        assert lo == hi and side & (side - 1) == 0
        cols = slice(g * gw, (g + 1) * gw)
        pad_scr[0:padr, :] = jnp.zeros((padr, gw), F32)
        pad_scr[padr + t:padr + t + padr, :] = jnp.zeros((padr, gw), F32)
        band = jnp.logical_and(same_row, jnp.logical_and(cj >= ci - lo, cj < ci + hi))
        pw01 = jnp.where(band, 1.0, 0.0).astype(BF16)
        for k in range(t // tile):
            rs = slice(k * tile, (k + 1) * tile)
            pad_scr[padr + k * tile:padr + (k + 1) * tile, :] = _dot(pw01, u_ref[rs, cols])
        ext = (side - 2) // 2
        tot = span(-1, ext) + span(0, ext)
        k = 2
        while k < side:
            pad_scr[padr - ext * GRID_W:padr + ext * GRID_W + t, :] = tot
            ext = (side - 2 * k) // 2
            tot = span(-(k // 2), ext) + span(k // 2, ext)
            k *= 2
        a = tot * inv_ref[g] - u_ref[:, cols].astype(F32)
        p = _dot(a.astype(BF16), mix_ref[g]) * scale_ref[:, cols]
        p_out[:, cols] = p.astype(BF16)


def _pool_inv_counts(t, gw):
    rows = t // GRID_W
    r, c = np.arange(t) // GRID_W, np.arange(t) % GRID_W
    out = []
    for side in POOL_WINDOWS:
        lo, hi = side // 2, side - side // 2
        cnt = ((np.minimum(r + hi, rows) - np.maximum(r - lo, 0))
               * (np.minimum(c + hi, GRID_W) - np.maximum(c - lo, 0)))
        out.append(np.broadcast_to((1.0 / cnt).astype(np.float32)[:, None], (t, gw)))
    return jnp.asarray(np.stack(out))


def _pool_call(u, mix, scale):
    b, t, pw = u.shape
    ng = len(POOL_WINDOWS)
    gw = pw // ng
    padr = (max(POOL_WINDOWS) // 2) * GRID_W
    return pl.pallas_call(
        _pool_kernel,
        grid=(b,),
        in_specs=[pl.BlockSpec((None, t, pw), lambda i: (i, 0, 0)),
                  pl.BlockSpec(mix.shape, lambda i: (0, 0, 0)),
                  pl.BlockSpec((1, pw), lambda i: (0, 0)),
                  pl.BlockSpec((ng, t, gw), lambda i: (0, 0, 0))],
        out_specs=pl.BlockSpec((None, t, pw), lambda i: (i, 0, 0)),
        out_shape=jax.ShapeDtypeStruct((b, t, pw), BF16),
        scratch_shapes=[pltpu.VMEM((t + 2 * padr, gw), F32)],
        compiler_params=_params("parallel"),
        name="pool",
    )(u, mix, scale, _pool_inv_counts(t, gw))


def _mlstm_dir(q_ref, k_ref, vt_ref, cq_ref, rows_ref, c_scr, m_scr, reverse):
    L, d = q_ref.shape
    dh = d // N_HEADS
    nq = N_DIRS * N_HEADS
    si = lax.broadcasted_iota(jnp.int32, (L, L), 0)
    tj = lax.broadcasted_iota(jnp.int32, (L, L), 1)
    mask = (si >= tj) if reverse else (si <= tj)
    ones = jnp.ones((AUG_ROWS, L), BF16)
    end = 0 if reverse else L - 1
    off = N_HEADS if reverse else 0

    hs_all = []
    for h in range(N_HEADS):
        hs = slice(h * dh, (h + 1) * dh)
        st = off + h
        q = q_ref[:, hs]
        k = k_ref[:, hs]
        vt_aug = jnp.concatenate([vt_ref[hs, :], ones], axis=0)
        c_c = cq_ref[:, st:st + 1]
        b_r = rows_ref[st:st + 1, :]
        cm_r = rows_ref[nq + st:nq + st + 1, :]
        m_prev = m_scr[st][:, 0:1]
        ct_prev = c_scr[st]

        mm = jnp.maximum(cm_r, m_prev)
        w_inter = jnp.exp2(m_prev - mm)
        st_mat = (_dot(k, q, NT) * jnp.exp2(jnp.where(mask, c_c - mm, -jnp.inf))).astype(BF16)
        intra = _dot(vt_aug, st_mat)
        inter = _dot(ct_prev.astype(BF16), q, NT)
        den = w_inter * inter[dh:dh + 1, :] + intra[dh:dh + 1, :]
        inv = 1.0 / jnp.maximum(jnp.abs(den), jnp.exp2(-(b_r + mm)))
        hs_all.append((w_inter * inter[0:dh, :] + intra[0:dh, :]) * inv)

        g_tot = b_r[:, end:end + 1]
        m_new = g_tot + jnp.maximum(m_prev, cm_r[:, end:end + 1])
        decay = jnp.exp2(g_tot + m_prev - m_new)
        wk = (k.astype(F32) * jnp.exp2(g_tot + c_c - m_new)).astype(BF16)
        c_scr[st] = decay * ct_prev + _dot(vt_aug, wk)
        m_scr[st] = jnp.broadcast_to(m_new, (1, LANES))
    return hs_all


def _mlstm_kernel(qf_ref, kf_ref, vf_ref, ogf_ref, cqf_ref, rwf_ref,
                  qb_ref, kb_ref, vb_ref, ogb_ref, cqb_ref, rwb_ref,
                  c0_ref, m0_ref, ng_ref, o_ref,
                  c_scr, m_scr, hf_scr, hb_scr):
    L, d = qf_ref.shape
    dh = d // N_HEADS
    s = pl.program_id(1)
    nch = pl.num_programs(1)
    half = nch // 2

    @pl.when(s == 0)
    def _():
        for j in range(N_DIRS * N_HEADS):
            c_scr[j] = c0_ref[j // N_HEADS, j % N_HEADS]
            m_scr[j] = m0_ref[j // N_HEADS, j % N_HEADS]

    h_f = _mlstm_dir(qf_ref, kf_ref, vf_ref, cqf_ref, rwf_ref, c_scr, m_scr, False)
    h_b = _mlstm_dir(qb_ref, kb_ref, vb_ref, cqb_ref, rwb_ref, c_scr, m_scr, True)

    @pl.when(s < half)
    def _():
        for h in range(N_HEADS):
            hs = slice(h * dh, (h + 1) * dh)
            hf_scr[s, hs, :] = h_f[h]
            hb_scr[half - 1 - s, hs, :] = h_b[h]

    @pl.when(s >= half)
    def _():
        def finish(ht, h, og_ref, out):
            hs = slice(h * dh, (h + 1) * dh)
            scale = lax.rsqrt(jnp.mean(ht * ht, axis=0, keepdims=True) + NORM_EPS)
            y = jnp.concatenate([ht[:, i * LANES:(i + 1) * LANES] * scale[:, i * LANES:(i + 1) * LANES]
                                 * ng_ref[hs, :] for i in range(L // LANES)], axis=1).T
            out[:, hs] = (y * og_ref[:, hs].astype(F32)).astype(BF16)

        for h in range(N_HEADS):
            hs = slice(h * dh, (h + 1) * dh)
            finish(h_f[h] + hb_scr[s - half, hs, :], h, ogf_ref, o_ref.at[1])
            finish(h_b[h] + hf_scr[nch - 1 - s, hs, :], h, ogb_ref, o_ref.at[0])


def _mlstm_call(q, k, vt, og, cq, rows, c0, m0, norm_g):
    b, t, d = q.shape
    dh = d // N_HEADS
    L = CHUNK
    nch = t // L
    half = nch // 2
    assert nch % 2 == 0
    nq = N_DIRS * N_HEADS

    def specs(chunk):
        seq = lambda w: pl.BlockSpec((None, L, w), lambda bi, s: (bi, chunk(s), 0))
        seq_t = lambda h: pl.BlockSpec((None, h, L), lambda bi, s: (bi, 0, chunk(s)))
        return [seq(d), seq(d), seq_t(d), seq(d), seq(nq), seq_t(2 * nq)]

    state = lambda w0, w1: pl.BlockSpec((None, N_DIRS, N_HEADS, w0, w1), lambda bi, s: (bi, 0, 0, 0, 0))
    return pl.pallas_call(
        _mlstm_kernel,
        grid=(b, nch),
        in_specs=specs(lambda s: s) + specs(lambda s: nch - 1 - s) + [
            state(dh + AUG_ROWS, dh), state(1, LANES), pl.BlockSpec((d, LANES), lambda bi, s: (0, 0))],
        out_specs=pl.BlockSpec((None, 2, None, L, d), lambda bi, s: (bi, 0, jnp.maximum(s - half, 0), 0, 0)),
        out_shape=jax.ShapeDtypeStruct((b, 2, half, L, d), BF16),
        scratch_shapes=[pltpu.VMEM((N_DIRS * N_HEADS, dh + AUG_ROWS, dh), F32),
                        pltpu.VMEM((N_DIRS * N_HEADS, 1, LANES), F32),
                        pltpu.VMEM((half, d, L), F32),
                        pltpu.VMEM((half, d, L), F32)],
        compiler_params=_params("parallel", "arbitrary"),
        name="mlstm",
    )(q, k, vt, og, cq, rows, q, k, vt, og, cq, rows, c0, m0, norm_g)


def _merge_kernel(p_ref, m_ref, gg_ref, x_ref, g1_ref, sh2_ref, sc2_ref, n2_ref,
                  wpo_ref, wmo_ref, wout_ref, wr_ref, x1_out, h2_out, aff_out):
    tm, d = x_ref.shape
    cpt = m_ref.shape[0]
    L = m_ref.shape[1]
    upper = pl.program_id(1) >= pl.num_programs(1) // 2
    sub = TOKEN_TILE
    cps = sub // L
    streams = [slice(r * sub, (r + 1) * sub) for r in range(tm // sub)]

    def branches(r):
        m = jnp.where(upper,
                      jnp.concatenate([m_ref[r * cps + j] for j in range(cps)], axis=0),
                      jnp.concatenate([m_ref[cpt - 1 - r * cps - j] for j in range(cps)], axis=0))
        return _dot(p_ref[streams[r], :], wpo_ref[...]), _dot(m, wmo_ref[...])

    def mix(r, a, mm):
        rows = streams[r]
        return (gg_ref[rows, 0:d].astype(F32) * a + gg_ref[rows, d:2 * d].astype(F32) * mm).astype(BF16)

    def residual(r, mixed):
        x1 = x_ref[streams[r], :] + g1_ref[...] * _dot(mixed, wout_ref[...])
        x1_out[streams[r], :] = x1
        return x1

    def tail(r, x1):
        rows = streams[r]
        h2 = (x1 * _rms_scale(x1) * n2_ref[...]) * (1.0 + sc2_ref[...]) + sh2_ref[...]
        h2_out[rows, :] = h2.astype(BF16)
        logits = _dot3(wr_ref[...], h2, NT)
        z = jnp.exp(logits - jnp.max(logits, axis=0, keepdims=True))
        aff_out[:, rows] = z / jnp.sum(z, axis=0, keepdims=True)

    n = len(streams)
    ab = [branches(r) for r in range(n)]
    mixed = [mix(r, *ab[r]) for r in range(n)]
    x1s = [residual(r, mixed[r]) for r in range(n)]
    for r in range(n):
        tail(r, x1s[r])


def _merge_call(p, m, gg, x, gate1, sh2, sc2, n2, wpo, wmo, wout, wr_t):
    b, t, d = x.shape
    tm = min(MERGE_TILE, t // 2)
    e = wr_t.shape[0]
    per_b = pl.BlockSpec((None, 1, d), lambda bi, i: (bi, 0, 0))
    tile = lambda w: pl.BlockSpec((None, tm, w), lambda bi, i: (bi, i, 0))
    const = lambda a: pl.BlockSpec(a.shape, lambda bi, i: (0,) * a.ndim)
    cpt = tm // CHUNK
    nth = t // tm // 2
    m_spec = pl.BlockSpec((None, None, cpt, CHUNK, d),
                          lambda bi, i: (bi, i // nth, jnp.where(i >= nth, i - nth, nth - 1 - i), 0, 0))
    return pl.pallas_call(
        _merge_kernel,
        grid=(b, t // tm),
        in_specs=[tile(d // 2), m_spec, tile(2 * d), tile(d), per_b, per_b, per_b, const(n2),
                  const(wpo), const(wmo), const(wout), const(wr_t)],
        out_specs=[tile(d), tile(d), pl.BlockSpec((None, e, tm), lambda bi, i: (bi, 0, i))],
        out_shape=[jax.ShapeDtypeStruct((b, t, d), F32),
                   jax.ShapeDtypeStruct((b, t, d), BF16),
                   jax.ShapeDtypeStruct((b, e, t), F32)],
        compiler_params=_params("parallel", "parallel"),
        name="merge",
    )(p, m, gg, x, gate1, sh2, sc2, n2, wpo, wmo, wout, wr_t)


def _route_kernel(aff_ref, slot_out, lo_out, *, cap):
    e, t = aff_ref.shape
    aff = aff_ref[...]

    def step(i, thr):
        cand = thr | (jnp.int32(1) << (30 - i))
        cnt = jnp.sum(jnp.where(aff >= pltpu.bitcast(cand, F32), 1.0, 0.0), axis=-1, keepdims=True)
        return jnp.where(cnt >= cap, cand, thr)

    thr = pltpu.bitcast(lax.fori_loop(0, 31, step, jnp.zeros((e, 1), jnp.int32)), F32)
    gt = aff > thr
    eq = aff == thr
    need = cap - jnp.sum(jnp.where(gt, 1.0, 0.0), axis=-1, keepdims=True).astype(jnp.int32)

    seg = 256
    t_ge = _tri01(seg, "ge")

    def prefix_incl(x01):
        outs, carries, carry = [], [], jnp.zeros((e, 1), F32)
        for j in range(t // seg):
            p = _dot(x01[:, j * seg:(j + 1) * seg].astype(BF16), t_ge) + carry
            outs.append(p)
            carry = p[:, seg - 1:seg]
            carries.append(carry)
        return jnp.concatenate(outs, axis=1), carries

    eq_f = jnp.where(eq, 1.0, 0.0)
    tie_rank = (prefix_incl(eq_f)[0] - eq_f).astype(jnp.int32)
    sel = jnp.logical_or(gt, jnp.logical_and(eq, tie_rank < need))
    rank, carries = prefix_incl(jnp.where(sel, 1.0, 0.0))
    slot_out[...] = jnp.where(sel, rank.astype(jnp.int32) - 1, -1)

    lane = lax.broadcasted_iota(jnp.int32, (e, LANES), 1)
    lo = jnp.zeros((e, LANES), F32)
    per_tile = GATHER_TILE // seg
    for c in range(1, t // GATHER_TILE + 1):
        lo = jnp.where(lane == c, carries[c * per_tile - 1], lo)
    lo_out[...] = lo.astype(jnp.int32)


def _route_call(aff_t, cap):
    b, e, t = aff_t.shape
    n = b * e
    slot, lo = pl.pallas_call(
        functools.partial(_route_kernel, cap=cap),
        grid=(1,),
        in_specs=[pl.BlockSpec((n, t), lambda i: (0, 0))],
        out_specs=[pl.BlockSpec((n, t), lambda i: (0, 0)),
                   pl.BlockSpec((n, LANES), lambda i: (0, 0))],
        out_shape=[jax.ShapeDtypeStruct((n, t), jnp.int32),
                   jax.ShapeDtypeStruct((n, LANES), jnp.int32)],
        compiler_params=_params("arbitrary"),
        name="route",
    )(aff_t.reshape(n, t))
    return slot.reshape(b, e, t), lo.reshape(b, e, LANES)


SLOT_WINDOW = 96
GATHER_TILE = 256
GATHER_WINDOW = 64
SLOT_ALIGN = 16
EXPERT_GROUP = 8


def _aligned(lo):
    return jnp.bitwise_and(lo, -SLOT_ALIGN)


def _window_start(nominal, cap, w):
    return pl.multiple_of(jnp.minimum(nominal, cap - w), SLOT_ALIGN)


def _n_windows(lo, hi, w):
    return lax.div(hi - _aligned(lo) + (w - 1), w)


def _gather_kernel(lo_ref, h2_ref, slot_ref, xe_out, *, cap, n_tiles):
    n_exp = slot_ref.shape[0]
    tc, w = GATHER_TILE, GATHER_WINDOW
    b = pl.program_id(0)
    stride = n_tiles + 1
    xe_out[...] = jnp.zeros(xe_out.shape, BF16)
    s_id = lax.broadcasted_iota(jnp.int32, (w, tc), 0)

    def add_rows(e, start, z):
        xe_out[e, pl.ds(start, w), :] = xe_out[e, pl.ds(start, w), :] + z.astype(BF16)

    def tile_body(c, carry):
        t0 = pl.multiple_of(c * tc, tc)
        for g0 in range(0, n_exp, EXPERT_GROUP):
            starts, blocks = [], []
            for e in range(g0, g0 + EXPERT_GROUP):
                a0 = _window_start(_aligned(lo_ref[b, e * stride + c]), cap, w)
                hit = (s_id + a0) == slot_ref[e:e + 1, pl.ds(t0, tc)]
                blocks.append(jnp.where(hit, 1.0, 0.0).astype(BF16))
                starts.append(a0)
            z = _dot(jnp.concatenate(blocks, axis=0), h2_ref[pl.ds(t0, tc), :])
            for j in range(EXPERT_GROUP):
                add_rows(g0 + j, starts[j], z[j * w:(j + 1) * w, :])
        for e in range(n_exp):
            lo, hi = lo_ref[b, e * stride + c], lo_ref[b, e * stride + c + 1]

            def window_body(k, carry2, e=e, lo=lo):
                nominal = _aligned(lo) + k * w
                a = _window_start(nominal, cap, w)
                srow = slot_ref[e:e + 1, pl.ds(t0, tc)]
                hit = jnp.logical_and((s_id + a) == srow, srow >= nominal)
                add_rows(e, a, _dot(jnp.where(hit, 1.0, 0.0).astype(BF16), h2_ref[pl.ds(t0, tc), :]))
                return carry2

            lax.fori_loop(1, _n_windows(lo, hi, w), window_body, 0)
        return carry

    lax.fori_loop(0, n_tiles, tile_body, 0)


def _gather_call(lo2, h2, slot_t, cap):
    b, t, d = h2.shape
    e = slot_t.shape[1]
    n_tiles = t // GATHER_TILE
    grid_spec = pltpu.PrefetchScalarGridSpec(
        num_scalar_prefetch=1,
        grid=(b,),
        in_specs=[pl.BlockSpec((None, t, d), lambda i, lo: (i, 0, 0)),
                  pl.BlockSpec((None, e, t), lambda i, lo: (i, 0, 0))],
        out_specs=pl.BlockSpec((None, e, cap, d), lambda i, lo: (i, 0, 0, 0)),
    )
    return pl.pallas_call(
        functools.partial(_gather_kernel, cap=cap, n_tiles=n_tiles),
        grid_spec=grid_spec,
        out_shape=jax.ShapeDtypeStruct((b, e, cap, d), BF16),
        compiler_params=_params("arbitrary"),
        name="gather",
    )(lo2, h2, slot_t)


def _expert_kernel(xe_ref, wg_ref, wu_ref, wd_ref, ye_out, wg_scr, wu_scr, wd_scr):
    e, b = pl.program_id(0), pl.program_id(1)
    n_exp = pl.num_programs(0) - 1
    slab = wg_ref.shape[0]
    f = wg_scr.shape[2]
    ns, cap, d = xe_ref.shape

    @pl.when(e < n_exp)
    def _():
        slot = lax.rem(e, 2)
        rows = pl.ds(pl.multiple_of(b * slab, slab), slab)
        wg_scr[slot, rows, :] = wg_ref[...].astype(BF16)
        wu_scr[slot, rows, :] = wu_ref[...].astype(BF16)
        wd_scr[slot, rows, :] = wd_ref[...].astype(BF16)

    @pl.when(e > 0)
    def _():
        slot = lax.rem(e + 1, 2)
        xe = xe_ref[...].reshape(ns * cap, d)
        fc = 512
        y = None
        for c in range(f // fc):
            cols = slice(c * fc, (c + 1) * fc)
            hid = _silu(_dot(xe, wg_scr[slot, :, cols])) * _dot(xe, wu_scr[slot, :, cols])
            part = _dot(hid.astype(BF16), wd_scr[slot, cols, :])
            y = part if y is None else y + part
        ye_out[...] = y.astype(BF16).reshape(ns, cap, d)


def _expert_call(xe, wg, wu, wd):
    b, e, cap, d = xe.shape
    f = wg.shape[2]
    ns = 2 if b % 2 == 0 else 1
    steps = b // ns
    assert d % steps == 0 and f % steps == 0
    w_spec = lambda rows, cols: pl.BlockSpec((None, rows // steps, cols),
                                             lambda ei, bi: (jnp.minimum(ei, e - 1), bi, 0))
    return pl.pallas_call(
        _expert_kernel,
        grid=(e + 1, steps),
        in_specs=[pl.BlockSpec((ns, None, cap, d), lambda ei, bi: (bi, jnp.maximum(ei - 1, 0), 0, 0)),
                  w_spec(d, f), w_spec(d, f), w_spec(f, d)],
        out_specs=pl.BlockSpec((ns, None, cap, d),
                               lambda ei, bi: (jnp.where(ei == 0, 0, bi), jnp.maximum(ei - 1, 0), 0, 0)),
        out_shape=jax.ShapeDtypeStruct((b, e, cap, d), BF16),
        scratch_shapes=[pltpu.VMEM((2, d, f), BF16), pltpu.VMEM((2, d, f), BF16), pltpu.VMEM((2, f, d), BF16)],
        compiler_params=_params("arbitrary", "arbitrary"),
        name="experts",
    )(xe, wg, wu, wd)


def _combine_kernel(lo_ref, ye_ref, slot_ref, aff_ref, x1_ref, g2_ref, fg_ref, o_ref, acc_scr,
                    *, final_norm, n_tiles):
    n_exp, cap, d = ye_ref.shape
    tm, w = x1_ref.shape[0], SLOT_WINDOW
    b, i = pl.program_id(0), pl.program_id(1)
    per = tm // GATHER_TILE
    stride = n_tiles * per + 1
    s_id = lax.broadcasted_iota(jnp.int32, (w, tm), 0)

    for g0 in range(0, n_exp, EXPERT_GROUP):
        ps, rows = [], []
        for e in range(g0, g0 + EXPERT_GROUP):
            a0 = _window_start(_aligned(lo_ref[b, e * stride + i * per]), cap, w)
            hit = (s_id + a0) == slot_ref[e:e + 1, :]
            ps.append(jnp.where(hit, aff_ref[e:e + 1, :], 0.0).astype(BF16))
            rows.append(ye_ref[e, pl.ds(a0, w), :])
        part = _dot(jnp.concatenate(ps, axis=0), jnp.concatenate(rows, axis=0), TN)
        if g0 == 0:
            acc_scr[...] = part
        else:
            acc_scr[...] += part

    for e in range(n_exp):
        lo, hi = lo_ref[b, e * stride + i * per], lo_ref[b, e * stride + (i + 1) * per]

        def window_body(k, carry, e=e, lo=lo):
            nominal = _aligned(lo) + k * w
            a = _window_start(nominal, cap, w)
            sr = slot_ref[e:e + 1, :]
            hit = jnp.logical_and((s_id + a) == sr, sr >= nominal)
            p = jnp.where(hit, aff_ref[e:e + 1, :], 0.0).astype(BF16)
            acc_scr[...] += _dot(p, ye_ref[e, pl.ds(a, w), :], TN)
            return carry

        lax.fori_loop(1, _n_windows(lo, hi, w), window_body, 0)

    x2 = x1_ref[...] + g2_ref[...] * acc_scr[...]
    o_ref[...] = x2 * _rms_scale(x2) * fg_ref[...] if final_norm else x2


def _combine_call(lo2, ye, slot_t, aff_t, x1, gate2, final_g, final_norm):
    b, t, d = x1.shape
    e, cap = ye.shape[1], ye.shape[2]
    tm = TOKEN_TILE
    grid_spec = pltpu.PrefetchScalarGridSpec(
        num_scalar_prefetch=1,
        grid=(b, t // tm),
        in_specs=[pl.BlockSpec((None, e, cap, d), lambda bi, i, lo: (bi, 0, 0, 0)),
                  pl.BlockSpec((None, e, tm), lambda bi, i, lo: (bi, 0, i)),
                  pl.BlockSpec((None, e, tm), lambda bi, i, lo: (bi, 0, i)),
                  pl.BlockSpec((None, tm, d), lambda bi, i, lo: (bi, i, 0)),
                  pl.BlockSpec((None, 1, d), lambda bi, i, lo: (bi, 0, 0)),
                  pl.BlockSpec((1, d), lambda bi, i, lo: (0, 0))],
        out_specs=pl.BlockSpec((None, tm, d), lambda bi, i, lo: (bi, i, 0)),
        scratch_shapes=[pltpu.VMEM((tm, d), F32)],
    )
    return pl.pallas_call(
        functools.partial(_combine_kernel, final_norm=final_norm, n_tiles=t // tm),
        grid_spec=grid_spec,
        out_shape=jax.ShapeDtypeStruct((b, t, d), F32),
        compiler_params=_params("parallel", "arbitrary"),
        name="combine",
    )(lo2, ye, slot_t, aff_t, x1, gate2, final_g)


def _layer(x, c, ctx, c_ctx, w_mod, b_mod, norm1_g, norm2_g, w_in, conv_w, conv_b, b_if,
           pool_mix, pool_scale, mlstm_norm_g, w_pool_out, w_mlstm_out, w_out,
           w_router, w_gate, w_up, w_down):
    b, t, d = x.shape
    pw = d // 2
    ng = N_DIRS * 2 * N_HEADS
    q_off, k_off, v_off, o_off = pw, pw + d, pw + 2 * d, pw + 3 * d
    if_off, gate_off = pw + 4 * d, pw + 4 * d + ng
    cap = EC_CAPACITY * t // N_EXPERTS
    row = lambda a: a.reshape(1, -1)

    rows = -(-(b + 1) // 8) * 8
    cvec = jnp.zeros((rows, d), F32).at[:b].set(c).at[b].set(c_ctx)
    mod = _mod_call(cvec, w_mod, row(b_mod))
    shift1, scale1, gate1, shift2, scale2, gate2 = [
        mod[:b, j * d:(j + 1) * d].reshape(b, 1, d) for j in range(6)]
    shift_c, scale_c = mod[b:b + 1, 0:d], mod[b:b + 1, d:2 * d]

    w_in_b = w_in.astype(BF16)
    nq = N_DIRS * N_HEADS
    w_if3 = w_in_b[:, if_off:gate_off].reshape(d, N_DIRS, 2, N_HEADS)
    b_if3 = b_if.reshape(N_DIRS, 2, N_HEADS)
    w_i, w_f = w_if3[:, :, 0, :].reshape(d, nq), w_if3[:, :, 1, :].reshape(d, nq)
    b_i, b_f = b_if3[:, 0, :].reshape(nq), b_if3[:, 1, :].reshape(nq)
    pad_w = lambda w: jnp.zeros((d, LANES), BF16).at[:, :nq].set(w)
    pad_b = lambda v: jnp.zeros((1, LANES), F32).at[0, :nq].set(v)

    w_vt = w_in_b[:, v_off:o_off].T
    c0, m0 = _ctx_call(ctx, shift_c, scale_c, row(norm1_g),
                       w_in_b[:, k_off:v_off], w_vt, pad_w(w_i), pad_w(w_f),
                       conv_w[:, d:], row(conv_b[d:]), pad_b(b_i), pad_b(b_f))

    u, q, k, vt, og, gg, cq, rows = _proj_call(
        x, shift1, scale1, row(norm1_g),
        w_in_b[:, 0:q_off], w_in_b[:, q_off:v_off], w_vt, w_in_b[:, o_off:if_off],
        w_in_b[:, gate_off:], w_i.T, w_f.T,
        conv_w, row(conv_b), b_i.reshape(nq, 1), b_f.reshape(nq, 1))

    p = _pool_call(u, pool_mix.astype(BF16), row(pool_scale))
    m = _mlstm_call(q, k, vt, og, cq, rows, c0, m0, jnp.broadcast_to(mlstm_norm_g[:, None], (d, LANES)))

    x1, h2, aff_t = _merge_call(p, m, gg, x, gate1, shift2, scale2, row(norm2_g),
                                w_pool_out.astype(BF16), w_mlstm_out.astype(BF16), w_out.astype(BF16),
                                w_router.T)
    slot_t, lo = _route_call(aff_t, cap)
    lo2 = lo[:, :, :t // GATHER_TILE + 1].reshape(b, -1)
    xe = _gather_call(lo2, h2, slot_t, cap)
    ye = _expert_call(xe, w_gate, w_up, w_down)
    return lo2, ye, slot_t, aff_t, x1, gate2


def kernel(x, c, ctx, c_ctx, w_mod, b_mod, norm1_g, norm2_g, w_in, conv_w, conv_b, b_if, pool_mix, pool_scale,
           mlstm_norm_g, w_pool_out, w_mlstm_out, w_out, w_router, w_gate, w_up, w_down, final_g):
    depth = w_mod.shape[0]
    for l in range(depth):
        lo2, ye, slot_t, aff_t, x1, gate2 = _layer(
            x, c, ctx, c_ctx, w_mod[l], b_mod[l], norm1_g[l], norm2_g[l], w_in[l], conv_w[l], conv_b[l],
            b_if[l], pool_mix[l], pool_scale[l], mlstm_norm_g[l], w_pool_out[l], w_mlstm_out[l], w_out[l],
            w_router[l], w_gate[l], w_up[l], w_down[l])
        x = _combine_call(lo2, ye, slot_t, aff_t, x1, gate2, final_g.reshape(1, -1), final_norm=l == depth - 1)
    return x
```

```python
import functools

import jax
import jax.numpy as jnp
import numpy as np
from jax import lax
from jax.experimental import pallas as pl
from jax.experimental.pallas import tpu as pltpu

F32 = jnp.float32
BF16 = jnp.bfloat16

GRID_W = 64
POOL_WINDOWS = (2, 4, 8, 16)
N_HEADS = 4
CONV_W = 5
N_DIRS = 2
N_EXPERTS = 16
EC_CAPACITY = 2
NORM_EPS = 1e-6
LOG2E = 1.4426950408889634

CHUNK = 256
TOKEN_TILE = 512
MERGE_TILE = 1024
HALO = 16
LANES = 128
AUG_ROWS = 16
V7X_VMEM_LIMIT_BYTES = 56 * 1024 * 1024

NN = (((1,), (0,)), ((), ()))
NT = (((1,), (1,)), ((), ()))
TN = (((0,), (0,)), ((), ()))


def _dot(a, b, dims=NN):
    return lax.dot_general(a, b, dims, preferred_element_type=F32)


def _split2(a):
    hi = a.astype(BF16)
    lo = (a - hi.astype(F32)).astype(BF16)
    return hi, lo


def _split3(a):
    a1 = a.astype(BF16)
    r1 = a - a1.astype(F32)
    a2 = r1.astype(BF16)
    a3 = (r1 - a2.astype(F32)).astype(BF16)
    return a1, a2, a3


def _dot3(a, b, dims=NN):
    ah, al = _split2(a)
    bh, bl = _split2(b)
    return _dot(ah, bh, dims) + _dot(ah, bl, dims) + _dot(al, bh, dims)


def _dot_left01(t01, a):
    a1, a2, a3 = _split3(a)
    return _dot(t01, a1) + _dot(t01, a2) + _dot(t01, a3)


def _dot_right01(a, t01):
    a1, a2, a3 = _split3(a)
    return _dot(a1, t01) + _dot(a2, t01) + _dot(a3, t01)


def _silu(x):
    return x * jax.nn.sigmoid(x)


def _log_sigmoid(x):
    return jnp.minimum(x, 0.0) - jnp.log1p(jnp.exp(-jnp.abs(x)))


def _rms_scale(x):
    return lax.rsqrt(jnp.mean(x * x, axis=-1, keepdims=True) + NORM_EPS)


def _tri01(n, kind):
    i = lax.broadcasted_iota(jnp.int32, (n, n), 0)
    j = lax.broadcasted_iota(jnp.int32, (n, n), 1)
    cond = {"le": j <= i, "ge": j >= i, "lt": j < i, "gt": j > i}[kind]
    return jnp.where(cond, 1.0, 0.0).astype(BF16)


def _shift(n):
    assert n & (n - 1) == 0, n
    return n.bit_length() - 1


def _div_pow2(x, n):
    return lax.shift_right_logical(x, _shift(n))


def _mod_pow2(x, n):
    return jnp.bitwise_and(x, n - 1)


def _params(*sem):
    return pltpu.CompilerParams(dimension_semantics=sem, vmem_limit_bytes=V7X_VMEM_LIMIT_BYTES)


def _mod_kernel(c_ref, w_ref, b_ref, o_ref):
    o_ref[...] = _dot3(_silu(c_ref[...]), w_ref[...]) + b_ref[...]


def _mod_call(cvec, w_mod, b_mod):
    rows, d = cvec.shape
    n = w_mod.shape[1]
    tn = 1536
    return pl.pallas_call(
        _mod_kernel,
        grid=(n // tn,),
        in_specs=[pl.BlockSpec((rows, d), lambda j: (0, 0)),
                  pl.BlockSpec((d, tn), lambda j: (0, j)),
                  pl.BlockSpec((1, tn), lambda j: (0, j))],
        out_specs=pl.BlockSpec((rows, tn), lambda j: (0, j)),
        out_shape=jax.ShapeDtypeStruct((rows, n), F32),
        compiler_params=_params("parallel"),
        name="mod",
    )(cvec, w_mod, b_mod)


def _ctx_kernel(ctx_ref, sh_ref, sc_ref, g_ref, wk_ref, wvt_ref, wi_ref, wf_ref, cw_ref, cb_ref, bi_ref, bf_ref,
                c_out, m_out):
    lc, d = ctx_ref.shape
    dh = d // N_HEADS
    x = ctx_ref[...]
    hc = (x * _rms_scale(x) * g_ref[...]) * (1.0 + sc_ref[...]) + sh_ref[...]
    hcb = hc.astype(BF16)

    kpre = _dot(hcb, wk_ref[...])
    pad = jnp.zeros((8, d), F32)
    kp = jnp.concatenate([pad, kpre, pad], axis=0)
    cw = cw_ref[...]
    acc = cb_ref[...] + cw[0:1, :] * kp[6:6 + lc, :]
    for j in range(1, CONV_W):
        acc = acc + cw[j:j + 1, :] * kp[6 + j:6 + j + lc, :]
    k = _silu(acc) * (dh ** -0.5)
    vt = _dot(wvt_ref[...], hcb, NT).astype(BF16)

    gi = _dot(hcb, wi_ref[...]) + bi_ref[...]
    lf = _log_sigmoid(_dot(hcb, wf_ref[...]) + bf_ref[...])
    lane = lax.broadcasted_iota(jnp.int32, lf.shape, 1)
    w_all = gi + jnp.where(lane < N_HEADS, _dot_left01(_tri01(lc, "gt"), lf), _dot_left01(_tri01(lc, "lt"), lf))
    ones = jnp.ones((AUG_ROWS, lc), BF16)
    for dr in range(N_DIRS):
        for h in range(N_HEADS):
            col = dr * N_HEADS + h
            w = w_all[:, col:col + 1]
            m = jnp.max(w, axis=0, keepdims=True)
            wk = jnp.exp(w - m) * k[:, h * dh:(h + 1) * dh]
            vt_aug = jnp.concatenate([vt[h * dh:(h + 1) * dh, :], ones], axis=0)
            c_out[dr, h] = _dot(vt_aug, wk.astype(BF16))
            m_out[dr, h] = jnp.broadcast_to(m * LOG2E, (1, LANES))


def _ctx_call(ctx, sh_c, sc_c, g1, wk, wvt, wi, wf, cw_k, cb_k, bi, bf):
    b, lc, d = ctx.shape
    dh = d // N_HEADS
    row = lambda w: pl.BlockSpec((1, w), lambda i: (0, 0))
    return pl.pallas_call(
        _ctx_kernel,
        grid=(b,),
        in_specs=[pl.BlockSpec((None, lc, d), lambda i: (i, 0, 0)),
                  row(d), row(d), row(d),
                  pl.BlockSpec((d, d), lambda i: (0, 0)),
                  pl.BlockSpec((d, d), lambda i: (0, 0)),
                  pl.BlockSpec((d, LANES), lambda i: (0, 0)),
                  pl.BlockSpec((d, LANES), lambda i: (0, 0)),
                  pl.BlockSpec((CONV_W, d), lambda i: (0, 0)),
                  row(d), row(LANES), row(LANES)],
        out_specs=[pl.BlockSpec((None, N_DIRS, N_HEADS, dh + AUG_ROWS, dh), lambda i: (i, 0, 0, 0, 0)),
                   pl.BlockSpec((None, N_DIRS, N_HEADS, 1, LANES), lambda i: (i, 0, 0, 0, 0))],
        out_shape=[jax.ShapeDtypeStruct((b, N_DIRS, N_HEADS, dh + AUG_ROWS, dh), F32),
                   jax.ShapeDtypeStruct((b, N_DIRS, N_HEADS, 1, LANES), F32)],
        compiler_params=_params("parallel"),
        name="ctx_states",
    )(ctx, sh_c, sc_c, g1, wk, wvt, wi, wf, cw_k, cb_k, bi, bf)


def _cummax_lanes(x, reverse):
    n = x.shape[-1]
    lane = lax.broadcasted_iota(jnp.int32, x.shape, x.ndim - 1)
    k = 1
    while k < n:
        if reverse:
            shifted = jnp.where(lane < n - k, pltpu.roll(x, n - k, axis=x.ndim - 1), -jnp.inf)
        else:
            shifted = jnp.where(lane >= k, pltpu.roll(x, k, axis=x.ndim - 1), -jnp.inf)
        x = jnp.maximum(x, shifted)
        k *= 2
    return x


def _proj_kernel(xp_ref, x_ref, xn_ref, sh_ref, sc_ref, g_ref,
                 wpool_ref, wqk_ref, wvt_ref, wo_ref, wg_ref, wit_ref, wft_ref,
                 cw_ref, cb_ref, bit_ref, bft_ref, *rest, part):
    if part == "qk":
        q_out, k_out, hx_scr, r_scr = rest
    else:
        u_out, vt_out, og_out, gg_out, cq_out, rows_out, hx_scr = rest
    tm, d = x_ref.shape
    dh = d // N_HEADS
    i = pl.program_id(1)
    last = pl.num_programs(1) - 1

    x_ext = jnp.concatenate([xp_ref[...], x_ref[...], xn_ref[...]], axis=0)
    hx = (x_ext * _rms_scale(x_ext) * g_ref[...]) * (1.0 + sc_ref[...]) + sh_ref[...]
    n_ext = tm + 2 * HALO
    r_id = lax.broadcasted_iota(jnp.int32, (n_ext, 1), 0)
    valid = jnp.logical_and(jnp.logical_or(i > 0, r_id >= HALO),
                            jnp.logical_or(i < last, r_id < HALO + tm))
    hx_scr[...] = jnp.where(valid, hx, 0.0).astype(BF16)
    hxc = hx_scr[HALO:HALO + tm, :]

    nc = 512
    half = CONV_W // 2

    def qk_dot(c):
        r_scr[c % 2] = _dot(hx_scr[...], wqk_ref[:, c * nc:(c + 1) * nc])

    def qk_conv(c):
        cols = slice(c * nc, (c + 1) * nc)
        r = r_scr.at[c % 2]
        cw = cw_ref[:, cols]
        acc = cb_ref[:, cols] + cw[0:1, :] * r[HALO - half:HALO - half + tm, :]
        for j in range(1, CONV_W):
            acc = acc + cw[j:j + 1, :] * r[HALO - half + j:HALO - half + j + tm, :]
        y = _silu(acc)
        if c * nc < d:
            q_out[:, cols] = y.astype(BF16)
        else:
            k_out[:, c * nc - d:(c + 1) * nc - d] = (y * (dh ** -0.5)).astype(BF16)

    def v_chunk(c):
        cols = slice(c * nc, (c + 1) * nc)
        vt_out[cols, :] = _dot(wvt_ref[cols, :], hxc, NT).astype(BF16)

    def o_chunk(c):
        cols = slice(c * nc, (c + 1) * nc)
        og_out[:, cols] = jax.nn.sigmoid(_dot(hxc, wo_ref[:, cols])).astype(BF16)

    def g_chunk(c):
        cols = slice(c * nc, (c + 1) * nc)
        gg_out[:, cols] = jax.nn.sigmoid(_dot(hxc, wg_ref[:, cols])).astype(BF16)

    if part == "qk":
        qk_dot(0); qk_dot(1)
        qk_conv(0); qk_dot(2)
        qk_conv(1); qk_dot(3)
        qk_conv(2)
        qk_conv(3)
        return
    for c in range(d // nc):
        v_chunk(c)
        o_chunk(c)
    for c in range(2 * d // nc):
        g_chunk(c)
    u_out[...] = _dot(hxc, wpool_ref[...]).astype(BF16)

    nq = N_DIRS * N_HEADS
    gi_r = _dot(wit_ref[...], hxc, NT) + bit_ref[...]
    lf_r = _log_sigmoid(_dot(wft_ref[...], hxc, NT) + bft_ref[...])
    fwd_sub = lax.broadcasted_iota(jnp.int32, (nq, CHUNK), 0) < N_HEADS
    t_le, t_ge = _tri01(CHUNK, "le"), _tri01(CHUNK, "ge")
    for j in range(tm // CHUNK):
        rows = slice(j * CHUNK, (j + 1) * CHUNK)
        b_r = jnp.where(fwd_sub, _dot_right01(lf_r[:, rows], t_ge), _dot_right01(lf_r[:, rows], t_le))
        c_r = (gi_r[:, rows] - b_r) * LOG2E
        b_r = b_r * LOG2E
        cq_out[rows, :] = c_r.T
        rows_out[0:nq, rows] = b_r
        rows_out[nq:2 * nq, rows] = jnp.where(fwd_sub, _cummax_lanes(c_r, False), _cummax_lanes(c_r, True))


def _proj_call(x, sh, sc, g1, wpool, wqk, wvt, wo, wg, wit, wft, cw, cb, bit, bft):
    b, t, d = x.shape
    tm = TOKEN_TILE
    nt = t // tm
    hb = tm // HALO
    nq = N_DIRS * N_HEADS
    per_b = pl.BlockSpec((None, 1, d), lambda bi_, i: (bi_, 0, 0))
    tile = lambda w: pl.BlockSpec((None, tm, w), lambda bi_, i: (bi_, i, 0))
    tile_t = lambda h: pl.BlockSpec((None, h, tm), lambda bi_, i: (bi_, 0, i))
    const = lambda a: pl.BlockSpec(a.shape, lambda bi_, i: (0,) * a.ndim)
    out_shapes = [jax.ShapeDtypeStruct((b, t, d // 2), BF16),
                  jax.ShapeDtypeStruct((b, t, d), BF16),
                  jax.ShapeDtypeStruct((b, t, d), BF16),
                  jax.ShapeDtypeStruct((b, d, t), BF16),
                  jax.ShapeDtypeStruct((b, t, d), BF16),
                  jax.ShapeDtypeStruct((b, t, 2 * d), BF16),
                  jax.ShapeDtypeStruct((b, t, nq), F32),
                  jax.ShapeDtypeStruct((b, 2 * nq, t), F32)]
    out_specs = [tile(d // 2), tile(d), tile(d), tile_t(d), tile(d), tile(2 * d),
                 tile(nq), tile_t(2 * nq)]
    in_specs = [pl.BlockSpec((None, HALO, d), lambda bi_, i: (bi_, jnp.maximum(i * hb - 1, 0), 0)),
                tile(d),
                pl.BlockSpec((None, HALO, d), lambda bi_, i: (bi_, jnp.minimum((i + 1) * hb, t // HALO - 1), 0)),
                per_b, per_b, const(g1),
                const(wpool), const(wqk), const(wvt), const(wo), const(wg),
                const(wit), const(wft),
                const(cw), const(cb), const(bit), const(bft)]
    args = (x, x, x, sh, sc, g1, wpool, wqk, wvt, wo, wg, wit, wft, cw, cb, bit, bft)
    hx_scratch = pltpu.VMEM((tm + 2 * HALO, d), BF16)
    qk_ids, rest_ids = (1, 2), (0, 3, 4, 5, 6, 7)
    q, k = pl.pallas_call(
        functools.partial(_proj_kernel, part="qk"),
        grid=(b, nt),
        in_specs=in_specs,
        out_specs=[out_specs[j] for j in qk_ids],
        out_shape=[out_shapes[j] for j in qk_ids],
        scratch_shapes=[hx_scratch, pltpu.VMEM((2, tm + 2 * HALO, 512), F32)],
        compiler_params=_params("parallel", "parallel"),
        name="proj_qk",
    )(*args)
    u, vt, og, gg, cq, rows = pl.pallas_call(
        functools.partial(_proj_kernel, part="rest"),
        grid=(b, nt),
        in_specs=in_specs,
        out_specs=[out_specs[j] for j in rest_ids],
        out_shape=[out_shapes[j] for j in rest_ids],
        scratch_shapes=[hx_scratch],
        compiler_params=_params("parallel", "parallel"),
        name="proj_rest",
    )(*args)
    return u, q, k, vt, og, gg, cq, rows


def _pool_kernel(u_ref, mix_ref, scale_ref, inv_ref, p_out, pad_scr):
    t, pw = u_ref.shape
    gw = pw // len(POOL_WINDOWS)
    tile = 256
    maxlo = max(POOL_WINDOWS) // 2
    padr = maxlo * GRID_W
    ti = lax.broadcasted_iota(jnp.int32, (tile, tile), 0)
    tj = lax.broadcasted_iota(jnp.int32, (tile, tile), 1)
    same_row = _div_pow2(ti, GRID_W) == _div_pow2(tj, GRID_W)
    ci, cj = _mod_pow2(ti, GRID_W), _mod_pow2(tj, GRID_W)

    def span(dlt, ext):
        return pad_scr[padr + (dlt - ext) * GRID_W:padr + (dlt + ext) * GRID_W + t, :]

    for g, side in enumerate(POOL_WINDOWS):
        lo, hi = side // 2, side - side // 2
        assert lo == hi and side & (side - 1) == 0
        cols = slice(g * gw, (g + 1) * gw)
        pad_scr[0:padr, :] = jnp.zeros((padr, gw), F32)
        pad_scr[padr + t:padr + t + padr, :] = jnp.zeros((padr, gw), F32)
        band = jnp.logical_and(same_row, jnp.logical_and(cj >= ci - lo, cj < ci + hi))
        pw01 = jnp.where(band, 1.0, 0.0).astype(BF16)
        for k in range(t // tile):
            rs = slice(k * tile, (k + 1) * tile)
            pad_scr[padr + k * tile:padr + (k + 1) * tile, :] = _dot(pw01, u_ref[rs, cols])
        ext = (side - 2) // 2
        tot = span(-1, ext) + span(0, ext)
        k = 2
        while k < side:
            pad_scr[padr - ext * GRID_W:padr + ext * GRID_W + t, :] = tot
            ext = (side - 2 * k) // 2
            tot = span(-(k // 2), ext) + span(k // 2, ext)
            k *= 2
        a = tot * inv_ref[g] - u_ref[:, cols].astype(F32)
        p = _dot(a.astype(BF16), mix_ref[g]) * scale_ref[:, cols]
        p_out[:, cols] = p.astype(BF16)


def _pool_inv_counts(t, gw):
    rows = t // GRID_W
    r, c = np.arange(t) // GRID_W, np.arange(t) % GRID_W
    out = []
    for side in POOL_WINDOWS:
        lo, hi = side // 2, side - side // 2
        cnt = ((np.minimum(r + hi, rows) - np.maximum(r - lo, 0))
               * (np.minimum(c + hi, GRID_W) - np.maximum(c - lo, 0)))
        out.append(np.broadcast_to((1.0 / cnt).astype(np.float32)[:, None], (t, gw)))
    return jnp.asarray(np.stack(out))


def _pool_call(u, mix, scale):
    b, t, pw = u.shape
    ng = len(POOL_WINDOWS)
    gw = pw // ng
    padr = (max(POOL_WINDOWS) // 2) * GRID_W
    return pl.pallas_call(
        _pool_kernel,
        grid=(b,),
        in_specs=[pl.BlockSpec((None, t, pw), lambda i: (i, 0, 0)),
                  pl.BlockSpec(mix.shape, lambda i: (0, 0, 0)),
                  pl.BlockSpec((1, pw), lambda i: (0, 0)),
                  pl.BlockSpec((ng, t, gw), lambda i: (0, 0, 0))],
        out_specs=pl.BlockSpec((None, t, pw), lambda i: (i, 0, 0)),
        out_shape=jax.ShapeDtypeStruct((b, t, pw), BF16),
        scratch_shapes=[pltpu.VMEM((t + 2 * padr, gw), F32)],
        compiler_params=_params("parallel"),
        name="pool",
    )(u, mix, scale, _pool_inv_counts(t, gw))


def _mlstm_dir(q_ref, k_ref, vt_ref, cq_ref, rows_ref, c_scr, m_scr, reverse):
    L, d = q_ref.shape
    dh = d // N_HEADS
    nq = N_DIRS * N_HEADS
    si = lax.broadcasted_iota(jnp.int32, (L, L), 0)
    tj = lax.broadcasted_iota(jnp.int32, (L, L), 1)
    mask = (si >= tj) if reverse else (si <= tj)
    ones = jnp.ones((AUG_ROWS, L), BF16)
    end = 0 if reverse else L - 1
    off = N_HEADS if reverse else 0

    hs_all = []
    for h in range(N_HEADS):
        hs = slice(h * dh, (h + 1) * dh)
        st = off + h
        q = q_ref[:, hs]
        k = k_ref[:, hs]
        vt_aug = jnp.concatenate([vt_ref[hs, :], ones], axis=0)
        c_c = cq_ref[:, st:st + 1]
        b_r = rows_ref[st:st + 1, :]
        cm_r = rows_ref[nq + st:nq + st + 1, :]
        m_prev = m_scr[st][:, 0:1]
        ct_prev = c_scr[st]

        mm = jnp.maximum(cm_r, m_prev)
        w_inter = jnp.exp2(m_prev - mm)
        st_mat = (_dot(k, q, NT) * jnp.exp2(jnp.where(mask, c_c - mm, -jnp.inf))).astype(BF16)
        intra = _dot(vt_aug, st_mat)
        inter = _dot(ct_prev.astype(BF16), q, NT)
        den = w_inter * inter[dh:dh + 1, :] + intra[dh:dh + 1, :]
        inv = 1.0 / jnp.maximum(jnp.abs(den), jnp.exp2(-(b_r + mm)))
        hs_all.append((w_inter * inter[0:dh, :] + intra[0:dh, :]) * inv)

        g_tot = b_r[:, end:end + 1]
        m_new = g_tot + jnp.maximum(m_prev, cm_r[:, end:end + 1])
        decay = jnp.exp2(g_tot + m_prev - m_new)
        wk = (k.astype(F32) * jnp.exp2(g_tot + c_c - m_new)).astype(BF16)
        c_scr[st] = decay * ct_prev + _dot(vt_aug, wk)
        m_scr[st] = jnp.broadcast_to(m_new, (1, LANES))
    return hs_all


def _mlstm_kernel(qf_ref, kf_ref, vf_ref, ogf_ref, cqf_ref, rwf_ref,
                  qb_ref, kb_ref, vb_ref, ogb_ref, cqb_ref, rwb_ref,
                  c0_ref, m0_ref, ng_ref, o_ref,
                  c_scr, m_scr, hf_scr, hb_scr):
    L, d = qf_ref.shape
    dh = d // N_HEADS
    s = pl.program_id(1)
    nch = pl.num_programs(1)
    half = nch // 2

    @pl.when(s == 0)
    def _():
        for j in range(N_DIRS * N_HEADS):
            c_scr[j] = c0_ref[j // N_HEADS, j % N_HEADS]
            m_scr[j] = m0_ref[j // N_HEADS, j % N_HEADS]

    h_f = _mlstm_dir(qf_ref, kf_ref, vf_ref, cqf_ref, rwf_ref, c_scr, m_scr, False)
    h_b = _mlstm_dir(qb_ref, kb_ref, vb_ref, cqb_ref, rwb_ref, c_scr, m_scr, True)

    @pl.when(s < half)
    def _():
        for h in range(N_HEADS):
            hs = slice(h * dh, (h + 1) * dh)
            hf_scr[s, hs, :] = h_f[h]
            hb_scr[half - 1 - s, hs, :] = h_b[h]

    @pl.when(s >= half)
    def _():
        def finish(ht, h, og_ref, out):
            hs = slice(h * dh, (h + 1) * dh)
            scale = lax.rsqrt(jnp.mean(ht * ht, axis=0, keepdims=True) + NORM_EPS)
            y = jnp.concatenate([ht[:, i * LANES:(i + 1) * LANES] * scale[:, i * LANES:(i + 1) * LANES]
                                 * ng_ref[hs, :] for i in range(L // LANES)], axis=1).T
            out[:, hs] = (y * og_ref[:, hs].astype(F32)).astype(BF16)

        for h in range(N_HEADS):
            hs = slice(h * dh, (h + 1) * dh)
            finish(h_f[h] + hb_scr[s - half, hs, :], h, ogf_ref, o_ref.at[1])
            finish(h_b[h] + hf_scr[nch - 1 - s, hs, :], h, ogb_ref, o_ref.at[0])


def _mlstm_call(q, k, vt, og, cq, rows, c0, m0, norm_g):
    b, t, d = q.shape
    dh = d // N_HEADS
    L = CHUNK
    nch = t // L
    half = nch // 2
    assert nch % 2 == 0
    nq = N_DIRS * N_HEADS

    def specs(chunk):
        seq = lambda w: pl.BlockSpec((None, L, w), lambda bi, s: (bi, chunk(s), 0))
        seq_t = lambda h: pl.BlockSpec((None, h, L), lambda bi, s: (bi, 0, chunk(s)))
        return [seq(d), seq(d), seq_t(d), seq(d), seq(nq), seq_t(2 * nq)]

    state = lambda w0, w1: pl.BlockSpec((None, N_DIRS, N_HEADS, w0, w1), lambda bi, s: (bi, 0, 0, 0, 0))
    return pl.pallas_call(
        _mlstm_kernel,
        grid=(b, nch),
        in_specs=specs(lambda s: s) + specs(lambda s: nch - 1 - s) + [
            state(dh + AUG_ROWS, dh), state(1, LANES), pl.BlockSpec((d, LANES), lambda bi, s: (0, 0))],
        out_specs=pl.BlockSpec((None, 2, None, L, d), lambda bi, s: (bi, 0, jnp.maximum(s - half, 0), 0, 0)),
        out_shape=jax.ShapeDtypeStruct((b, 2, half, L, d), BF16),
        scratch_shapes=[pltpu.VMEM((N_DIRS * N_HEADS, dh + AUG_ROWS, dh), F32),
                        pltpu.VMEM((N_DIRS * N_HEADS, 1, LANES), F32),
                        pltpu.VMEM((half, d, L), F32),
                        pltpu.VMEM((half, d, L), F32)],
        compiler_params=_params("parallel", "arbitrary"),
        name="mlstm",
    )(q, k, vt, og, cq, rows, q, k, vt, og, cq, rows, c0, m0, norm_g)


def _merge_kernel(p_ref, m_ref, gg_ref, x_ref, g1_ref, sh2_ref, sc2_ref, n2_ref,
                  wpo_ref, wmo_ref, wout_ref, wr_ref, x1_out, h2_out, aff_out):
    tm, d = x_ref.shape
    cpt = m_ref.shape[0]
    L = m_ref.shape[1]
    upper = pl.program_id(1) >= pl.num_programs(1) // 2
    sub = TOKEN_TILE
    cps = sub // L
    streams = [slice(r * sub, (r + 1) * sub) for r in range(tm // sub)]

    def branches(r):
        m = jnp.where(upper,
                      jnp.concatenate([m_ref[r * cps + j] for j in range(cps)], axis=0),
                      jnp.concatenate([m_ref[cpt - 1 - r * cps - j] for j in range(cps)], axis=0))
        return _dot(p_ref[streams[r], :], wpo_ref[...]), _dot(m, wmo_ref[...])

    def mix(r, a, mm):
        rows = streams[r]
        return (gg_ref[rows, 0:d].astype(F32) * a + gg_ref[rows, d:2 * d].astype(F32) * mm).astype(BF16)

    def residual(r, mixed):
        x1 = x_ref[streams[r], :] + g1_ref[...] * _dot(mixed, wout_ref[...])
        x1_out[streams[r], :] = x1
        return x1

    def tail(r, x1):
        rows = streams[r]
        h2 = (x1 * _rms_scale(x1) * n2_ref[...]) * (1.0 + sc2_ref[...]) + sh2_ref[...]
        h2_out[rows, :] = h2.astype(BF16)
        logits = _dot3(wr_ref[...], h2, NT)
        z = jnp.exp(logits - jnp.max(logits, axis=0, keepdims=True))
        aff_out[:, rows] = z / jnp.sum(z, axis=0, keepdims=True)

    n = len(streams)
    ab = [branches(r) for r in range(n)]
    mixed = [mix(r, *ab[r]) for r in range(n)]
    x1s = [residual(r, mixed[r]) for r in range(n)]
    for r in range(n):
        tail(r, x1s[r])


def _merge_call(p, m, gg, x, gate1, sh2, sc2, n2, wpo, wmo, wout, wr_t):
    b, t, d = x.shape
    tm = min(MERGE_TILE, t // 2)
    e = wr_t.shape[0]
    per_b = pl.BlockSpec((None, 1, d), lambda bi, i: (bi, 0, 0))
    tile = lambda w: pl.BlockSpec((None, tm, w), lambda bi, i: (bi, i, 0))
    const = lambda a: pl.BlockSpec(a.shape, lambda bi, i: (0,) * a.ndim)
    cpt = tm // CHUNK
    nth = t // tm // 2
    m_spec = pl.BlockSpec((None, None, cpt, CHUNK, d),
                          lambda bi, i: (bi, i // nth, jnp.where(i >= nth, i - nth, nth - 1 - i), 0, 0))
    return pl.pallas_call(
        _merge_kernel,
        grid=(b, t // tm),
        in_specs=[tile(d // 2), m_spec, tile(2 * d), tile(d), per_b, per_b, per_b, const(n2),
                  const(wpo), const(wmo), const(wout), const(wr_t)],
        out_specs=[tile(d), tile(d), pl.BlockSpec((None, e, tm), lambda bi, i: (bi, 0, i))],
        out_shape=[jax.ShapeDtypeStruct((b, t, d), F32),
                   jax.ShapeDtypeStruct((b, t, d), BF16),
                   jax.ShapeDtypeStruct((b, e, t), F32)],
        compiler_params=_params("parallel", "parallel"),
        name="merge",
    )(p, m, gg, x, gate1, sh2, sc2, n2, wpo, wmo, wout, wr_t)


def _route_kernel(aff_ref, slot_out, lo_out, *, cap):
    e, t = aff_ref.shape
    aff = aff_ref[...]

    def step(i, thr):
        cand = thr | (jnp.int32(1) << (30 - i))
        cnt = jnp.sum(jnp.where(aff >= pltpu.bitcast(cand, F32), 1.0, 0.0), axis=-1, keepdims=True)
        return jnp.where(cnt >= cap, cand, thr)

    thr = pltpu.bitcast(lax.fori_loop(0, 31, step, jnp.zeros((e, 1), jnp.int32)), F32)
    gt = aff > thr
    eq = aff == thr
    need = cap - jnp.sum(jnp.where(gt, 1.0, 0.0), axis=-1, keepdims=True).astype(jnp.int32)

    seg = 256
    t_ge = _tri01(seg, "ge")

    def prefix_incl(x01):
        outs, carries, carry = [], [], jnp.zeros((e, 1), F32)
        for j in range(t // seg):
            p = _dot(x01[:, j * seg:(j + 1) * seg].astype(BF16), t_ge) + carry
            outs.append(p)
            carry = p[:, seg - 1:seg]
            carries.append(carry)
        return jnp.concatenate(outs, axis=1), carries

    eq_f = jnp.where(eq, 1.0, 0.0)
    tie_rank = (prefix_incl(eq_f)[0] - eq_f).astype(jnp.int32)
    sel = jnp.logical_or(gt, jnp.logical_and(eq, tie_rank < need))
    rank, carries = prefix_incl(jnp.where(sel, 1.0, 0.0))
    slot_out[...] = jnp.where(sel, rank.astype(jnp.int32) - 1, -1)

    lane = lax.broadcasted_iota(jnp.int32, (e, LANES), 1)
    lo = jnp.zeros((e, LANES), F32)
    per_tile = GATHER_TILE // seg
    for c in range(1, t // GATHER_TILE + 1):
        lo = jnp.where(lane == c, carries[c * per_tile - 1], lo)
    lo_out[...] = lo.astype(jnp.int32)


def _route_call(aff_t, cap):
    b, e, t = aff_t.shape
    n = b * e
    slot, lo = pl.pallas_call(
        functools.partial(_route_kernel, cap=cap),
        grid=(1,),
        in_specs=[pl.BlockSpec((n, t), lambda i: (0, 0))],
        out_specs=[pl.BlockSpec((n, t), lambda i: (0, 0)),
                   pl.BlockSpec((n, LANES), lambda i: (0, 0))],
        out_shape=[jax.ShapeDtypeStruct((n, t), jnp.int32),
                   jax.ShapeDtypeStruct((n, LANES), jnp.int32)],
        compiler_params=_params("arbitrary"),
        name="route",
    )(aff_t.reshape(n, t))
    return slot.reshape(b, e, t), lo.reshape(b, e, LANES)


SLOT_WINDOW = 96
GATHER_TILE = 256
GATHER_WINDOW = 64
SLOT_ALIGN = 16
EXPERT_GROUP = 8


def _aligned(lo):
    return jnp.bitwise_and(lo, -SLOT_ALIGN)


def _window_start(nominal, cap, w):
    return pl.multiple_of(jnp.minimum(nominal, cap - w), SLOT_ALIGN)


def _n_windows(lo, hi, w):
    return lax.div(hi - _aligned(lo) + (w - 1), w)


def _gather_kernel(lo_ref, h2_ref, slot_ref, xe_out, *, cap, n_tiles):
    n_exp = slot_ref.shape[0]
    tc, w = GATHER_TILE, GATHER_WINDOW
    b = pl.program_id(0)
    stride = n_tiles + 1
    xe_out[...] = jnp.zeros(xe_out.shape, BF16)
    s_id = lax.broadcasted_iota(jnp.int32, (w, tc), 0)

    def add_rows(e, start, z):
        xe_out[e, pl.ds(start, w), :] = xe_out[e, pl.ds(start, w), :] + z.astype(BF16)

    def tile_body(c, carry):
        t0 = pl.multiple_of(c * tc, tc)
        for g0 in range(0, n_exp, EXPERT_GROUP):
            starts, blocks = [], []
            for e in range(g0, g0 + EXPERT_GROUP):
                a0 = _window_start(_aligned(lo_ref[b, e * stride + c]), cap, w)
                hit = (s_id + a0) == slot_ref[e:e + 1, pl.ds(t0, tc)]
                blocks.append(jnp.where(hit, 1.0, 0.0).astype(BF16))
                starts.append(a0)
            z = _dot(jnp.concatenate(blocks, axis=0), h2_ref[pl.ds(t0, tc), :])
            for j in range(EXPERT_GROUP):
                add_rows(g0 + j, starts[j], z[j * w:(j + 1) * w, :])
        for e in range(n_exp):
            lo, hi = lo_ref[b, e * stride + c], lo_ref[b, e * stride + c + 1]

            def window_body(k, carry2, e=e, lo=lo):
                nominal = _aligned(lo) + k * w
                a = _window_start(nominal, cap, w)
                srow = slot_ref[e:e + 1, pl.ds(t0, tc)]
                hit = jnp.logical_and((s_id + a) == srow, srow >= nominal)
                add_rows(e, a, _dot(jnp.where(hit, 1.0, 0.0).astype(BF16), h2_ref[pl.ds(t0, tc), :]))
                return carry2

            lax.fori_loop(1, _n_windows(lo, hi, w), window_body, 0)
        return carry

    lax.fori_loop(0, n_tiles, tile_body, 0)


def _gather_call(lo2, h2, slot_t, cap):
    b, t, d = h2.shape
    e = slot_t.shape[1]
    n_tiles = t // GATHER_TILE
    grid_spec = pltpu.PrefetchScalarGridSpec(
        num_scalar_prefetch=1,
        grid=(b,),
        in_specs=[pl.BlockSpec((None, t, d), lambda i, lo: (i, 0, 0)),
                  pl.BlockSpec((None, e, t), lambda i, lo: (i, 0, 0))],
        out_specs=pl.BlockSpec((None, e, cap, d), lambda i, lo: (i, 0, 0, 0)),
    )
    return pl.pallas_call(
        functools.partial(_gather_kernel, cap=cap, n_tiles=n_tiles),
        grid_spec=grid_spec,
        out_shape=jax.ShapeDtypeStruct((b, e, cap, d), BF16),
        compiler_params=_params("arbitrary"),
        name="gather",
    )(lo2, h2, slot_t)


def _expert_kernel(xe_ref, wg_ref, wu_ref, wd_ref, ye_out, wg_scr, wu_scr, wd_scr):
    e, b = pl.program_id(0), pl.program_id(1)
    n_exp = pl.num_programs(0) - 1
    slab = wg_ref.shape[0]
    f = wg_scr.shape[2]
    ns, cap, d = xe_ref.shape

    @pl.when(e < n_exp)
    def _():
        slot = lax.rem(e, 2)
        rows = pl.ds(pl.multiple_of(b * slab, slab), slab)
        wg_scr[slot, rows, :] = wg_ref[...].astype(BF16)
        wu_scr[slot, rows, :] = wu_ref[...].astype(BF16)
        wd_scr[slot, rows, :] = wd_ref[...].astype(BF16)

    @pl.when(e > 0)
    def _():
        slot = lax.rem(e + 1, 2)
        xe = xe_ref[...].reshape(ns * cap, d)
        fc = 512
        y = None
        for c in range(f // fc):
            cols = slice(c * fc, (c + 1) * fc)
            hid = _silu(_dot(xe, wg_scr[slot, :, cols])) * _dot(xe, wu_scr[slot, :, cols])
            part = _dot(hid.astype(BF16), wd_scr[slot, cols, :])
            y = part if y is None else y + part
        ye_out[...] = y.astype(BF16).reshape(ns, cap, d)


def _expert_call(xe, wg, wu, wd):
    b, e, cap, d = xe.shape
    f = wg.shape[2]
    ns = 2 if b % 2 == 0 else 1
    steps = b // ns
    assert d % steps == 0 and f % steps == 0
    w_spec = lambda rows, cols: pl.BlockSpec((None, rows // steps, cols),
                                             lambda ei, bi: (jnp.minimum(ei, e - 1), bi, 0))
    return pl.pallas_call(
        _expert_kernel,
        grid=(e + 1, steps),
        in_specs=[pl.BlockSpec((ns, None, cap, d), lambda ei, bi: (bi, jnp.maximum(ei - 1, 0), 0, 0)),
                  w_spec(d, f), w_spec(d, f), w_spec(f, d)],
        out_specs=pl.BlockSpec((ns, None, cap, d),
                               lambda ei, bi: (jnp.where(ei == 0, 0, bi), jnp.maximum(ei - 1, 0), 0, 0)),
        out_shape=jax.ShapeDtypeStruct((b, e, cap, d), BF16),
        scratch_shapes=[pltpu.VMEM((2, d, f), BF16), pltpu.VMEM((2, d, f), BF16), pltpu.VMEM((2, f, d), BF16)],
        compiler_params=_params("arbitrary", "arbitrary"),
        name="experts",
    )(xe, wg, wu, wd)


def _combine_kernel(lo_ref, ye_ref, slot_ref, aff_ref, x1_ref, g2_ref, fg_ref, o_ref, acc_scr,
                    *, final_norm, n_tiles):
    n_exp, cap, d = ye_ref.shape
    tm, w = x1_ref.shape[0], SLOT_WINDOW
    b, i = pl.program_id(0), pl.program_id(1)
    per = tm // GATHER_TILE
    stride = n_tiles * per + 1
    s_id = lax.broadcasted_iota(jnp.int32, (w, tm), 0)

    for g0 in range(0, n_exp, EXPERT_GROUP):
        ps, rows = [], []
        for e in range(g0, g0 + EXPERT_GROUP):
            a0 = _window_start(_aligned(lo_ref[b, e * stride + i * per]), cap, w)
            hit = (s_id + a0) == slot_ref[e:e + 1, :]
            ps.append(jnp.where(hit, aff_ref[e:e + 1, :], 0.0).astype(BF16))
            rows.append(ye_ref[e, pl.ds(a0, w), :])
        part = _dot(jnp.concatenate(ps, axis=0), jnp.concatenate(rows, axis=0), TN)
        if g0 == 0:
            acc_scr[...] = part
        else:
            acc_scr[...] += part

    for e in range(n_exp):
        lo, hi = lo_ref[b, e * stride + i * per], lo_ref[b, e * stride + (i + 1) * per]

        def window_body(k, carry, e=e, lo=lo):
            nominal = _aligned(lo) + k * w
            a = _window_start(nominal, cap, w)
            sr = slot_ref[e:e + 1, :]
            hit = jnp.logical_and((s_id + a) == sr, sr >= nominal)
            p = jnp.where(hit, aff_ref[e:e + 1, :], 0.0).astype(BF16)
            acc_scr[...] += _dot(p, ye_ref[e, pl.ds(a, w), :], TN)
            return carry

        lax.fori_loop(1, _n_windows(lo, hi, w), window_body, 0)

    x2 = x1_ref[...] + g2_ref[...] * acc_scr[...]
    o_ref[...] = x2 * _rms_scale(x2) * fg_ref[...] if final_norm else x2


def _combine_call(lo2, ye, slot_t, aff_t, x1, gate2, final_g, final_norm):
    b, t, d = x1.shape
    e, cap = ye.shape[1], ye.shape[2]
    tm = TOKEN_TILE
    grid_spec = pltpu.PrefetchScalarGridSpec(
        num_scalar_prefetch=1,
        grid=(b, t // tm),
        in_specs=[pl.BlockSpec((None, e, cap, d), lambda bi, i, lo: (bi, 0, 0, 0)),
                  pl.BlockSpec((None, e, tm), lambda bi, i, lo: (bi, 0, i)),
                  pl.BlockSpec((None, e, tm), lambda bi, i, lo: (bi, 0, i)),
                  pl.BlockSpec((None, tm, d), lambda bi, i, lo: (bi, i, 0)),
                  pl.BlockSpec((None, 1, d), lambda bi, i, lo: (bi, 0, 0)),
                  pl.BlockSpec((1, d), lambda bi, i, lo: (0, 0))],
        out_specs=pl.BlockSpec((None, tm, d), lambda bi, i, lo: (bi, i, 0)),
        scratch_shapes=[pltpu.VMEM((tm, d), F32)],
    )
    return pl.pallas_call(
        functools.partial(_combine_kernel, final_norm=final_norm, n_tiles=t // tm),
        grid_spec=grid_spec,
        out_shape=jax.ShapeDtypeStruct((b, t, d), F32),
        compiler_params=_params("parallel", "arbitrary"),
        name="combine",
    )(lo2, ye, slot_t, aff_t, x1, gate2, final_g)


def _layer(x, c, ctx, c_ctx, w_mod, b_mod, norm1_g, norm2_g, w_in, conv_w, conv_b, b_if,
           pool_mix, pool_scale, mlstm_norm_g, w_pool_out, w_mlstm_out, w_out,
           w_router, w_gate, w_up, w_down):
    b, t, d = x.shape
    pw = d // 2
    ng = N_DIRS * 2 * N_HEADS
    q_off, k_off, v_off, o_off = pw, pw + d, pw + 2 * d, pw + 3 * d
    if_off, gate_off = pw + 4 * d, pw + 4 * d + ng
    cap = EC_CAPACITY * t // N_EXPERTS
    row = lambda a: a.reshape(1, -1)

    rows = -(-(b + 1) // 8) * 8
    cvec = jnp.zeros((rows, d), F32).at[:b].set(c).at[b].set(c_ctx)
    mod = _mod_call(cvec, w_mod, row(b_mod))
    shift1, scale1, gate1, shift2, scale2, gate2 = [
        mod[:b, j * d:(j + 1) * d].reshape(b, 1, d) for j in range(6)]
    shift_c, scale_c = mod[b:b + 1, 0:d], mod[b:b + 1, d:2 * d]

    w_in_b = w_in.astype(BF16)
    nq = N_DIRS * N_HEADS
    w_if3 = w_in_b[:, if_off:gate_off].reshape(d, N_DIRS, 2, N_HEADS)
    b_if3 = b_if.reshape(N_DIRS, 2, N_HEADS)
    w_i, w_f = w_if3[:, :, 0, :].reshape(d, nq), w_if3[:, :, 1, :].reshape(d, nq)
    b_i, b_f = b_if3[:, 0, :].reshape(nq), b_if3[:, 1, :].reshape(nq)
    pad_w = lambda w: jnp.zeros((d, LANES), BF16).at[:, :nq].set(w)
    pad_b = lambda v: jnp.zeros((1, LANES), F32).at[0, :nq].set(v)

    w_vt = w_in_b[:, v_off:o_off].T
    c0, m0 = _ctx_call(ctx, shift_c, scale_c, row(norm1_g),
                       w_in_b[:, k_off:v_off], w_vt, pad_w(w_i), pad_w(w_f),
                       conv_w[:, d:], row(conv_b[d:]), pad_b(b_i), pad_b(b_f))

    u, q, k, vt, og, gg, cq, rows = _proj_call(
        x, shift1, scale1, row(norm1_g),
        w_in_b[:, 0:q_off], w_in_b[:, q_off:v_off], w_vt, w_in_b[:, o_off:if_off],
        w_in_b[:, gate_off:], w_i.T, w_f.T,
        conv_w, row(conv_b), b_i.reshape(nq, 1), b_f.reshape(nq, 1))

    p = _pool_call(u, pool_mix.astype(BF16), row(pool_scale))
    m = _mlstm_call(q, k, vt, og, cq, rows, c0, m0, jnp.broadcast_to(mlstm_norm_g[:, None], (d, LANES)))

    x1, h2, aff_t = _merge_call(p, m, gg, x, gate1, shift2, scale2, row(norm2_g),
                                w_pool_out.astype(BF16), w_mlstm_out.astype(BF16), w_out.astype(BF16),
                                w_router.T)
    slot_t, lo = _route_call(aff_t, cap)
    lo2 = lo[:, :, :t // GATHER_TILE + 1].reshape(b, -1)
    xe = _gather_call(lo2, h2, slot_t, cap)
    ye = _expert_call(xe, w_gate, w_up, w_down)
    return lo2, ye, slot_t, aff_t, x1, gate2


def kernel(x, c, ctx, c_ctx, w_mod, b_mod, norm1_g, norm2_g, w_in, conv_w, conv_b, b_if, pool_mix, pool_scale,
           mlstm_norm_g, w_pool_out, w_mlstm_out, w_out, w_router, w_gate, w_up, w_down, final_g):
    depth = w_mod.shape[0]
    for l in range(depth):
        lo2, ye, slot_t, aff_t, x1, gate2 = _layer(
            x, c, ctx, c_ctx, w_mod[l], b_mod[l], norm1_g[l], norm2_g[l], w_in[l], conv_w[l], conv_b[l],
            b_if[l], pool_mix[l], pool_scale[l], mlstm_norm_g[l], w_pool_out[l], w_mlstm_out[l], w_out[l],
            w_router[l], w_gate[l], w_up[l], w_down[l])
        x = _combine_call(lo2, ye, slot_t, aff_t, x1, gate2, final_g.reshape(1, -1), final_norm=l == depth - 1)
    return x
```
